```python
import jax
import jax.numpy as jnp
from jax import lax
import numpy as np

D_MODEL = 1024
BATCH = 32
SEQ = 2048
DEPTH = 2

GRID_W = 64
CTX_LEN = 256

MIX_DIM = D_MODEL
RW_HEAD_DIM = 64
RW_DIM = MIX_DIM // 4
RW_HEADS = RW_DIM // RW_HEAD_DIM
RW_W_LORA = 64
RW_A_LORA = 64
RW_G_LORA = 128
RW_GN_EPS = 64e-5
RW_COLS = 3 * RW_DIM + 2 * RW_W_LORA + 2 * RW_A_LORA + RW_G_LORA
MLA_DIM = MIX_DIM // 2
MLA_V_DIM = 128
MLA_HEADS = MLA_DIM // MLA_V_DIM
MLA_NOPE_DIM = 128
MLA_ROPE_DIM = 64
MLA_Q_LORA = 256
MLA_KV_LORA = 128
MLA_COLS = MLA_Q_LORA + MLA_KV_LORA + MLA_ROPE_DIM
NA_DIM = MIX_DIM - RW_DIM - MLA_DIM
NA_HEAD_DIM = 64
NA_HEADS = NA_DIM // NA_HEAD_DIM
NA_WIN_R = 8
NA_WIN_C = 16
NA_COLS = 3 * NA_DIM
IN_COLS = RW_COLS + MLA_COLS + NA_COLS

ROPE_BASE = 10000.0
Q_BLOCK = 128
N_EXPERTS = 32
TOP_K = 4
D_FF = D_MODEL
SWIGLU_ALPHA = 1.702
SWIGLU_LIMIT = 7.0
MOE_BLOCK = 512
DN_ALPHA = (2 * DEPTH) ** 0.25
DN_BETA = (8 * DEPTH) ** -0.25
NEG_INF = -1e30

kernel_name = 'hybrid_rwkv7_mla_natten_moe_dit'


def split_cols(x, sizes):
    idx = [int(i) for i in np.cumsum(sizes)[:-1]]
    return jnp.split(x, idx, axis=-1)


def layer_norm(x, g=None, b=None, eps=1e-5):
    xf = x.astype(jnp.float32)
    mu = xf.mean(-1, keepdims=True)
    var = jnp.square(xf - mu).mean(-1, keepdims=True)
    y = (xf - mu) * lax.rsqrt(var + eps)
    if g is not None:
        y = y * g + b
    return y.astype(x.dtype)


def rms_norm(x, g, eps=1e-6):
    xf = x.astype(jnp.float32)
    y = xf * lax.rsqrt(jnp.mean(xf * xf, -1, keepdims=True) + eps) * g
    return y.astype(x.dtype)


def modulate(x, shift, scale):
    return x * (1 + scale) + shift


def axial_rope(n_tokens, rot_dim):
    t = jnp.arange(n_tokens)
    row = (t // GRID_W).astype(jnp.float32)
    col = (t % GRID_W).astype(jnp.float32)
    n_freq = rot_dim // 4
    inv = ROPE_BASE ** (-jnp.arange(n_freq, dtype=jnp.float32) / n_freq)
    ang = jnp.concatenate([row[:, None] * inv, col[:, None] * inv], axis=-1)
    return jnp.cos(ang), jnp.sin(ang)


def apply_rope(x, cos, sin):
    xp = x.reshape(x.shape[:-1] + (-1, 2))
    xe, xo = xp[..., 0], xp[..., 1]
    y = jnp.stack([xe * cos - xo * sin, xe * sin + xo * cos], axis=-1)
    return y.reshape(x.shape).astype(x.dtype)


def attend(q, k, v, scale):
    s = jnp.einsum('bqhd,bkhd->bhqk', q, k, preferred_element_type=jnp.float32) * scale
    p = jax.nn.softmax(s, axis=-1).astype(v.dtype)
    o = jnp.einsum('bhqk,bkhd->bqhd', p, v)
    return o.reshape(o.shape[:2] + (-1,))


def token_shift(p, mu):
    zero = jnp.zeros_like(p[:, :1])
    prev = jnp.concatenate([zero, p[:, :-1]], axis=1)
    nxt = jnp.concatenate([p[:, 1:], zero], axis=1)
    return p + mu[0] * (prev - p) + mu[1] * (nxt - p)


def rwkv_features(p, mu, w0, w2, a0, a2, k_k, k_a):
    B, T, _ = p.shape
    r, k, v, wf, wb, af, ab, g_pre = split_cols(
        token_shift(p, mu), [RW_DIM] * 3 + [RW_W_LORA] * 2 + [RW_A_LORA] * 2 + [RW_G_LORA])
    hd = lambda z: z.reshape(B, T, RW_HEADS, RW_HEAD_DIM)
    kk = hd(k * k_k).astype(jnp.float32)
    kk = kk * lax.rsqrt(jnp.sum(kk * kk, -1, keepdims=True) + 1e-12)
    dirs = []
    for d, (w_lo, a_lo) in enumerate(((wf, af), (wb, ab))):
        logw = -jax.nn.softplus(-(w0[d] + jnp.tanh(w_lo) @ w2[d]).astype(jnp.float32)) - 0.5
        decay = jnp.exp(-jnp.exp(logw))
        a = jax.nn.sigmoid((a0[d] + a_lo @ a2[d]).astype(jnp.float32))
        k_d = k * (1 + (a - 1) * k_a)
        dirs.append((hd(decay), hd(k_d), hd(a) * kk))
    return hd(r), hd(k), hd(v), g_pre, kk, dirs


def rwkv_scan(S0, decay, k, v, kk, b, r=None, reverse=False):
    emit = r is not None
    tm = lambda z: jnp.moveaxis(z.astype(jnp.float32), 1, 0)
    xs = (tm(decay), tm(k), tm(v), tm(kk), tm(b)) + ((tm(r),) if emit else ())

    def step(S, inp):
        w_t, k_t, v_t, kk_t, b_t = inp[:5]
        s_a = jnp.einsum('bhvk,bhk->bhv', S, kk_t)
        S = S * w_t[:, :, None, :] - s_a[..., None] * b_t[:, :, None, :] + v_t[..., None] * k_t[:, :, None, :]
        o = jnp.einsum('bhvk,bhk->bhv', S, inp[5]) if emit else None
        return S, o

    S, o = lax.scan(step, S0, xs, reverse=reverse)
    return S, (jnp.moveaxis(o, 0, 1) if emit else None)


def rwkv_output(o, r, k, v, g_pre, g2, r_k, gn_w, gn_b):
    B, T = o.shape[:2]
    mu = o.mean(-1, keepdims=True)
    var = jnp.square(o - mu).mean(-1, keepdims=True)
    on = ((o - mu) * lax.rsqrt(var + RW_GN_EPS)).reshape(B, T, RW_DIM) * gn_w + gn_b
    bonus = jnp.sum(r * k * r_k, -1, keepdims=True) * v
    g = jax.nn.sigmoid(g_pre) @ g2
    return ((on + bonus.reshape(B, T, RW_DIM)) * g).astype(r.dtype)


def rwkv_mixer(p_ctx, p_lat, mu, w0, w2, a0, a2, g2, k_k, k_a, r_k, gn_w, gn_b, need_ctx):
    r_c, k_c, v_c, g_c, kk_c, dirs_c = rwkv_features(p_ctx, mu, w0, w2, a0, a2, k_k, k_a)
    r_l, k_l, v_l, g_l, kk_l, dirs_l = rwkv_features(p_lat, mu, w0, w2, a0, a2, k_k, k_a)
    S0 = jnp.zeros((p_lat.shape[0], RW_HEADS, RW_HEAD_DIM, RW_HEAD_DIM), jnp.float32)
    o_lat, o_ctx = 0.0, 0.0
    for (dec_c, kd_c, b_c), (dec_l, kd_l, b_l), rev in zip(dirs_c, dirs_l, (False, True)):
        S_c, oc = rwkv_scan(S0, dec_c, kd_c, v_c, kk_c, b_c, r_c if need_ctx else None, rev)
        _, ol = rwkv_scan(S_c, dec_l, kd_l, v_l, kk_l, b_l, r_l, rev)
        o_lat = o_lat + ol
        if need_ctx:
            o_ctx = o_ctx + oc
    y_lat = rwkv_output(o_lat, r_l, k_l, v_l, g_l, g2, r_k, gn_w, gn_b)
    y_ctx = rwkv_output(o_ctx, r_c, k_c, v_c, g_c, g2, r_k, gn_w, gn_b) if need_ctx else None
    return y_lat, y_ctx


def mla_attend(qn, qr, kn, kr, v):
    scale = (MLA_NOPE_DIM + MLA_ROPE_DIM) ** -0.5
    s = (jnp.einsum('bqhd,bkhd->bhqk', qn, kn, preferred_element_type=jnp.float32)
         + jnp.einsum('bqhd,bkd->bhqk', qr, kr, preferred_element_type=jnp.float32)) * scale
    p = jax.nn.softmax(s, axis=-1).astype(v.dtype)
    o = jnp.einsum('bhqk,bkhd->bqhd', p, v)
    return o.reshape(o.shape[:2] + (-1,))


def mla_mixer(p_ctx, p_lat, q_norm, kv_norm, w_uq, w_ukv, cos, sin, need_ctx):
    def queries(p):
        q = (rms_norm(p[..., :MLA_Q_LORA], q_norm) @ w_uq).reshape(
            p.shape[:2] + (MLA_HEADS, MLA_NOPE_DIM + MLA_ROPE_DIM))
        return q[..., :MLA_NOPE_DIM], q[..., MLA_NOPE_DIM:]

    def keys_values(p):
        kv_c = p[..., MLA_Q_LORA:MLA_Q_LORA + MLA_KV_LORA]
        k_rope = p[..., MLA_Q_LORA + MLA_KV_LORA:]
        kv = (rms_norm(kv_c, kv_norm) @ w_ukv).reshape(p.shape[:2] + (MLA_HEADS, MLA_NOPE_DIM + MLA_V_DIM))
        return kv[..., :MLA_NOPE_DIM], k_rope, kv[..., MLA_NOPE_DIM:]

    B, T, _ = p_lat.shape
    qn_l, qr_l = queries(p_lat)
    qr_l = apply_rope(qr_l, cos[:, None], sin[:, None])
    kn_l, kr_l, v_l = keys_values(p_lat)
    kr_l = apply_rope(kr_l, cos, sin)
    kn_c, kr_c, v_c = keys_values(p_ctx)
    kn = jnp.concatenate([kn_l, kn_c], axis=1)
    kr = jnp.concatenate([kr_l, kr_c], axis=1)
    v = jnp.concatenate([v_l, v_c], axis=1)
    nb = T // Q_BLOCK
    blocks = lambda z: jnp.moveaxis(z.reshape((B, nb, Q_BLOCK) + z.shape[2:]), 1, 0)
    o = lax.map(lambda qs: mla_attend(qs[0], qs[1], kn, kr, v), (blocks(qn_l), blocks(qr_l)))
    y_lat = jnp.moveaxis(o, 0, 1).reshape(B, T, MLA_DIM)
    y_ctx = None
    if need_ctx:
        qn_c, qr_c = queries(p_ctx)
        y_ctx = mla_attend(qn_c, qr_c, kn_c, kr_c, v_c)
    return y_lat, y_ctx


def na_mixer(p_ctx, p_lat, rpb, need_ctx):
    B, T, _ = p_lat.shape
    rows = T // GRID_W
    kr = min(NA_WIN_R, rows)
    heads = lambda z: z.reshape(z.shape[:2] + (NA_HEADS, NA_HEAD_DIM))
    q_l, k_l, v_l = (heads(z) for z in split_cols(p_lat, [NA_DIM] * 3))
    k_c = heads(p_ctx[..., NA_DIM:2 * NA_DIM])
    v_c = heads(p_ctx[..., 2 * NA_DIM:])
    scale = NA_HEAD_DIM ** -0.5
    grid = lambda z: z.reshape(B, rows, GRID_W, NA_HEADS, NA_HEAD_DIM)
    qg, kg, vg = grid(q_l), grid(k_l), grid(v_l)
    col = jnp.arange(GRID_W)
    c_start = jnp.clip(col - NA_WIN_C // 2, 0, GRID_W - NA_WIN_C)
    in_win = (col[None, :] >= c_start[:, None]) & (col[None, :] < c_start[:, None] + NA_WIN_C)
    dc_idx = jnp.clip(col[None, :] - col[:, None] + NA_WIN_C - 1, 0, 2 * NA_WIN_C - 2)
    n_loc = kr * GRID_W

    def row_block(i):
        r_start = jnp.clip(i - kr // 2, 0, rows - kr)
        q_i = lax.dynamic_index_in_dim(qg, i, axis=1, keepdims=False)
        k_b = lax.dynamic_slice_in_dim(kg, r_start, kr, axis=1)
        v_b = lax.dynamic_slice_in_dim(vg, r_start, kr, axis=1)
        s_loc = jnp.einsum('bqhd,brkhd->bhqrk', q_i, k_b, preferred_element_type=jnp.float32) * scale
        dr_idx = r_start + jnp.arange(kr) - i + NA_WIN_R - 1
        bias = jnp.transpose(rpb[:, dr_idx][:, :, dc_idx], (0, 2, 1, 3))
        s_loc = jnp.where(in_win[:, None, :], s_loc + bias, NEG_INF)
        s_ctx = jnp.einsum('bqhd,bkhd->bhqk', q_i, k_c, preferred_element_type=jnp.float32) * scale
        s = jnp.concatenate([s_loc.reshape(B, NA_HEADS, GRID_W, n_loc), s_ctx], axis=-1)
        p = jax.nn.softmax(s, axis=-1).astype(v_c.dtype)
        p_loc = p[..., :n_loc].reshape(B, NA_HEADS, GRID_W, kr, GRID_W)
        return (jnp.einsum('bhqrk,brkhd->bqhd', p_loc, v_b)
                + jnp.einsum('bhqk,bkhd->bqhd', p[..., n_loc:], v_c))

    o = lax.map(row_block, jnp.arange(rows))
    y_lat = jnp.moveaxis(o, 0, 1).reshape(B, T, NA_DIM)
    y_ctx = attend(heads(p_ctx[..., :NA_DIM]), k_c, v_c, scale) if need_ctx else None
    return y_lat, y_ctx


def clamped_swiglu(gu):
    x_glu = jnp.minimum(gu[..., ::2], SWIGLU_LIMIT)
    x_lin = jnp.clip(gu[..., 1::2], -SWIGLU_LIMIT, SWIGLU_LIMIT)
    return x_glu * jax.nn.sigmoid(SWIGLU_ALPHA * x_glu) * (x_lin + 1)


def moe_ffn(h, router_w, router_b, w_gu, b_gu, w_dn, b_dn):
    N, D = h.shape
    logits = (h @ router_w).astype(jnp.float32) + router_b
    top_v, top_i = lax.top_k(logits, TOP_K)
    gates = jax.nn.softmax(top_v, axis=-1)
    n_assign = N * TOP_K
    flat_e = top_i.reshape(-1)
    flat_tok = (jnp.arange(n_assign) // TOP_K).astype(jnp.int32)
    order = jnp.argsort(flat_e)
    e_sorted = flat_e[order]
    counts = jnp.bincount(flat_e, length=N_EXPERTS)
    padded = (counts + MOE_BLOCK - 1) // MOE_BLOCK * MOE_BLOCK
    start = jnp.cumsum(counts) - counts
    ends_p = jnp.cumsum(padded)
    dest = (ends_p - padded)[e_sorted] + jnp.arange(n_assign) - start[e_sorted]
    n_blocks = -(-(n_assign + N_EXPERTS * (MOE_BLOCK - 1)) // MOE_BLOCK)
    n_pad = n_blocks * MOE_BLOCK
    tok_pad = jnp.full((n_pad,), N, jnp.int32).at[dest].set(flat_tok[order])
    gate_pad = jnp.zeros((n_pad,), gates.dtype).at[dest].set(gates.reshape(-1)[order])
    block_e = jnp.minimum(jnp.searchsorted(ends_p, jnp.arange(n_blocks) * MOE_BLOCK, side='right'),
                          N_EXPERTS - 1)
    h_ext = jnp.concatenate([h, jnp.zeros((1, D), h.dtype)], axis=0)

    def step(acc, blk):
        e, idx, g = blk
        gu = h_ext[idx] @ w_gu[e] + b_gu[e]
        y = clamped_swiglu(gu) @ w_dn[e] + b_dn[e]
        return acc.at[idx].add((y * g[:, None]).astype(acc.dtype)), None

    acc, _ = lax.scan(step, jnp.zeros((N + 1, D), h.dtype),
                      (block_e, tok_pad.reshape(n_blocks, MOE_BLOCK), gate_pad.reshape(n_blocks, MOE_BLOCK)))
    return acc[:N]


def setup_inputs(seed: int = 0) -> dict:
    key = jax.random.key(seed)
    ks = jax.random.split(key, 34)
    L, D = DEPTH, D_MODEL
    nrm = lambda i, shape, s: s * jax.random.normal(ks[i], shape, jnp.float32)
    return {
        'x': nrm(0, (BATCH, SEQ, D), 1.0),
        'c': nrm(1, (BATCH, D), 1.0),
        'ctx': nrm(2, (BATCH, CTX_LEN, D), 1.0),
        'c_ctx': nrm(3, (D,), 1.0),
        'ada_w': nrm(4, (L, D, 6 * D), 0.5 * D ** -0.5),
        'ada_b': nrm(5, (L, 6 * D), 0.02),
        'w_in': nrm(6, (L, D, IN_COLS), D ** -0.5),
        'rw_mu': 0.25 + nrm(7, (L, 2, RW_COLS), 0.1),
        'rw_w0': jax.random.uniform(ks[8], (L, 2, RW_DIM), jnp.float32, -5.0, -1.0),
        'rw_w2': nrm(9, (L, 2, RW_W_LORA, RW_DIM), 0.5 * RW_W_LORA ** -0.5),
        'rw_a0': nrm(10, (L, 2, RW_DIM), 0.1),
        'rw_a2': nrm(11, (L, 2, RW_A_LORA, RW_DIM), 0.5 * RW_A_LORA ** -0.5),
        'rw_g2': nrm(12, (L, RW_G_LORA, RW_DIM), RW_G_LORA ** -0.5),
        'rw_kk': 0.85 + nrm(13, (L, RW_DIM), 0.02),
        'rw_ka': 1.0 + nrm(14, (L, RW_DIM), 0.02),
        'rw_rk': nrm(15, (L, RW_HEADS, RW_HEAD_DIM), 0.1),
        'rw_gn_w': 1.0 + nrm(16, (L, RW_DIM), 0.02),
        'rw_gn_b': nrm(17, (L, RW_DIM), 0.02),
        'mla_q_norm': 1.0 + nrm(18, (L, MLA_Q_LORA), 0.02),
        'mla_kv_norm': 1.0 + nrm(19, (L, MLA_KV_LORA), 0.02),
        'mla_w_uq': nrm(20, (L, MLA_Q_LORA, MLA_HEADS * (MLA_NOPE_DIM + MLA_ROPE_DIM)), MLA_Q_LORA ** -0.5),
        'mla_w_ukv': nrm(21, (L, MLA_KV_LORA, MLA_HEADS * (MLA_NOPE_DIM + MLA_V_DIM)), MLA_KV_LORA ** -0.5),
        'na_rpb': nrm(22, (L, NA_HEADS, 2 * NA_WIN_R - 1, 2 * NA_WIN_C - 1), 0.1),
        'w_out': nrm(23, (L, MIX_DIM, D), DN_BETA * MIX_DIM ** -0.5),
        'ln1_g': 1.0 + nrm(24, (L, D), 0.02),
        'ln1_b': nrm(25, (L, D), 0.02),
        'router_w': nrm(26, (L, D, N_EXPERTS), D ** -0.5),
        'router_b': nrm(27, (L, N_EXPERTS), 0.01),
        'w_gu': nrm(28, (L, N_EXPERTS, D, 2 * D_FF), D ** -0.5),
        'b_gu': nrm(29, (L, N_EXPERTS, 2 * D_FF), 0.02),
        'w_dn': nrm(30, (L, N_EXPERTS, D_FF, D), DN_BETA * D_FF ** -0.5),
        'b_dn': nrm(31, (L, N_EXPERTS, D), 0.02),
        'ln2_g': 1.0 + nrm(32, (L, D), 0.02),
        'ln2_b': nrm(33, (L, D), 0.02),
    }


def reference(x, c, ctx, c_ctx, ada_w, ada_b, w_in, rw_mu, rw_w0, rw_w2, rw_a0, rw_a2, rw_g2,
              rw_kk, rw_ka, rw_rk, rw_gn_w, rw_gn_b, mla_q_norm, mla_kv_norm, mla_w_uq, mla_w_ukv,
              na_rpb, w_out, ln1_g, ln1_b, router_w, router_b, w_gu, b_gu, w_dn, b_dn, ln2_g, ln2_b):
    B, T, D = x.shape
    n_ctx = ctx.shape[1]
    cos, sin = axial_rope(T, MLA_ROPE_DIM)
    cond_l = jax.nn.silu(c)
    cond_c = jax.nn.silu(c_ctx)
    xl, xc = x, ctx
    for l in range(DEPTH):
        need_ctx = l < DEPTH - 1
        mod_l = cond_l @ ada_w[l] + ada_b[l]
        mod_c = cond_c @ ada_w[l] + ada_b[l]
        sa_l, ca_l, ga_l, sf_l, cf_l, gf_l = jnp.split(mod_l[:, None, :], 6, axis=-1)
        sa_c, ca_c, ga_c, sf_c, cf_c, gf_c = jnp.split(mod_c, 6, axis=-1)

        p_l = modulate(layer_norm(xl), sa_l, ca_l) @ w_in[l]
        p_c = modulate(layer_norm(xc), sa_c, ca_c) @ w_in[l]
        rw_pl, mla_pl, na_pl = split_cols(p_l, [RW_COLS, MLA_COLS, NA_COLS])
        rw_pc, mla_pc, na_pc = split_cols(p_c, [RW_COLS, MLA_COLS, NA_COLS])
        rw_yl, rw_yc = rwkv_mixer(rw_pc, rw_pl, rw_mu[l], rw_w0[l], rw_w2[l], rw_a0[l], rw_a2[l], rw_g2[l],
                                  rw_kk[l], rw_ka[l], rw_rk[l], rw_gn_w[l], rw_gn_b[l], need_ctx)
        mla_yl, mla_yc = mla_mixer(mla_pc, mla_pl, mla_q_norm[l], mla_kv_norm[l], mla_w_uq[l], mla_w_ukv[l],
                                   cos, sin, need_ctx)
        na_yl, na_yc = na_mixer(na_pc, na_pl, na_rpb[l], need_ctx)
        y_l = jnp.concatenate([rw_yl, mla_yl, na_yl], axis=-1) @ w_out[l]
        xl = layer_norm(DN_ALPHA * xl + ga_l * y_l, ln1_g[l], ln1_b[l])
        if need_ctx:
            y_c = jnp.concatenate([rw_yc, mla_yc, na_yc], axis=-1) @ w_out[l]
            xc = layer_norm(DN_ALPHA * xc + ga_c * y_c, ln1_g[l], ln1_b[l])

        h_l = modulate(layer_norm(xl), sf_l, cf_l).reshape(B * T, D)
        if need_ctx:
            h_c = modulate(layer_norm(xc), sf_c, cf_c).reshape(B * n_ctx, D)
            f = moe_ffn(jnp.concatenate([h_l, h_c], axis=0), router_w[l], router_b[l],
                        w_gu[l], b_gu[l], w_dn[l], b_dn[l])
            f_l = f[:B * T].reshape(B, T, D)
            xc = layer_norm(DN_ALPHA * xc + gf_c * f[B * T:].reshape(B, n_ctx, D), ln2_g[l], ln2_b[l])
        else:
            f_l = moe_ffn(h_l, router_w[l], router_b[l], w_gu[l], b_gu[l], w_dn[l], b_dn[l]).reshape(B, T, D)
        xl = layer_norm(DN_ALPHA * xl + gf_l * f_l, ln2_g[l], ln2_b[l])
    return xl
```

```python
import functools

import numpy as np
import jax
import jax.numpy as jnp
from jax import lax
from jax.experimental import pallas as pl
from jax.experimental.pallas import tpu as pltpu

F32 = jnp.float32
BF16 = jnp.bfloat16

GRID_W = 64
RW_HEAD_DIM = 64
RW_HEADS = 4
RW_DIM = RW_HEADS * RW_HEAD_DIM
RW_LORA = 64
RW_G_LORA = 128
RW_GN_EPS = 64e-5
RW_COLS = 3 * RW_DIM + 4 * RW_LORA + RW_G_LORA
MLA_HEADS = 4
MLA_NOPE = 128
MLA_ROPE = 64
MLA_V = 128
MLA_Q_LORA = 256
MLA_KV_LORA = 128
MLA_COLS = MLA_Q_LORA + MLA_KV_LORA + MLA_ROPE
MLA_COLS_EXT = MLA_Q_LORA + MLA_KV_LORA + 256
MLA_DIM = MLA_HEADS * MLA_V
NA_HEADS = 4
NA_HEAD_DIM = 64
NA_DIM = NA_HEADS * NA_HEAD_DIM
NA_WIN_R = 8
NA_WIN_C = 16
NA_COLS = 3 * NA_DIM
ROPE_BASE = 10000.0
N_EXPERTS = 32
TOP_K = 4
SWIGLU_ALPHA = 1.702
SWIGLU_LIMIT = 7.0
NEG_INF = -1e30

TM = 256
SCAN_TB = 16
MOE_BLK = 512
VMEM_LIMIT = 56 * 1024 * 1024


def _cparams(sem):
    return pltpu.CompilerParams(dimension_semantics=sem, vmem_limit_bytes=VMEM_LIMIT)


def _ln(x, eps=1e-5):
    mu = jnp.mean(x, axis=-1, keepdims=True)
    d = x - mu
    var = jnp.mean(d * d, axis=-1, keepdims=True)
    return d * lax.rsqrt(var + eps)


def _dot(a, b):
    return jnp.dot(a, b, preferred_element_type=F32)


def _split(a):
    hi = a.astype(BF16)
    lo = (a - hi.astype(F32)).astype(BF16)
    return hi, lo


def _dot_hl(a, b_bf16):
    hi, lo = _split(a)
    return _dot(hi, b_bf16) + _dot(lo, b_bf16)


def _dot3(a, b_hi, b_lo):
    hi, lo = _split(a)
    return _dot(hi, b_hi) + _dot(lo, b_hi) + _dot(hi, b_lo)


def _sigmoid(x):
    return 1.0 / (1.0 + jnp.exp(-x))


def _ada_kernel(cond_ref, w_ref, b_ref, o_ref):
    c = cond_ref[...]
    s = c * _sigmoid(c)
    w = w_ref[0]
    w_hi, w_lo = _split(w)
    o_ref[0] = _dot3(s, w_hi, w_lo) + b_ref[0]


def _ada_mod(cond, ada_w, ada_b):
    L, D, N = ada_w.shape
    R = cond.shape[0]
    tn = 512
    return pl.pallas_call(
        _ada_kernel,
        grid=(L, N // tn),
        in_specs=[
            pl.BlockSpec((R, D), lambda l, j: (0, 0)),
            pl.BlockSpec((1, D, tn), lambda l, j: (l, 0, j)),
            pl.BlockSpec((1, 1, tn), lambda l, j: (l, 0, j)),
        ],
        out_specs=pl.BlockSpec((1, R, tn), lambda l, j: (l, 0, j)),
        out_shape=jax.ShapeDtypeStruct((L, R, N), F32),
        compiler_params=_cparams(("arbitrary", "arbitrary")),
        name="ada_mod",
    )(cond, ada_w, ada_b.reshape(L, 1, N))


def _win_kernel(x_ref, mod_ref, w_ref, prw_ref, pmla_ref, pna_ref):
    x = x_ref[0]
    m = mod_ref[0, 0]
    xm = _ln(x) * (1.0 + m[1:2]) + m[0:1]
    p = _dot(xm.astype(BF16), w_ref[...])
    prw_ref[0] = p[:, :RW_COLS]
    pmla_ref[0] = p[:, RW_COLS:RW_COLS + MLA_COLS_EXT]
    pna_ref[0] = p[:, RW_COLS + MLA_COLS_EXT:].astype(BF16)


def _in_proj(xa, mod6, w_in_ext, nct):
    B, S, D = xa.shape
    NC = w_in_ext.shape[1]
    kind = lambda t: (t >= nct).astype(jnp.int32)
    return pl.pallas_call(
        _win_kernel,
        grid=(B, S // TM),
        in_specs=[
            pl.BlockSpec((1, TM, D), lambda b, t: (b, t, 0)),
            pl.BlockSpec((1, 1, 6, D), lambda b, t: (b, kind(t), 0, 0)),
            pl.BlockSpec((D, NC), lambda b, t: (0, 0)),
        ],
        out_specs=[
            pl.BlockSpec((1, TM, RW_COLS), lambda b, t: (b, t, 0)),
            pl.BlockSpec((1, TM, MLA_COLS_EXT), lambda b, t: (b, t, 0)),
            pl.BlockSpec((1, TM, NA_COLS), lambda b, t: (b, t, 0)),
        ],
        out_shape=[
            jax.ShapeDtypeStruct((B, S, RW_COLS), F32),
            jax.ShapeDtypeStruct((B, S, MLA_COLS_EXT), F32),
            jax.ShapeDtypeStruct((B, S, NA_COLS), BF16),
        ],
        compiler_params=_cparams(("arbitrary", "arbitrary")),
        name="in_proj",
    )(xa, mod6, w_in_ext)


def _group_sum(x, ones_ref):
    return _dot_hl(x, ones_ref[...])


def _rwfeat_kernel(p_ref, pp_ref, pn_ref, mu_ref, w0_ref, w2_ref, a0_ref, a2_ref, g2_ref,
                   kk_ref, rk_ref, ones_ref, f_ref, g_ref, bonus_ref, *, nct, nt):
    t = pl.program_id(1)
    p = p_ref[0]
    first = jnp.logical_or(t == 0, t == nct)
    last = jnp.logical_or(t == nct - 1, t == nt - 1)
    prev_row = jnp.where(first, 0.0, pp_ref[0, 7:8, :])
    next_row = jnp.where(last, 0.0, pn_ref[0, 0:1, :])
    rows = lax.broadcasted_iota(jnp.int32, p.shape, 0)
    prev = jnp.where(rows == 0, prev_row, pltpu.roll(p, 1, axis=0))
    nxt = jnp.where(rows == TM - 1, next_row, pltpu.roll(p, TM - 1, axis=0))
    mu = mu_ref[...]
    xs = p + mu[0:1] * (prev - p) + mu[1:2] * (nxt - p)

    D3 = 3 * RW_DIM
    r = xs[:, 0:RW_DIM]
    k = xs[:, RW_DIM:2 * RW_DIM]
    v = xs[:, 2 * RW_DIM:D3]
    w_lo = xs[:, D3:D3 + 2 * RW_LORA]
    a_lo = xs[:, D3 + 2 * RW_LORA:D3 + 4 * RW_LORA]
    g_pre = xs[:, D3 + 4 * RW_LORA:]

    lw = _dot(jnp.tanh(w_lo).astype(BF16), w2_ref[...]) + w0_ref[...]
    logw = jnp.minimum(lw, 0.0) - jnp.log(1.0 + jnp.exp(-jnp.abs(lw))) - 0.5
    decay = jnp.exp(-jnp.exp(logw))
    a = _sigmoid(_dot(a_lo.astype(BF16), a2_ref[...]) + a0_ref[...])

    kkr = k * kk_ref[...]
    kk = kkr * lax.rsqrt(_group_sum(kkr * kkr, ones_ref) + 1e-12)
    g_ref[0] = _dot(_sigmoid(g_pre).astype(BF16), g2_ref[...])
    bonus_ref[0] = _group_sum(r * k * rk_ref[...], ones_ref) * v

    f_ref[0, :, 0 * RW_DIM:1 * RW_DIM] = r
    f_ref[0, :, 1 * RW_DIM:2 * RW_DIM] = k
    f_ref[0, :, 2 * RW_DIM:3 * RW_DIM] = v
    f_ref[0, :, 3 * RW_DIM:4 * RW_DIM] = kk
    f_ref[0, :, 4 * RW_DIM:6 * RW_DIM] = decay
    f_ref[0, :, 6 * RW_DIM:8 * RW_DIM] = a


def _rw_features(p_rw, prm, nct):
    B, S, C = p_rw.shape
    nt = S // TM
    hb = TM // 8
    last_hb = S // 8 - 1
    full = lambda shape: pl.BlockSpec(shape, lambda b, t: (0,) * len(shape))
    return pl.pallas_call(
        functools.partial(_rwfeat_kernel, nct=nct, nt=nt),
        grid=(B, nt),
        in_specs=[
            pl.BlockSpec((1, TM, C), lambda b, t: (b, t, 0)),
            pl.BlockSpec((1, 8, C), lambda b, t: (b, jnp.maximum(t * hb - 1, 0), 0)),
            pl.BlockSpec((1, 8, C), lambda b, t: (b, jnp.minimum((t + 1) * hb, last_hb), 0)),
            full((2, C)),
            full((1, 2 * RW_DIM)), full((2 * RW_LORA, 2 * RW_DIM)),
            full((1, 2 * RW_DIM)), full((2 * RW_LORA, 2 * RW_DIM)),
            full((RW_G_LORA, RW_DIM)),
            full((1, RW_DIM)), full((1, RW_DIM)),
            full((RW_DIM, RW_DIM)),
        ],
        out_specs=[
            pl.BlockSpec((1, TM, 8 * RW_DIM), lambda b, t: (b, t, 0)),
            pl.BlockSpec((1, TM, RW_DIM), lambda b, t: (b, t, 0)),
            pl.BlockSpec((1, TM, RW_DIM), lambda b, t: (b, t, 0)),
        ],
        out_shape=[
            jax.ShapeDtypeStruct((B, S, 8 * RW_DIM), F32),
            jax.ShapeDtypeStruct((B, S, RW_DIM), F32),
            jax.ShapeDtypeStruct((B, S, RW_DIM), F32),
        ],
        compiler_params=_cparams(("arbitrary", "arbitrary")),
        name="rw_features",
    )(p_rw, p_rw, p_rw, prm["mu"], prm["w0"], prm["w2"], prm["a0"], prm["a2"], prm["g2"],
      prm["kk"], prm["rk"], prm["ones"])


def _scan_kernel(fs_ref, fd_ref, ka_ref, o_ref, s_ref, tmp_ref, *, tb):
    d = pl.program_id(0)
    g = pl.program_id(1)
    N = RW_HEAD_DIM

    @pl.when(g == 0)
    def _():
        s_ref[...] = jnp.zeros_like(s_ref)

    ka = ka_ref[...]

    def step(i, carry):
        tt = jnp.where(d == 0, i, tb - 1 - i)
        r = fs_ref[tt, 0]
        k = fs_ref[tt, 1]
        v = fs_ref[tt, 2]
        kk = fs_ref[tt, 3]
        w = fd_ref[0, tt, 0]
        a = fd_ref[0, tt, 1]
        b = a * kk
        kd = k * (1.0 + (a - 1.0) * ka)
        wr = w * r
        br = jnp.sum(b * r, axis=0, keepdims=True)
        kr = jnp.sum(kd * r, axis=0, keepdims=True)
        tmp_ref[0] = wr
        tmp_ref[1] = b
        tmp_ref[2] = kd
        sa = [jnp.zeros_like(v), jnp.zeros_like(v)]
        op = [jnp.zeros_like(v), jnp.zeros_like(v)]
        for j in range(N):
            sk = s_ref[j]
            sa[j % 2] = sa[j % 2] + sk * fs_ref[tt, 3, pl.ds(j, 1), :]
            op[j % 2] = op[j % 2] + sk * tmp_ref[0, pl.ds(j, 1), :]
        sa = sa[0] + sa[1]
        op = op[0] + op[1]
        for j in range(N):
            s_ref[j] = (s_ref[j] * fd_ref[0, tt, 0, pl.ds(j, 1), :]
                        - sa * tmp_ref[1, pl.ds(j, 1), :]
                        + v * tmp_ref[2, pl.ds(j, 1), :])
        o_ref[0, tt] = op - sa * br + v * kr
        return carry

    lax.fori_loop(0, tb, step, 0)


def _rw_scan(fs, fd, ka_t, n_ctx):
    S, _, N, BH = fs.shape
    tb = SCAN_TB
    nb = S // tb
    ncb = n_ctx // tb

    def tblk(d, g):
        bwd = jnp.where(g < ncb, ncb - 1 - g, nb - 1 - g + ncb)
        return jnp.where(d == 0, g, bwd)

    return pl.pallas_call(
        functools.partial(_scan_kernel, tb=tb),
        grid=(2, nb),
        in_specs=[
            pl.BlockSpec((tb, 4, N, BH), lambda d, g: (tblk(d, g), 0, 0, 0)),
            pl.BlockSpec((1, tb, 2, N, BH), lambda d, g: (d, tblk(d, g), 0, 0, 0)),
            pl.BlockSpec((N, BH), lambda d, g: (0, 0)),
        ],
        out_specs=pl.BlockSpec((1, tb, N, BH), lambda d, g: (d, tblk(d, g), 0, 0)),
        out_shape=jax.ShapeDtypeStruct((2, S, N, BH), F32),
        scratch_shapes=[pltpu.VMEM((N, N, BH), F32), pltpu.VMEM((3, N, BH), F32)],
        compiler_params=_cparams(("arbitrary", "arbitrary")),
        name="rw_scan",
    )(fs, fd, ka_t)


def _mlaprep_kernel(p_ref, cos_ref, sin_ref, qn_ref, kvn_ref, wuq_ref, wukv_ref, q_ref, k_ref, v_ref):
    p = p_ref[0]
    cos = cos_ref[...]
    sin = sin_ref[...]
    scale = (MLA_NOPE + MLA_ROPE) ** -0.5

    def rms(x, g):
        return x * lax.rsqrt(jnp.mean(x * x, axis=-1, keepdims=True) + 1e-6) * g

    q = _dot(rms(p[:, :MLA_Q_LORA], qn_ref[...]).astype(BF16), wuq_ref[...])
    kv = _dot(rms(p[:, MLA_Q_LORA:MLA_Q_LORA + MLA_KV_LORA], kvn_ref[...]).astype(BF16), wukv_ref[...])
    c0 = MLA_Q_LORA + MLA_KV_LORA
    kr = (p[:, c0:c0 + 128] * cos + p[:, c0 + 128:c0 + 256] * sin).astype(BF16)
    for h in range(MLA_HEADS):
        qb = h * 384
        q_ref[0, :, h * 256:h * 256 + 128] = (q[:, qb:qb + 128] * scale).astype(BF16)
        q_ref[0, :, h * 256 + 128:h * 256 + 256] = (
            (q[:, qb + 128:qb + 256] * cos + q[:, qb + 256:qb + 384] * sin) * scale).astype(BF16)
        k_ref[0, :, h * 256:h * 256 + 128] = kv[:, h * 256:h * 256 + 128].astype(BF16)
        k_ref[0, :, h * 256 + 128:h * 256 + 256] = kr
        v_ref[0, :, h * 128:(h + 1) * 128] = kv[:, h * 256 + 128:h * 256 + 256].astype(BF16)


def _mla_prep(p_mla, cos128, sin128, prm):
    B, S, C = p_mla.shape
    full = lambda shape: pl.BlockSpec(shape, lambda b, t: (0,) * len(shape))
    H = MLA_HEADS
    return pl.pallas_call(
        _mlaprep_kernel,
        grid=(B, S // TM),
        in_specs=[
            pl.BlockSpec((1, TM, C), lambda b, t: (b, t, 0)),
            pl.BlockSpec((TM, 128), lambda b, t: (t, 0)),
            pl.BlockSpec((TM, 128), lambda b, t: (t, 0)),
            full((1, MLA_Q_LORA)), full((1, MLA_KV_LORA)),
            full((MLA_Q_LORA, H * 384)), full((MLA_KV_LORA, H * 256)),
        ],
        out_specs=[
            pl.BlockSpec((1, TM, H * 256), lambda b, t: (b, t, 0)),
            pl.BlockSpec((1, TM, H * 256), lambda b, t: (b, t, 0)),
            pl.BlockSpec((1, TM, H * 128), lambda b, t: (b, t, 0)),
        ],
        out_shape=[
            jax.ShapeDtypeStruct((B, S, H * 256), BF16),
            jax.ShapeDtypeStruct((B, S, H * 256), BF16),
            jax.ShapeDtypeStruct((B, S, H * 128), BF16),
        ],
        compiler_params=_cparams(("arbitrary", "arbitrary")),
        name="mla_prep",
    )(p_mla, cos128, sin128, prm["q_norm"], prm["kv_norm"], prm["w_uq"], prm["w_ukv"])


def _mla_attn_kernel(q_ref, k_ref, v_ref, o_ref, *, t0, nct, n_ctx):
    t = pl.program_id(1) + t0

    def attend(n_keys):
        for h in range(MLA_HEADS):
            q = q_ref[0, :, h * 256:(h + 1) * 256]
            k = k_ref[0, 0:n_keys, h * 256:(h + 1) * 256]
            s = lax.dot_general(q, k, (((1,), (1,)), ((), ())), preferred_element_type=F32)
            m = jnp.max(s, axis=-1, keepdims=True)
            e = jnp.exp(s - m)
            l = jnp.sum(e, axis=-1, keepdims=True)
            o = _dot(e.astype(BF16), v_ref[0, 0:n_keys, h * 128:(h + 1) * 128])
            o_ref[0, :, h * 128:(h + 1) * 128] = (o / l).astype(BF16)

    S = k_ref.shape[1]
    if t0 < nct:
        @pl.when(t < nct)
        def _():
            attend(n_ctx)

        @pl.when(t >= nct)
        def _():
            attend(S)
    else:
        attend(S)


def _mla_attn(q, k, v, t0, nct, n_ctx):
    B, S, _ = q.shape
    nq = S // TM - t0
    H = MLA_HEADS
    return pl.pallas_call(
        functools.partial(_mla_attn_kernel, t0=t0, nct=nct, n_ctx=n_ctx),
        grid=(B, nq),
        in_specs=[
            pl.BlockSpec((1, TM, H * 256), lambda b, t: (b, t + t0, 0)),
            pl.BlockSpec((1, S, H * 256), lambda b, t: (b, 0, 0)),
            pl.BlockSpec((1, S, H * 128), lambda b, t: (b, 0, 0)),
        ],
        out_specs=pl.BlockSpec((1, TM, H * 128), lambda b, t: (b, t, 0)),
        out_shape=jax.ShapeDtypeStruct((B, nq * TM, H * 128), BF16),
        compiler_params=_cparams(("arbitrary", "arbitrary")),
        name="mla_attn",
    )(q, k, v)


def _na_kernel(p_ref, bias_ref, o_ref, *, n_ctx, rows, with_ctx):
    s_id = pl.program_id(1)
    W = GRID_W
    n_loc = NA_WIN_R * W
    scale = NA_HEAD_DIM ** -0.5
    nq_ctx = n_ctx // W

    def heads_out(q, parts):
        outs = []
        for h in range(NA_HEADS):
            hs = slice(h * NA_HEAD_DIM, (h + 1) * NA_HEAD_DIM)
            qh = q[:, hs] * scale
            ss = []
            for kx, vx, bias in parts:
                s = lax.dot_general(qh, kx[:, hs], (((1,), (1,)), ((), ())), preferred_element_type=F32)
                if bias is not None:
                    s = s + bias[h]
                ss.append(s)
            m = ss[0].max(axis=-1, keepdims=True)
            for s in ss[1:]:
                m = jnp.maximum(m, s.max(axis=-1, keepdims=True))
            acc = 0.0
            l = 0.0
            for s, (kx, vx, bias) in zip(ss, parts):
                e = jnp.exp(s - m)
                l = l + jnp.sum(e, axis=-1, keepdims=True)
                acc = acc + _dot(e.astype(BF16), vx[:, hs])
            outs.append(acc / l)
        return jnp.concatenate(outs, axis=-1).astype(BF16)

    k_c = p_ref[0, 0:n_ctx, NA_DIM:2 * NA_DIM]
    v_c = p_ref[0, 0:n_ctx, 2 * NA_DIM:3 * NA_DIM]

    def lat_row(i):
        r_start = jnp.clip(i - NA_WIN_R // 2, 0, rows - NA_WIN_R)
        pat = i - r_start
        q0 = pl.multiple_of(n_ctx + i * W, W)
        k0 = pl.multiple_of(n_ctx + r_start * W, W)
        q = p_ref[0, pl.ds(q0, W), 0:NA_DIM]
        k_l = p_ref[0, pl.ds(k0, n_loc), NA_DIM:2 * NA_DIM]
        v_l = p_ref[0, pl.ds(k0, n_loc), 2 * NA_DIM:3 * NA_DIM]
        o_ref[0] = heads_out(q, [(k_l, v_l, bias_ref[pat]), (k_c, v_c, None)])

    if with_ctx:
        @pl.when(s_id < nq_ctx)
        def _():
            q0 = pl.multiple_of(s_id * W, W)
            q = p_ref[0, pl.ds(q0, W), 0:NA_DIM]
            o_ref[0] = heads_out(q, [(k_c, v_c, None)])

        @pl.when(s_id >= nq_ctx)
        def _():
            lat_row(s_id - nq_ctx)
    else:
        lat_row(s_id)


def _na_attn(p_na, bias_tab, n_ctx, with_ctx):
    B, S, C = p_na.shape
    T = S - n_ctx
    rows = T // GRID_W
    nsteps = rows + (n_ctx // GRID_W if with_ctx else 0)
    return pl.pallas_call(
        functools.partial(_na_kernel, n_ctx=n_ctx, rows=rows, with_ctx=with_ctx),
        grid=(B, nsteps),
        in_specs=[
            pl.BlockSpec((1, S, C), lambda b, s: (b, 0, 0)),
            pl.BlockSpec(bias_tab.shape, lambda b, s: (0, 0, 0, 0)),
        ],
        out_specs=pl.BlockSpec((1, GRID_W, NA_DIM), lambda b, s: (b, s, 0)),
        out_shape=jax.ShapeDtypeStruct((B, nsteps * GRID_W, NA_DIM), BF16),
        compiler_params=_cparams(("arbitrary", "arbitrary")),
        name="na_attn",
    )(p_na, bias_tab)


def _na_bias_table(rpb):
    col = jnp.arange(GRID_W)
    c_start = jnp.clip(col - NA_WIN_C // 2, 0, GRID_W - NA_WIN_C)
    in_win = (col[None, :] >= c_start[:, None]) & (col[None, :] < c_start[:, None] + NA_WIN_C)
    dc_idx = jnp.clip(col[None, :] - col[:, None] + NA_WIN_C - 1, 0, 2 * NA_WIN_C - 2)
    pats = []
    for pat in range(NA_WIN_R):
        dr_idx = jnp.arange(NA_WIN_R) - pat + NA_WIN_R - 1
        bias = jnp.transpose(rpb[:, dr_idx][:, :, dc_idx], (0, 2, 1, 3))
        bias = jnp.where(in_win[:, None, :], bias, NEG_INF)
        pats.append(bias.reshape(NA_HEADS, GRID_W, NA_WIN_R * GRID_W))
    return jnp.stack(pats).astype(F32)


def _outproj_kernel(orw_ref, bonus_ref, g_ref, mla_ref, na_ref, x_ref, mod_ref, gnw_ref, gnb_ref, ones_ref,
                    wout_ref, ln1g_ref, ln1b_ref, rwh_ref, rwl_ref, rb_ref,
                    x1_ref, h_ref, lg_ref, *, dn_alpha):
    o2 = orw_ref[0]
    o = o2[:, :RW_DIM] + o2[:, RW_DIM:]
    inv_n = 1.0 / RW_HEAD_DIM
    mu = _group_sum(o, ones_ref) * inv_n
    dlt = o - mu
    var = _group_sum(dlt * dlt, ones_ref) * inv_n
    on = dlt * lax.rsqrt(var + RW_GN_EPS) * gnw_ref[...] + gnb_ref[...]
    rw_y = ((on + bonus_ref[0]) * g_ref[0]).astype(BF16)
    y = (_dot(rw_y, wout_ref[0:RW_DIM, :])
         + _dot(mla_ref[0], wout_ref[RW_DIM:RW_DIM + MLA_DIM, :])
         + _dot(na_ref[0], wout_ref[RW_DIM + MLA_DIM:, :]))
    m = mod_ref[0, 0]
    x1 = _ln(dn_alpha * x_ref[0] + m[2:3] * y) * ln1g_ref[...] + ln1b_ref[...]
    x1_ref[0] = x1
    h = _ln(x1) * (1.0 + m[4:5]) + m[3:4]
    h_ref[0] = h.astype(BF16)
    lg_ref[0] = _dot3(h, rwh_ref[...], rwl_ref[...]) + rb_ref[...]


def _out_proj(o_rw, bonus, g, mla_o, na_o, xa, mod6, prm, t0, nct, dn_alpha):
    B, S, D = xa.shape
    nt = S // TM - t0
    So = nt * TM
    kind = lambda t: (t + t0 >= nct).astype(jnp.int32)
    full = lambda shape: pl.BlockSpec(shape, lambda b, t: (0,) * len(shape))
    off = lambda C: pl.BlockSpec((1, TM, C), lambda b, t: (b, t + t0, 0))
    own = lambda C: pl.BlockSpec((1, TM, C), lambda b, t: (b, t, 0))
    return pl.pallas_call(
        functools.partial(_outproj_kernel, dn_alpha=dn_alpha),
        grid=(B, nt),
        in_specs=[
            off(2 * RW_DIM), off(RW_DIM), off(RW_DIM), own(MLA_DIM), own(NA_DIM), off(D),
            pl.BlockSpec((1, 1, 6, D), lambda b, t: (b, kind(t), 0, 0)),
            full((1, RW_DIM)), full((1, RW_DIM)), full((RW_DIM, RW_DIM)),
            full((D, D)), full((1, D)), full((1, D)),
            full((D, 128)), full((D, 128)), full((1, 128)),
        ],
        out_specs=[own(D), own(D), own(128)],
        out_shape=[
            jax.ShapeDtypeStruct((B, So, D), F32),
            jax.ShapeDtypeStruct((B, So, D), BF16),
            jax.ShapeDtypeStruct((B, So, 128), F32),
        ],
        compiler_params=_cparams(("arbitrary", "arbitrary")),
        name="out_proj",
    )(o_rw, bonus, g, mla_o, na_o, xa, mod6, prm["gn_w"], prm["gn_b"], prm["ones"],
      prm["w_out"], prm["ln1_g"], prm["ln1_b"], prm["router_hi"], prm["router_lo"], prm["router_b"])


def _moe_kernel(be_ref, nu_ref, x_ref, wgu_ref, bgu_ref, wdn_ref, bdn_ref, y_ref):
    i = pl.program_id(0)
    F = wdn_ref.shape[1]

    @pl.when(i < nu_ref[0])
    def _():
        gu = _dot(x_ref[...], wgu_ref[0]) + bgu_ref[0]
        glu = jnp.minimum(gu[:, :F], SWIGLU_LIMIT)
        lin = jnp.clip(gu[:, F:], -SWIGLU_LIMIT, SWIGLU_LIMIT)
        act = glu * _sigmoid(SWIGLU_ALPHA * glu) * (lin + 1.0)
        y = _dot(act.astype(BF16), wdn_ref[0]) + bdn_ref[0]
        y_ref[...] = y.astype(BF16)

    @pl.when(i >= nu_ref[0])
    def _():
        y_ref[...] = jnp.zeros_like(y_ref)


def _moe_ffn(x_sorted, block_e, n_used, w_gu, b_gu, w_dn, b_dn):
    n_pad, D = x_sorted.shape
    E, _, F2 = w_gu.shape
    F = F2 // 2
    n_blocks = n_pad // MOE_BLK
    grid_spec = pltpu.PrefetchScalarGridSpec(
        num_scalar_prefetch=2,
        grid=(n_blocks,),
        in_specs=[
            pl.BlockSpec((MOE_BLK, D), lambda i, be, nu: (i, 0)),
            pl.BlockSpec((1, D, F2), lambda i, be, nu: (be[i], 0, 0)),
            pl.BlockSpec((1, 1, F2), lambda i, be, nu: (be[i], 0, 0)),
            pl.BlockSpec((1, F, D), lambda i, be, nu: (be[i], 0, 0)),
            pl.BlockSpec((1, 1, D), lambda i, be, nu: (be[i], 0, 0)),
        ],
        out_specs=pl.BlockSpec((MOE_BLK, D), lambda i, be, nu: (i, 0)),
    )
    return pl.pallas_call(
        _moe_kernel,
        grid_spec=grid_spec,
        out_shape=jax.ShapeDtypeStruct((n_pad, D), BF16),
        compiler_params=_cparams(("arbitrary",)),
        name="moe_ffn",
    )(block_e, n_used, x_sorted, w_gu, b_gu.reshape(E, 1, F2), w_dn, b_dn.reshape(E, 1, D))


def _route(logits, n_tok):
    top_v, top_i = lax.top_k(logits, TOP_K)
    gates = jax.nn.softmax(top_v, axis=-1)
    n_assign = n_tok * TOP_K
    flat_e = top_i.reshape(-1)
    onehot = (flat_e[:, None] == jnp.arange(N_EXPERTS)[None, :]).astype(jnp.int32)
    rank = jnp.take_along_axis(jnp.cumsum(onehot, axis=0), flat_e[:, None], axis=1)[:, 0] - 1
    counts = jnp.sum(onehot, axis=0)
    padded = (counts + MOE_BLK - 1) // MOE_BLK * MOE_BLK
    ends_p = jnp.cumsum(padded)
    dest = (ends_p - padded)[flat_e] + rank
    n_blocks = -(-(n_assign + N_EXPERTS * (MOE_BLK - 1)) // MOE_BLK)
    n_pad = n_blocks * MOE_BLK
    flat_tok = (jnp.arange(n_assign) // TOP_K).astype(jnp.int32)
    tok_pad = jnp.zeros((n_pad,), jnp.int32).at[dest].set(flat_tok)
    block_e = jnp.minimum(jnp.searchsorted(ends_p, jnp.arange(n_blocks) * MOE_BLK, side='right'),
                          N_EXPERTS - 1).astype(jnp.int32)
    n_used = (ends_p[-1] // MOE_BLK).astype(jnp.int32).reshape(1)
    return gates, dest.reshape(n_tok, TOP_K).astype(jnp.int32), tok_pad, block_e, n_used


def _final_kernel(x1_ref, f_ref, mod_ref, g_ref, b_ref, o_ref, *, dn_alpha):
    m = mod_ref[0, 0]
    o_ref[0] = _ln(dn_alpha * x1_ref[0] + m[5:6] * f_ref[0]) * g_ref[...] + b_ref[...]


def _final(x1, f, mod6, ln_g, ln_b, t0, nct, dn_alpha):
    B, So, D = x1.shape
    kind = lambda t: (t + t0 >= nct).astype(jnp.int32)
    blk = pl.BlockSpec((1, TM, D), lambda b, t: (b, t, 0))
    vec = pl.BlockSpec((1, D), lambda b, t: (0, 0))
    return pl.pallas_call(
        functools.partial(_final_kernel, dn_alpha=dn_alpha),
        grid=(B, So // TM),
        in_specs=[blk, blk, pl.BlockSpec((1, 1, 6, D), lambda b, t: (b, kind(t), 0, 0)), vec, vec],
        out_specs=blk,
        out_shape=jax.ShapeDtypeStruct((B, So, D), F32),
        compiler_params=_cparams(("arbitrary", "arbitrary")),
        name="ffn_residual",
    )(x1, f, mod6, ln_g, ln_b)


def _rope_tables(n_ctx, T):
    t = jnp.arange(T)
    row = (t // GRID_W).astype(F32)
    col = (t % GRID_W).astype(F32)
    n_freq = MLA_ROPE // 4
    inv = ROPE_BASE ** (-jnp.arange(n_freq, dtype=F32) / n_freq)
    ang = jnp.concatenate([row[:, None] * inv, col[:, None] * inv], axis=-1)
    cos, sin = jnp.cos(ang), jnp.sin(ang)
    z = jnp.zeros((T, 128 - MLA_ROPE), F32)
    cos128 = jnp.concatenate([cos, cos, z], axis=-1)
    sin128 = jnp.concatenate([-sin, sin, z], axis=-1)
    cos_c = jnp.concatenate([jnp.ones((n_ctx, MLA_ROPE), F32), jnp.zeros((n_ctx, 128 - MLA_ROPE), F32)], -1)
    sin_c = jnp.zeros((n_ctx, 128), F32)
    return jnp.concatenate([cos_c, cos128], 0), jnp.concatenate([sin_c, sin128], 0)


def _rope_slabs(w):
    ev, od = w[:, 0::2], w[:, 1::2]
    z = jnp.zeros((w.shape[0], 128 - MLA_ROPE), w.dtype)
    return jnp.concatenate([ev, od, z, od, ev, z], axis=-1)


def _blockdiag2(m):
    z = jnp.zeros_like(m[0])
    return jnp.concatenate([jnp.concatenate([m[0], z], 1), jnp.concatenate([z, m[1]], 1)], 0)


def kernel(x, c, ctx, c_ctx, ada_w, ada_b, w_in, rw_mu, rw_w0, rw_w2, rw_a0, rw_a2, rw_g2, rw_kk, rw_ka, rw_rk, rw_gn_w, rw_gn_b, mla_q_norm, mla_kv_norm, mla_w_uq, mla_w_ukv, na_rpb, w_out, ln1_g, ln1_b, router_w, router_b, w_gu, b_gu, w_dn, b_dn, ln2_g, ln2_b):
    B, T, D = x.shape
    n_ctx = ctx.shape[1]
    depth = ada_w.shape[0]
    S = n_ctx + T
    assert n_ctx % TM == 0 and T % TM == 0 and T % GRID_W == 0 and T // GRID_W >= NA_WIN_R
    nct = n_ctx // TM
    BH = B * RW_HEADS
    dn_alpha = (2 * depth) ** 0.25
    F = w_dn.shape[2]

    R = (B + 1 + 7) // 8 * 8
    cond = jnp.zeros((R, D), F32).at[:B].set(c).at[B].set(c_ctx)
    mod = _ada_mod(cond, ada_w, ada_b)
    mod_l = mod[:, :B].reshape(depth, B, 1, 6, D)
    mod_c = jnp.broadcast_to(mod[:, B].reshape(depth, 1, 1, 6, D), (depth, B, 1, 6, D))
    mod6 = jnp.concatenate([mod_c, mod_l], axis=2)

    cos128, sin128 = _rope_tables(n_ctx, T)
    ones_blk = jnp.kron(jnp.eye(RW_HEADS, dtype=F32), jnp.ones((RW_HEAD_DIM, RW_HEAD_DIM), F32)).astype(BF16)

    xa = jnp.concatenate([ctx, x], axis=1)
    for l in range(depth):
        need_ctx = l < depth - 1
        t0 = 0 if need_ctx else nct

        wi = w_in[l]
        c_m = RW_COLS
        w_in_ext = jnp.concatenate(
            [wi[:, :c_m + MLA_Q_LORA + MLA_KV_LORA], _rope_slabs(wi[:, c_m + MLA_Q_LORA + MLA_KV_LORA:c_m + MLA_COLS]),
             wi[:, c_m + MLA_COLS:]], axis=-1).astype(BF16)
        wuq = mla_w_uq[l].reshape(MLA_Q_LORA, MLA_HEADS, MLA_NOPE + MLA_ROPE)
        wuq_ext = jnp.concatenate(
            [jnp.concatenate([wuq[:, h, :MLA_NOPE], _rope_slabs(wuq[:, h, MLA_NOPE:])], -1) for h in range(MLA_HEADS)],
            axis=-1).astype(BF16)
        rw_prm = dict(
            mu=rw_mu[l],
            w0=rw_w0[l].reshape(1, 2 * RW_DIM), w2=_blockdiag2(rw_w2[l]).astype(BF16),
            a0=rw_a0[l].reshape(1, 2 * RW_DIM), a2=_blockdiag2(rw_a2[l]).astype(BF16),
            g2=rw_g2[l].astype(BF16), kk=rw_kk[l].reshape(1, RW_DIM), rk=rw_rk[l].reshape(1, RW_DIM),
            ones=ones_blk)
        mla_prm = dict(q_norm=mla_q_norm[l].reshape(1, -1), kv_norm=mla_kv_norm[l].reshape(1, -1),
                       w_uq=wuq_ext, w_ukv=mla_w_ukv[l].astype(BF16))
        rt = jnp.zeros((D, 128), F32).at[:, :N_EXPERTS].set(router_w[l])
        rt_hi = rt.astype(BF16)
        out_prm = dict(gn_w=rw_gn_w[l].reshape(1, -1), gn_b=rw_gn_b[l].reshape(1, -1), ones=ones_blk,
                       w_out=w_out[l].astype(BF16), ln1_g=ln1_g[l].reshape(1, -1), ln1_b=ln1_b[l].reshape(1, -1),
                       router_hi=rt_hi, router_lo=(rt - rt_hi.astype(F32)).astype(BF16),
                       router_b=jnp.zeros((1, 128), F32).at[0, :N_EXPERTS].set(router_b[l]))
        wgu = jnp.concatenate([w_gu[l][:, :, 0::2], w_gu[l][:, :, 1::2]], axis=-1).astype(BF16)
        bgu = jnp.concatenate([b_gu[l][:, 0::2], b_gu[l][:, 1::2]], axis=-1)
        wdn = w_dn[l].astype(BF16)

        p_rw, p_mla, p_na = _in_proj(xa, mod6[l], w_in_ext, nct)
        feat, g_gate, bonus = _rw_features(p_rw, rw_prm, nct)
        ft = jnp.transpose(feat.reshape(B, S, 8, RW_HEADS, RW_HEAD_DIM), (1, 2, 4, 0, 3)).reshape(
            S, 8, RW_HEAD_DIM, BH)
        fs = ft[:, 0:4]
        fd = jnp.stack([ft[:, 4::2], ft[:, 5::2]], axis=0)
        ka_t = jnp.tile(rw_ka[l].reshape(RW_HEADS, RW_HEAD_DIM).T[:, None, :], (1, B, 1)).reshape(RW_HEAD_DIM, BH)
        o_scan = _rw_scan(fs, fd, ka_t, n_ctx)
        o_rw = jnp.transpose(o_scan.reshape(2, S, RW_HEAD_DIM, B, RW_HEADS), (3, 1, 0, 4, 2)).reshape(
            B, S, 2 * RW_DIM)

        q, k, v = _mla_prep(p_mla, cos128, sin128, mla_prm)
        mla_o = _mla_attn(q, k, v, t0, nct, n_ctx)
        na_o = _na_attn(p_na, _na_bias_table(na_rpb[l]), n_ctx, need_ctx)

        x1, h, logits = _out_proj(o_rw, bonus, g_gate, mla_o, na_o, xa, mod6[l], out_prm, t0, nct, dn_alpha)

        So = x1.shape[1]
        n_tok = B * So
        gates, dest, tok_pad, block_e, n_used = _route(logits.reshape(n_tok, 128)[:, :N_EXPERTS], n_tok)
        x_sorted = jnp.take(h.reshape(n_tok, D), tok_pad, axis=0)
        y_sorted = _moe_ffn(x_sorted, block_e, n_used, wgu, bgu, wdn, b_dn[l])
        yk = jnp.take(y_sorted, dest.reshape(-1), axis=0).reshape(n_tok, TOP_K, D).astype(F32)
        f = jnp.sum(yk * gates[:, :, None], axis=1).reshape(B, So, D)

        xa = _final(x1, f, mod6[l], ln2_g[l].reshape(1, -1), ln2_b[l].reshape(1, -1), t0, nct, dn_alpha)
    return xa
```

```python
import functools

import numpy as np
import jax
import jax.numpy as jnp
from jax import lax
from jax.experimental import pallas as pl
from jax.experimental.pallas import tpu as pltpu

F32 = jnp.float32
BF16 = jnp.bfloat16

GRID_W = 64
RW_HEAD_DIM = 64
RW_HEADS = 4
RW_DIM = RW_HEADS * RW_HEAD_DIM
RW_LORA = 64
RW_G_LORA = 128
RW_GN_EPS = 64e-5
RW_COLS = 3 * RW_DIM + 4 * RW_LORA + RW_G_LORA
MLA_HEADS = 4
MLA_NOPE = 128
MLA_ROPE = 64
MLA_V = 128
MLA_Q_LORA = 256
MLA_KV_LORA = 128
MLA_COLS = MLA_Q_LORA + MLA_KV_LORA + MLA_ROPE
MLA_COLS_EXT = MLA_Q_LORA + MLA_KV_LORA + 256
MLA_DIM = MLA_HEADS * MLA_V
NA_HEADS = 4
NA_HEAD_DIM = 64
NA_DIM = NA_HEADS * NA_HEAD_DIM
NA_WIN_R = 8
NA_WIN_C = 16
NA_COLS = 3 * NA_DIM
ROPE_BASE = 10000.0
N_EXPERTS = 32
TOP_K = 4
SWIGLU_ALPHA = 1.702
SWIGLU_LIMIT = 7.0
NEG_INF = -1e30

TM = 256
SCAN_TB = 16
MOE_BLK = 512
VMEM_LIMIT = 56 * 1024 * 1024


def _cparams(sem):
    return pltpu.CompilerParams(dimension_semantics=sem, vmem_limit_bytes=VMEM_LIMIT)


def _ln(x, eps=1e-5):
    mu = jnp.mean(x, axis=-1, keepdims=True)
    d = x - mu
    var = jnp.mean(d * d, axis=-1, keepdims=True)
    return d * lax.rsqrt(var + eps)


def _dot(a, b):
    return jnp.dot(a, b, preferred_element_type=F32)


def _split(a):
    hi = a.astype(BF16)
    lo = (a - hi.astype(F32)).astype(BF16)
    return hi, lo


def _dot_hl(a, b_bf16):
    hi, lo = _split(a)
    return _dot(hi, b_bf16) + _dot(lo, b_bf16)


def _dot3(a, b_hi, b_lo):
    hi, lo = _split(a)
    return _dot(hi, b_hi) + _dot(lo, b_hi) + _dot(hi, b_lo)


def _sigmoid(x):
    return 1.0 / (1.0 + jnp.exp(-x))


def _ada_kernel(cond_ref, w_ref, b_ref, o_ref):
    c = cond_ref[...]
    s = c * _sigmoid(c)
    w = w_ref[0]
    w_hi, w_lo = _split(w)
    o_ref[0] = _dot3(s, w_hi, w_lo) + b_ref[0]


def _ada_mod(cond, ada_w, ada_b):
    L, D, N = ada_w.shape
    R = cond.shape[0]
    tn = 512
    return pl.pallas_call(
        _ada_kernel,
        grid=(L, N // tn),
        in_specs=[
            pl.BlockSpec((R, D), lambda l, j: (0, 0)),
            pl.BlockSpec((1, D, tn), lambda l, j: (l, 0, j)),
            pl.BlockSpec((1, 1, tn), lambda l, j: (l, 0, j)),
        ],
        out_specs=pl.BlockSpec((1, R, tn), lambda l, j: (l, 0, j)),
        out_shape=jax.ShapeDtypeStruct((L, R, N), F32),
        compiler_params=_cparams(("arbitrary", "arbitrary")),
        name="ada_mod",
    )(cond, ada_w, ada_b.reshape(L, 1, N))


def _win_kernel(x_ref, mod_ref, w_ref, prw_ref, pmla_ref, pna_ref):
    x = x_ref[0]
    m = mod_ref[0, 0]
    xm = _ln(x) * (1.0 + m[1:2]) + m[0:1]
    p = _dot(xm.astype(BF16), w_ref[...])
    prw_ref[0] = p[:, :RW_COLS]
    pmla_ref[0] = p[:, RW_COLS:RW_COLS + MLA_COLS_EXT]
    pna_ref[0] = p[:, RW_COLS + MLA_COLS_EXT:].astype(BF16)


def _in_proj(xa, mod6, w_in_ext, nct):
    B, S, D = xa.shape
    NC = w_in_ext.shape[1]
    kind = lambda t: (t >= nct).astype(jnp.int32)
    return pl.pallas_call(
        _win_kernel,
        grid=(B, S // TM),
        in_specs=[
            pl.BlockSpec((1, TM, D), lambda b, t: (b, t, 0)),
            pl.BlockSpec((1, 1, 6, D), lambda b, t: (b, kind(t), 0, 0)),
            pl.BlockSpec((D, NC), lambda b, t: (0, 0)),
        ],
        out_specs=[
            pl.BlockSpec((1, TM, RW_COLS), lambda b, t: (b, t, 0)),
            pl.BlockSpec((1, TM, MLA_COLS_EXT), lambda b, t: (b, t, 0)),
            pl.BlockSpec((1, TM, NA_COLS), lambda b, t: (b, t, 0)),
        ],
        out_shape=[
            jax.ShapeDtypeStruct((B, S, RW_COLS), F32),
            jax.ShapeDtypeStruct((B, S, MLA_COLS_EXT), F32),
            jax.ShapeDtypeStruct((B, S, NA_COLS), BF16),
        ],
        compiler_params=_cparams(("arbitrary", "arbitrary")),
        name="in_proj",
    )(xa, mod6, w_in_ext)


def _group_sum(x, ones_ref):
    return _dot_hl(x, ones_ref[...])


def _rwfeat_kernel(p_ref, pp_ref, pn_ref, mu_ref, w0_ref, w2_ref, a0_ref, a2_ref, g2_ref,
                   kk_ref, rk_ref, ones_ref, f_ref, g_ref, bonus_ref, *, nct, nt):
    t = pl.program_id(1)
    p = p_ref[0]
    first = jnp.logical_or(t == 0, t == nct)
    last = jnp.logical_or(t == nct - 1, t == nt - 1)
    prev_row = jnp.where(first, 0.0, pp_ref[0, 7:8, :])
    next_row = jnp.where(last, 0.0, pn_ref[0, 0:1, :])
    rows = lax.broadcasted_iota(jnp.int32, p.shape, 0)
    prev = jnp.where(rows == 0, prev_row, pltpu.roll(p, 1, axis=0))
    nxt = jnp.where(rows == TM - 1, next_row, pltpu.roll(p, TM - 1, axis=0))
    mu = mu_ref[...]
    xs = p + mu[0:1] * (prev - p) + mu[1:2] * (nxt - p)

    D3 = 3 * RW_DIM
    r = xs[:, 0:RW_DIM]
    k = xs[:, RW_DIM:2 * RW_DIM]
    v = xs[:, 2 * RW_DIM:D3]
    w_lo = xs[:, D3:D3 + 2 * RW_LORA]
    a_lo = xs[:, D3 + 2 * RW_LORA:D3 + 4 * RW_LORA]
    g_pre = xs[:, D3 + 4 * RW_LORA:]

    lw = _dot(jnp.tanh(w_lo).astype(BF16), w2_ref[...]) + w0_ref[...]
    logw = jnp.minimum(lw, 0.0) - jnp.log(1.0 + jnp.exp(-jnp.abs(lw))) - 0.5
    decay = jnp.exp(-jnp.exp(logw))
    a = _sigmoid(_dot(a_lo.astype(BF16), a2_ref[...]) + a0_ref[...])

    kkr = k * kk_ref[...]
    kk = kkr * lax.rsqrt(_group_sum(kkr * kkr, ones_ref) + 1e-12)
    g_ref[0] = _dot(_sigmoid(g_pre).astype(BF16), g2_ref[...])
    bonus_ref[0] = _group_sum(r * k * rk_ref[...], ones_ref) * v

    comps = (r, k, v, kk, decay[:, :RW_DIM], a[:, :RW_DIM], decay[:, RW_DIM:], a[:, RW_DIM:])
    N = RW_HEAD_DIM
    for h in range(RW_HEADS):
        for ci, comp in enumerate(comps):
            col = (h * len(comps) + ci) * N
            f_ref[:, col:col + N] = comp[:, h * N:(h + 1) * N]


def _rw_features(p_rw, prm, nct):
    B, S, C = p_rw.shape
    nt = S // TM
    hb = TM // 8
    last_hb = S // 8 - 1
    full = lambda shape: pl.BlockSpec(shape, lambda b, t: (0,) * len(shape))
    return pl.pallas_call(
        functools.partial(_rwfeat_kernel, nct=nct, nt=nt),
        grid=(B, nt),
        in_specs=[
            pl.BlockSpec((1, TM, C), lambda b, t: (b, t, 0)),
            pl.BlockSpec((1, 8, C), lambda b, t: (b, jnp.maximum(t * hb - 1, 0), 0)),
            pl.BlockSpec((1, 8, C), lambda b, t: (b, jnp.minimum((t + 1) * hb, last_hb), 0)),
            full((2, C)),
            full((1, 2 * RW_DIM)), full((2 * RW_LORA, 2 * RW_DIM)),
            full((1, 2 * RW_DIM)), full((2 * RW_LORA, 2 * RW_DIM)),
            full((RW_G_LORA, RW_DIM)),
            full((1, RW_DIM)), full((1, RW_DIM)),
            full((RW_DIM, RW_DIM)),
        ],
        out_specs=[
            pl.BlockSpec((TM, 8 * RW_DIM), lambda b, t: (t, b)),
            pl.BlockSpec((1, TM, RW_DIM), lambda b, t: (b, t, 0)),
            pl.BlockSpec((1, TM, RW_DIM), lambda b, t: (b, t, 0)),
        ],
        out_shape=[
            jax.ShapeDtypeStruct((S, B * 8 * RW_DIM), F32),
            jax.ShapeDtypeStruct((B, S, RW_DIM), F32),
            jax.ShapeDtypeStruct((B, S, RW_DIM), F32),
        ],
        compiler_params=_cparams(("arbitrary", "arbitrary")),
        name="rw_features",
    )(p_rw, p_rw, p_rw, prm["mu"], prm["w0"], prm["w2"], prm["a0"], prm["a2"], prm["g2"],
      prm["kk"], prm["rk"], prm["ones"])


def _scan_kernel(fs_ref, fd_ref, ka_ref, o_ref, s_ref, tmp_ref, *, tb):
    d = pl.program_id(0)
    g = pl.program_id(1)
    N = RW_HEAD_DIM

    @pl.when(g == 0)
    def _():
        s_ref[...] = jnp.zeros_like(s_ref)

    ka = ka_ref[...]

    def step(i, carry):
        tt = jnp.where(d == 0, i, tb - 1 - i)
        r = fs_ref[tt, 0]
        k = fs_ref[tt, 1]
        v = fs_ref[tt, 2]
        kk = fs_ref[tt, 3]
        w = fd_ref[tt, 0]
        a = fd_ref[tt, 1]
        b = a * kk
        kd = k * (1.0 + (a - 1.0) * ka)
        wr = w * r
        br = jnp.sum(b * r, axis=0, keepdims=True)
        kr = jnp.sum(kd * r, axis=0, keepdims=True)
        tmp_ref[0] = wr
        tmp_ref[1] = b
        tmp_ref[2] = kd
        sa = [jnp.zeros_like(v), jnp.zeros_like(v)]
        op = [jnp.zeros_like(v), jnp.zeros_like(v)]
        for j in range(N):
            sk = s_ref[j]
            sa[j % 2] = sa[j % 2] + sk * fs_ref[tt, 3, pl.ds(j, 1), :]
            op[j % 2] = op[j % 2] + sk * tmp_ref[0, pl.ds(j, 1), :]
        sa = sa[0] + sa[1]
        op = op[0] + op[1]
        for j in range(N):
            s_ref[j] = (s_ref[j] * fd_ref[tt, 0, pl.ds(j, 1), :]
                        - sa * tmp_ref[1, pl.ds(j, 1), :]
                        + v * tmp_ref[2, pl.ds(j, 1), :])
        o_ref[0, tt] = op - sa * br + v * kr
        return carry

    lax.fori_loop(0, tb, step, 0)


def _rw_scan(ft, ka_t, n_ctx):
    S, _, N, BH = ft.shape
    tb = SCAN_TB
    nb = S // tb
    ncb = n_ctx // tb

    def tblk(d, g):
        bwd = jnp.where(g < ncb, ncb - 1 - g, nb - 1 - g + ncb)
        return jnp.where(d == 0, g, bwd)

    return pl.pallas_call(
        functools.partial(_scan_kernel, tb=tb),
        grid=(2, nb),
        in_specs=[
            pl.BlockSpec((tb, 4, N, BH), lambda d, g: (tblk(d, g), 0, 0, 0)),
            pl.BlockSpec((tb, 2, N, BH), lambda d, g: (tblk(d, g), 2 + d, 0, 0)),
            pl.BlockSpec((N, BH), lambda d, g: (0, 0)),
        ],
        out_specs=pl.BlockSpec((1, tb, N, BH), lambda d, g: (d, tblk(d, g), 0, 0)),
        out_shape=jax.ShapeDtypeStruct((2, S, N, BH), F32),
        scratch_shapes=[pltpu.VMEM((N, N, BH), F32), pltpu.VMEM((3, N, BH), F32)],
        compiler_params=_cparams(("arbitrary", "arbitrary")),
        name="rw_scan",
    )(ft, ft, ka_t)


def _mlaprep_kernel(p_ref, cos_ref, sin_ref, qn_ref, kvn_ref, wuq_ref, wukv_ref, q_ref, k_ref, v_ref):
    p = p_ref[0]
    cos = cos_ref[...]
    sin = sin_ref[...]
    scale = (MLA_NOPE + MLA_ROPE) ** -0.5

    def rms(x, g):
        return x * lax.rsqrt(jnp.mean(x * x, axis=-1, keepdims=True) + 1e-6) * g

    q = _dot(rms(p[:, :MLA_Q_LORA], qn_ref[...]).astype(BF16), wuq_ref[...])
    kv = _dot(rms(p[:, MLA_Q_LORA:MLA_Q_LORA + MLA_KV_LORA], kvn_ref[...]).astype(BF16), wukv_ref[...])
    c0 = MLA_Q_LORA + MLA_KV_LORA
    kr = (p[:, c0:c0 + 128] * cos + p[:, c0 + 128:c0 + 256] * sin).astype(BF16)
    for h in range(MLA_HEADS):
        qb = h * 384
        q_ref[0, :, h * 256:h * 256 + 128] = (q[:, qb:qb + 128] * scale).astype(BF16)
        q_ref[0, :, h * 256 + 128:h * 256 + 256] = (
            (q[:, qb + 128:qb + 256] * cos + q[:, qb + 256:qb + 384] * sin) * scale).astype(BF16)
        k_ref[0, :, h * 256:h * 256 + 128] = kv[:, h * 256:h * 256 + 128].astype(BF16)
        k_ref[0, :, h * 256 + 128:h * 256 + 256] = kr
        v_ref[0, :, h * 128:(h + 1) * 128] = kv[:, h * 256 + 128:h * 256 + 256].astype(BF16)


def _mla_prep(p_mla, cos128, sin128, prm):
    B, S, C = p_mla.shape
    full = lambda shape: pl.BlockSpec(shape, lambda b, t: (0,) * len(shape))
    H = MLA_HEADS
    return pl.pallas_call(
        _mlaprep_kernel,
        grid=(B, S // TM),
        in_specs=[
            pl.BlockSpec((1, TM, C), lambda b, t: (b, t, 0)),
            pl.BlockSpec((TM, 128), lambda b, t: (t, 0)),
            pl.BlockSpec((TM, 128), lambda b, t: (t, 0)),
            full((1, MLA_Q_LORA)), full((1, MLA_KV_LORA)),
            full((MLA_Q_LORA, H * 384)), full((MLA_KV_LORA, H * 256)),
        ],
        out_specs=[
            pl.BlockSpec((1, TM, H * 256), lambda b, t: (b, t, 0)),
            pl.BlockSpec((1, TM, H * 256), lambda b, t: (b, t, 0)),
            pl.BlockSpec((1, TM, H * 128), lambda b, t: (b, t, 0)),
        ],
        out_shape=[
            jax.ShapeDtypeStruct((B, S, H * 256), BF16),
            jax.ShapeDtypeStruct((B, S, H * 256), BF16),
            jax.ShapeDtypeStruct((B, S, H * 128), BF16),
        ],
        compiler_params=_cparams(("arbitrary", "arbitrary")),
        name="mla_prep",
    )(p_mla, cos128, sin128, prm["q_norm"], prm["kv_norm"], prm["w_uq"], prm["w_ukv"])


def _mla_attn_kernel(q_ref, k_ref, v_ref, o_ref, *, t0, nct, n_ctx):
    t = pl.program_id(1) + t0

    def attend(n_keys):
        for h in range(MLA_HEADS):
            q = q_ref[0, :, h * 256:(h + 1) * 256]
            k = k_ref[0, 0:n_keys, h * 256:(h + 1) * 256]
            s = lax.dot_general(q, k, (((1,), (1,)), ((), ())), preferred_element_type=F32)
            m = jnp.max(s, axis=-1, keepdims=True)
            e = jnp.exp(s - m)
            l = jnp.sum(e, axis=-1, keepdims=True)
            o = _dot(e.astype(BF16), v_ref[0, 0:n_keys, h * 128:(h + 1) * 128])
            o_ref[0, :, h * 128:(h + 1) * 128] = (o / l).astype(BF16)

    S = k_ref.shape[1]
    if t0 < nct:
        @pl.when(t < nct)
        def _():
            attend(n_ctx)

        @pl.when(t >= nct)
        def _():
            attend(S)
    else:
        attend(S)


def _mla_attn(q, k, v, t0, nct, n_ctx):
    B, S, _ = q.shape
    nq = S // TM - t0
    H = MLA_HEADS
    return pl.pallas_call(
        functools.partial(_mla_attn_kernel, t0=t0, nct=nct, n_ctx=n_ctx),
        grid=(B, nq),
        in_specs=[
            pl.BlockSpec((1, TM, H * 256), lambda b, t: (b, t + t0, 0)),
            pl.BlockSpec((1, S, H * 256), lambda b, t: (b, 0, 0)),
            pl.BlockSpec((1, S, H * 128), lambda b, t: (b, 0, 0)),
        ],
        out_specs=pl.BlockSpec((1, TM, H * 128), lambda b, t: (b, t, 0)),
        out_shape=jax.ShapeDtypeStruct((B, nq * TM, H * 128), BF16),
        compiler_params=_cparams(("arbitrary", "arbitrary")),
        name="mla_attn",
    )(q, k, v)


NA_QR = 4
NA_KR = NA_WIN_R + NA_QR
NA_QB = NA_QR * GRID_W


def _na_kernel(p_ref, bias_ref, o_ref, *, n_ctx, rows, with_ctx):
    s_id = pl.program_id(1)
    W = GRID_W
    n_loc = NA_KR * W
    scale = NA_HEAD_DIM ** -0.5
    nq_ctx = n_ctx // NA_QB
    nblk = rows // NA_QR

    def heads_out(q, parts):
        outs = []
        for h in range(NA_HEADS):
            hs = slice(h * NA_HEAD_DIM, (h + 1) * NA_HEAD_DIM)
            qh = q[:, hs] * scale
            ss = []
            for kx, vx, bias in parts:
                s = lax.dot_general(qh, kx[:, hs], (((1,), (1,)), ((), ())), preferred_element_type=F32)
                if bias is not None:
                    s = s + bias(h)
                ss.append(s)
            m = ss[0].max(axis=-1, keepdims=True)
            for s in ss[1:]:
                m = jnp.maximum(m, s.max(axis=-1, keepdims=True))
            acc = 0.0
            l = 0.0
            for s, (kx, vx, bias) in zip(ss, parts):
                e = jnp.exp(s - m)
                l = l + jnp.sum(e, axis=-1, keepdims=True)
                acc = acc + _dot(e.astype(BF16), vx[:, hs])
            outs.append(acc / l)
        return jnp.concatenate(outs, axis=-1).astype(BF16)

    k_c = p_ref[0, 0:n_ctx, NA_DIM:2 * NA_DIM]
    v_c = p_ref[0, 0:n_ctx, 2 * NA_DIM:3 * NA_DIM]

    def lat_block(j):
        i0 = j * NA_QR
        k_start = jnp.clip(i0 - NA_WIN_R // 2, 0, rows - NA_KR)
        pat = jnp.where(j == 0, 0, jnp.where(j == nblk - 1, 2, 1))
        q0 = pl.multiple_of(n_ctx + i0 * W, NA_QB)
        k0 = pl.multiple_of(n_ctx + k_start * W, W)
        q = p_ref[0, pl.ds(q0, NA_QB), 0:NA_DIM]
        k_l = p_ref[0, pl.ds(k0, n_loc), NA_DIM:2 * NA_DIM]
        v_l = p_ref[0, pl.ds(k0, n_loc), 2 * NA_DIM:3 * NA_DIM]
        o_ref[0] = heads_out(q, [(k_l, v_l, lambda h: bias_ref[pat, h]), (k_c, v_c, None)])

    if with_ctx:
        @pl.when(s_id < nq_ctx)
        def _():
            q0 = pl.multiple_of(s_id * NA_QB, NA_QB)
            q = p_ref[0, pl.ds(q0, NA_QB), 0:NA_DIM]
            o_ref[0] = heads_out(q, [(k_c, v_c, None)])

        @pl.when(s_id >= nq_ctx)
        def _():
            lat_block(s_id - nq_ctx)
    else:
        lat_block(s_id)


def _na_attn(p_na, bias_tab, n_ctx, with_ctx):
    B, S, C = p_na.shape
    T = S - n_ctx
    rows = T // GRID_W
    assert rows % NA_QR == 0 and rows >= NA_KR and n_ctx % NA_QB == 0
    nsteps = rows // NA_QR + (n_ctx // NA_QB if with_ctx else 0)
    return pl.pallas_call(
        functools.partial(_na_kernel, n_ctx=n_ctx, rows=rows, with_ctx=with_ctx),
        grid=(B, nsteps),
        in_specs=[
            pl.BlockSpec((1, S, C), lambda b, s: (b, 0, 0)),
            pl.BlockSpec(bias_tab.shape, lambda b, s: (0, 0, 0, 0)),
        ],
        out_specs=pl.BlockSpec((1, NA_QB, NA_DIM), lambda b, s: (b, s, 0)),
        out_shape=jax.ShapeDtypeStruct((B, nsteps * NA_QB, NA_DIM), BF16),
        compiler_params=_cparams(("arbitrary", "arbitrary")),
        name="na_attn",
    )(p_na, bias_tab)


def _na_bias_table(rpb):
    col = jnp.arange(GRID_W)
    c_start = jnp.clip(col - NA_WIN_C // 2, 0, GRID_W - NA_WIN_C)
    in_win = (col[None, :] >= c_start[:, None]) & (col[None, :] < c_start[:, None] + NA_WIN_C)
    dc_idx = jnp.clip(col[None, :] - col[:, None] + NA_WIN_C - 1, 0, 2 * NA_WIN_C - 2)
    qa = jnp.arange(NA_QR)[:, None]
    kc = jnp.arange(NA_KR)[None, :]
    pats = []
    for pat in range(3):
        off = (NA_WIN_R // 2) * pat
        first = (0 * qa, qa, 0 * qa + NA_WIN_R // 2)[pat]
        row_ok = (kc >= first) & (kc < first + NA_WIN_R)
        dr_idx = jnp.clip(kc - qa - off + NA_WIN_R - 1, 0, 2 * NA_WIN_R - 2)
        bias = rpb[:, dr_idx][:, :, :, dc_idx]
        ok = row_ok[:, :, None, None] & in_win[None, None, :, :]
        bias = jnp.where(ok[None], bias, NEG_INF)
        bias = jnp.transpose(bias, (0, 1, 3, 2, 4))
        pats.append(bias.reshape(NA_HEADS, NA_QB, NA_KR * GRID_W))
    return jnp.stack(pats).astype(F32)


def _outproj_kernel(orw_ref, bonus_ref, g_ref, mla_ref, na_ref, x_ref, mod_ref, gnw_ref, gnb_ref, ones_ref,
                    wout_ref, ln1g_ref, ln1b_ref, rwh_ref, rwl_ref, rb_ref,
                    x1_ref, h_ref, lg_ref, *, dn_alpha):
    o = orw_ref[0] + orw_ref[1]
    inv_n = 1.0 / RW_HEAD_DIM
    mu = _group_sum(o, ones_ref) * inv_n
    dlt = o - mu
    var = _group_sum(dlt * dlt, ones_ref) * inv_n
    on = dlt * lax.rsqrt(var + RW_GN_EPS) * gnw_ref[...] + gnb_ref[...]
    rw_y = ((on + bonus_ref[0]) * g_ref[0]).astype(BF16)
    y = (_dot(rw_y, wout_ref[0:RW_DIM, :])
         + _dot(mla_ref[0], wout_ref[RW_DIM:RW_DIM + MLA_DIM, :])
         + _dot(na_ref[0], wout_ref[RW_DIM + MLA_DIM:, :]))
    m = mod_ref[0, 0]
    x1 = _ln(dn_alpha * x_ref[0] + m[2:3] * y) * ln1g_ref[...] + ln1b_ref[...]
    x1_ref[0] = x1
    h = _ln(x1) * (1.0 + m[4:5]) + m[3:4]
    h_ref[0] = h.astype(BF16)
    lg_ref[0] = _dot3(h, rwh_ref[...], rwl_ref[...]) + rb_ref[...]


def _out_proj(o_rw, bonus, g, mla_o, na_o, xa, mod6, prm, t0, nct, dn_alpha):
    B, S, D = xa.shape
    nt = S // TM - t0
    So = nt * TM
    kind = lambda t: (t + t0 >= nct).astype(jnp.int32)
    full = lambda shape: pl.BlockSpec(shape, lambda b, t: (0,) * len(shape))
    off = lambda C: pl.BlockSpec((1, TM, C), lambda b, t: (b, t + t0, 0))
    own = lambda C: pl.BlockSpec((1, TM, C), lambda b, t: (b, t, 0))
    return pl.pallas_call(
        functools.partial(_outproj_kernel, dn_alpha=dn_alpha),
        grid=(B, nt),
        in_specs=[
            pl.BlockSpec((2, TM, RW_DIM), lambda b, t: (0, t + t0, b)),
            off(RW_DIM), off(RW_DIM), own(MLA_DIM), own(NA_DIM), off(D),
            pl.BlockSpec((1, 1, 6, D), lambda b, t: (b, kind(t), 0, 0)),
            full((1, RW_DIM)), full((1, RW_DIM)), full((RW_DIM, RW_DIM)),
            full((D, D)), full((1, D)), full((1, D)),
            full((D, 128)), full((D, 128)), full((1, 128)),
        ],
        out_specs=[own(D), own(D), own(128)],
        out_shape=[
            jax.ShapeDtypeStruct((B, So, D), F32),
            jax.ShapeDtypeStruct((B, So, D), BF16),
            jax.ShapeDtypeStruct((B, So, 128), F32),
        ],
        compiler_params=_cparams(("arbitrary", "arbitrary")),
        name="out_proj",
    )(o_rw, bonus, g, mla_o, na_o, xa, mod6, prm["gn_w"], prm["gn_b"], prm["ones"],
      prm["w_out"], prm["ln1_g"], prm["ln1_b"], prm["router_hi"], prm["router_lo"], prm["router_b"])


def _moe_kernel(be_ref, nu_ref, x_ref, wgu_ref, bgu_ref, wdn_ref, bdn_ref, y_ref):
    i = pl.program_id(0)
    F = wdn_ref.shape[1]

    @pl.when(i < nu_ref[0])
    def _():
        gu = _dot(x_ref[...], wgu_ref[0]) + bgu_ref[0]
        glu = jnp.minimum(gu[:, :F], SWIGLU_LIMIT)
        lin = jnp.clip(gu[:, F:], -SWIGLU_LIMIT, SWIGLU_LIMIT)
        act = glu * _sigmoid(SWIGLU_ALPHA * glu) * (lin + 1.0)
        y = _dot(act.astype(BF16), wdn_ref[0]) + bdn_ref[0]
        y_ref[...] = y.astype(BF16)

    @pl.when(i >= nu_ref[0])
    def _():
        y_ref[...] = jnp.zeros_like(y_ref)


def _moe_ffn(x_sorted, block_e, n_used, w_gu, b_gu, w_dn, b_dn):
    n_pad, D = x_sorted.shape
    E, _, F2 = w_gu.shape
    F = F2 // 2
    n_blocks = n_pad // MOE_BLK
    grid_spec = pltpu.PrefetchScalarGridSpec(
        num_scalar_prefetch=2,
        grid=(n_blocks,),
        in_specs=[
            pl.BlockSpec((MOE_BLK, D), lambda i, be, nu: (i, 0)),
            pl.BlockSpec((1, D, F2), lambda i, be, nu: (be[i], 0, 0)),
            pl.BlockSpec((1, 1, F2), lambda i, be, nu: (be[i], 0, 0)),
            pl.BlockSpec((1, F, D), lambda i, be, nu: (be[i], 0, 0)),
            pl.BlockSpec((1, 1, D), lambda i, be, nu: (be[i], 0, 0)),
        ],
        out_specs=pl.BlockSpec((MOE_BLK, D), lambda i, be, nu: (i, 0)),
    )
    return pl.pallas_call(
        _moe_kernel,
        grid_spec=grid_spec,
        out_shape=jax.ShapeDtypeStruct((n_pad, D), BF16),
        compiler_params=_cparams(("arbitrary",)),
        name="moe_ffn",
    )(block_e, n_used, x_sorted, w_gu, b_gu.reshape(E, 1, F2), w_dn, b_dn.reshape(E, 1, D))


def _route(logits, n_tok):
    top_v, top_i = lax.top_k(logits, TOP_K)
    gates = jax.nn.softmax(top_v, axis=-1)
    n_assign = n_tok * TOP_K
    flat_e = top_i.reshape(-1)
    onehot = (flat_e[:, None] == jnp.arange(N_EXPERTS)[None, :]).astype(jnp.int32)
    rank = jnp.take_along_axis(jnp.cumsum(onehot, axis=0), flat_e[:, None], axis=1)[:, 0] - 1
    counts = jnp.sum(onehot, axis=0)
    padded = (counts + MOE_BLK - 1) // MOE_BLK * MOE_BLK
    ends_p = jnp.cumsum(padded)
    dest = (ends_p - padded)[flat_e] + rank
    n_blocks = -(-(n_assign + N_EXPERTS * (MOE_BLK - 1)) // MOE_BLK)
    n_pad = n_blocks * MOE_BLK
    flat_tok = (jnp.arange(n_assign) // TOP_K).astype(jnp.int32)
    tok_pad = jnp.zeros((n_pad,), jnp.int32).at[dest].set(flat_tok, unique_indices=True)
    block_e = jnp.minimum(jnp.searchsorted(ends_p, jnp.arange(n_blocks) * MOE_BLK, side='right'),
                          N_EXPERTS - 1).astype(jnp.int32)
    n_used = (ends_p[-1] // MOE_BLK).astype(jnp.int32).reshape(1)
    return gates, dest.reshape(n_tok, TOP_K).astype(jnp.int32), tok_pad, block_e, n_used


def _final_kernel(x1_ref, f_ref, mod_ref, g_ref, b_ref, o_ref, *, dn_alpha):
    m = mod_ref[0, 0]
    o_ref[0] = _ln(dn_alpha * x1_ref[0] + m[5:6] * f_ref[0]) * g_ref[...] + b_ref[...]


def _final(x1, f, mod6, ln_g, ln_b, t0, nct, dn_alpha):
    B, So, D = x1.shape
    kind = lambda t: (t + t0 >= nct).astype(jnp.int32)
    blk = pl.BlockSpec((1, TM, D), lambda b, t: (b, t, 0))
    vec = pl.BlockSpec((1, D), lambda b, t: (0, 0))
    return pl.pallas_call(
        functools.partial(_final_kernel, dn_alpha=dn_alpha),
        grid=(B, So // TM),
        in_specs=[blk, blk, pl.BlockSpec((1, 1, 6, D), lambda b, t: (b, kind(t), 0, 0)), vec, vec],
        out_specs=blk,
        out_shape=jax.ShapeDtypeStruct((B, So, D), F32),
        compiler_params=_cparams(("arbitrary", "arbitrary")),
        name="ffn_residual",
    )(x1, f, mod6, ln_g, ln_b)


def _rope_tables(n_ctx, T):
    t = jnp.arange(T)
    row = (t // GRID_W).astype(F32)
    col = (t % GRID_W).astype(F32)
    n_freq = MLA_ROPE // 4
    inv = ROPE_BASE ** (-jnp.arange(n_freq, dtype=F32) / n_freq)
    ang = jnp.concatenate([row[:, None] * inv, col[:, None] * inv], axis=-1)
    cos, sin = jnp.cos(ang), jnp.sin(ang)
    z = jnp.zeros((T, 128 - MLA_ROPE), F32)
    cos128 = jnp.concatenate([cos, cos, z], axis=-1)
    sin128 = jnp.concatenate([-sin, sin, z], axis=-1)
    cos_c = jnp.concatenate([jnp.ones((n_ctx, MLA_ROPE), F32), jnp.zeros((n_ctx, 128 - MLA_ROPE), F32)], -1)
    sin_c = jnp.zeros((n_ctx, 128), F32)
    return jnp.concatenate([cos_c, cos128], 0), jnp.concatenate([sin_c, sin128], 0)


def _rope_slabs(w):
    ev, od = w[:, 0::2], w[:, 1::2]
    z = jnp.zeros((w.shape[0], 128 - MLA_ROPE), w.dtype)
    return jnp.concatenate([ev, od, z, od, ev, z], axis=-1)


def _blockdiag2(m):
    z = jnp.zeros_like(m[0])
    return jnp.concatenate([jnp.concatenate([m[0], z], 1), jnp.concatenate([z, m[1]], 1)], 0)


def kernel(x, c, ctx, c_ctx, ada_w, ada_b, w_in, rw_mu, rw_w0, rw_w2, rw_a0, rw_a2, rw_g2, rw_kk, rw_ka, rw_rk, rw_gn_w, rw_gn_b, mla_q_norm, mla_kv_norm, mla_w_uq, mla_w_ukv, na_rpb, w_out, ln1_g, ln1_b, router_w, router_b, w_gu, b_gu, w_dn, b_dn, ln2_g, ln2_b):
    B, T, D = x.shape
    n_ctx = ctx.shape[1]
    depth = ada_w.shape[0]
    S = n_ctx + T
    assert n_ctx % TM == 0 and T % TM == 0 and T % GRID_W == 0 and T // GRID_W >= NA_WIN_R
    nct = n_ctx // TM
    BH = B * RW_HEADS
    dn_alpha = (2 * depth) ** 0.25
    F = w_dn.shape[2]

    R = (B + 1 + 7) // 8 * 8
    cond = jnp.zeros((R, D), F32).at[:B].set(c).at[B].set(c_ctx)
    mod = _ada_mod(cond, ada_w, ada_b)
    mod_l = mod[:, :B].reshape(depth, B, 1, 6, D)
    mod_c = jnp.broadcast_to(mod[:, B].reshape(depth, 1, 1, 6, D), (depth, B, 1, 6, D))
    mod6 = jnp.concatenate([mod_c, mod_l], axis=2)

    cos128, sin128 = _rope_tables(n_ctx, T)
    ones_blk = jnp.kron(jnp.eye(RW_HEADS, dtype=F32), jnp.ones((RW_HEAD_DIM, RW_HEAD_DIM), F32)).astype(BF16)

    xa = jnp.concatenate([ctx, x], axis=1)
    for l in range(depth):
        need_ctx = l < depth - 1
        t0 = 0 if need_ctx else nct

        wi = w_in[l]
        c_m = RW_COLS
        w_in_ext = jnp.concatenate(
            [wi[:, :c_m + MLA_Q_LORA + MLA_KV_LORA], _rope_slabs(wi[:, c_m + MLA_Q_LORA + MLA_KV_LORA:c_m + MLA_COLS]),
             wi[:, c_m + MLA_COLS:]], axis=-1).astype(BF16)
        wuq = mla_w_uq[l].reshape(MLA_Q_LORA, MLA_HEADS, MLA_NOPE + MLA_ROPE)
        wuq_ext = jnp.concatenate(
            [jnp.concatenate([wuq[:, h, :MLA_NOPE], _rope_slabs(wuq[:, h, MLA_NOPE:])], -1) for h in range(MLA_HEADS)],
            axis=-1).astype(BF16)
        rw_prm = dict(
            mu=rw_mu[l],
            w0=rw_w0[l].reshape(1, 2 * RW_DIM), w2=_blockdiag2(rw_w2[l]).astype(BF16),
            a0=rw_a0[l].reshape(1, 2 * RW_DIM), a2=_blockdiag2(rw_a2[l]).astype(BF16),
            g2=rw_g2[l].astype(BF16), kk=rw_kk[l].reshape(1, RW_DIM), rk=rw_rk[l].reshape(1, RW_DIM),
            ones=ones_blk)
        mla_prm = dict(q_norm=mla_q_norm[l].reshape(1, -1), kv_norm=mla_kv_norm[l].reshape(1, -1),
                       w_uq=wuq_ext, w_ukv=mla_w_ukv[l].astype(BF16))
        rt = jnp.zeros((D, 128), F32).at[:, :N_EXPERTS].set(router_w[l])
        rt_hi = rt.astype(BF16)
        out_prm = dict(gn_w=rw_gn_w[l].reshape(1, -1), gn_b=rw_gn_b[l].reshape(1, -1), ones=ones_blk,
                       w_out=w_out[l].astype(BF16), ln1_g=ln1_g[l].reshape(1, -1), ln1_b=ln1_b[l].reshape(1, -1),
                       router_hi=rt_hi, router_lo=(rt - rt_hi.astype(F32)).astype(BF16),
                       router_b=jnp.zeros((1, 128), F32).at[0, :N_EXPERTS].set(router_b[l]))
        wgu = jnp.concatenate([w_gu[l][:, :, 0::2], w_gu[l][:, :, 1::2]], axis=-1).astype(BF16)
        bgu = jnp.concatenate([b_gu[l][:, 0::2], b_gu[l][:, 1::2]], axis=-1)
        wdn = w_dn[l].astype(BF16)

        p_rw, p_mla, p_na = _in_proj(xa, mod6[l], w_in_ext, nct)
        feat, g_gate, bonus = _rw_features(p_rw, rw_prm, nct)
        ft = jnp.swapaxes(feat.reshape(S, BH, 8 * RW_HEAD_DIM), 1, 2).reshape(S, 8, RW_HEAD_DIM, BH)
        ka_t = jnp.tile(rw_ka[l].reshape(RW_HEADS, RW_HEAD_DIM).T[:, None, :], (1, B, 1)).reshape(RW_HEAD_DIM, BH)
        o_scan = _rw_scan(ft, ka_t, n_ctx)
        o_rw = jnp.swapaxes(o_scan, 2, 3).reshape(2, S, B * RW_DIM)

        q, k, v = _mla_prep(p_mla, cos128, sin128, mla_prm)
        mla_o = _mla_attn(q, k, v, t0, nct, n_ctx)
        na_o = _na_attn(p_na, _na_bias_table(na_rpb[l]), n_ctx, need_ctx)

        x1, h, logits = _out_proj(o_rw, bonus, g_gate, mla_o, na_o, xa, mod6[l], out_prm, t0, nct, dn_alpha)

        So = x1.shape[1]
        n_tok = B * So
        gates, dest, tok_pad, block_e, n_used = _route(logits.reshape(n_tok, 128)[:, :N_EXPERTS], n_tok)
        x_sorted = jnp.take(h.reshape(n_tok, D), tok_pad, axis=0)
        y_sorted = _moe_ffn(x_sorted, block_e, n_used, wgu, bgu, wdn, b_dn[l])
        yk = jnp.take(y_sorted, dest.reshape(-1), axis=0).reshape(n_tok, TOP_K, D).astype(F32)
        f = jnp.sum(yk * gates[:, :, None], axis=1).reshape(B, So, D)

        xa = _final(x1, f, mod6[l], ln2_g[l].reshape(1, -1), ln2_b[l].reshape(1, -1), t0, nct, dn_alpha)
    return xa
```

```python
import functools

import numpy as np
import jax
import jax.numpy as jnp
from jax import lax
from jax.experimental import pallas as pl
from jax.experimental.pallas import tpu as pltpu

F32 = jnp.float32
BF16 = jnp.bfloat16

GRID_W = 64
RW_HEAD_DIM = 64
RW_HEADS = 4
RW_DIM = RW_HEADS * RW_HEAD_DIM
RW_LORA = 64
RW_G_LORA = 128
RW_GN_EPS = 64e-5
RW_COLS = 3 * RW_DIM + 4 * RW_LORA + RW_G_LORA
MLA_HEADS = 4
MLA_NOPE = 128
MLA_ROPE = 64
MLA_V = 128
MLA_Q_LORA = 256
MLA_KV_LORA = 128
MLA_COLS = MLA_Q_LORA + MLA_KV_LORA + MLA_ROPE
MLA_COLS_EXT = MLA_Q_LORA + MLA_KV_LORA + 256
MLA_DIM = MLA_HEADS * MLA_V
NA_HEADS = 4
NA_HEAD_DIM = 64
NA_DIM = NA_HEADS * NA_HEAD_DIM
NA_WIN_R = 8
NA_WIN_C = 16
NA_COLS = 3 * NA_DIM
ROPE_BASE = 10000.0
N_EXPERTS = 32
TOP_K = 4
SWIGLU_ALPHA = 1.702
SWIGLU_LIMIT = 7.0
NEG_INF = -1e30

TM = 256
SCAN_TB = 16
MOE_BLK = 512
VMEM_LIMIT = 56 * 1024 * 1024


def _cparams(sem):
    return pltpu.CompilerParams(dimension_semantics=sem, vmem_limit_bytes=VMEM_LIMIT)


def _ln(x, eps=1e-5):
    mu = jnp.mean(x, axis=-1, keepdims=True)
    d = x - mu
    var = jnp.mean(d * d, axis=-1, keepdims=True)
    return d * lax.rsqrt(var + eps)


def _dot(a, b):
    return jnp.dot(a, b, preferred_element_type=F32)


def _split(a):
    hi = a.astype(BF16)
    lo = (a - hi.astype(F32)).astype(BF16)
    return hi, lo


def _dot_hl(a, b_bf16):
    hi, lo = _split(a)
    return _dot(hi, b_bf16) + _dot(lo, b_bf16)


def _dot3(a, b_hi, b_lo):
    hi, lo = _split(a)
    return _dot(hi, b_hi) + _dot(lo, b_hi) + _dot(hi, b_lo)


def _sigmoid(x):
    return 1.0 / (1.0 + jnp.exp(-x))


def _ada_kernel(cond_ref, w_ref, b_ref, o_ref):
    c = cond_ref[...]
    s = c * _sigmoid(c)
    w = w_ref[0]
    w_hi, w_lo = _split(w)
    o_ref[0] = _dot3(s, w_hi, w_lo) + b_ref[0]


def _ada_mod(cond, ada_w, ada_b):
    L, D, N = ada_w.shape
    R = cond.shape[0]
    tn = 512
    return pl.pallas_call(
        _ada_kernel,
        grid=(L, N // tn),
        in_specs=[
            pl.BlockSpec((R, D), lambda l, j: (0, 0)),
            pl.BlockSpec((1, D, tn), lambda l, j: (l, 0, j)),
            pl.BlockSpec((1, 1, tn), lambda l, j: (l, 0, j)),
        ],
        out_specs=pl.BlockSpec((1, R, tn), lambda l, j: (l, 0, j)),
        out_shape=jax.ShapeDtypeStruct((L, R, N), F32),
        compiler_params=_cparams(("arbitrary", "arbitrary")),
        name="ada_mod",
    )(cond, ada_w, ada_b.reshape(L, 1, N))


def _win_kernel(x_ref, mod_ref, w_ref, prw_ref, pmla_ref, pna_ref):
    x = x_ref[0]
    m = mod_ref[0, 0]
    xm = _ln(x) * (1.0 + m[1:2]) + m[0:1]
    p = _dot(xm.astype(BF16), w_ref[...])
    prw_ref[0] = p[:, :RW_COLS]
    pmla_ref[0] = p[:, RW_COLS:RW_COLS + MLA_COLS_EXT]
    pna_ref[0] = p[:, RW_COLS + MLA_COLS_EXT:].astype(BF16)


def _in_proj(xa, mod6, w_in_ext, nct):
    B, S, D = xa.shape
    NC = w_in_ext.shape[1]
    kind = lambda t: (t >= nct).astype(jnp.int32)
    return pl.pallas_call(
        _win_kernel,
        grid=(B, S // TM),
        in_specs=[
            pl.BlockSpec((1, TM, D), lambda b, t: (b, t, 0)),
            pl.BlockSpec((1, 1, 6, D), lambda b, t: (b, kind(t), 0, 0)),
            pl.BlockSpec((D, NC), lambda b, t: (0, 0)),
        ],
        out_specs=[
            pl.BlockSpec((1, TM, RW_COLS), lambda b, t: (b, t, 0)),
            pl.BlockSpec((1, TM, MLA_COLS_EXT), lambda b, t: (b, t, 0)),
            pl.BlockSpec((1, TM, NA_COLS), lambda b, t: (b, t, 0)),
        ],
        out_shape=[
            jax.ShapeDtypeStruct((B, S, RW_COLS), F32),
            jax.ShapeDtypeStruct((B, S, MLA_COLS_EXT), F32),
            jax.ShapeDtypeStruct((B, S, NA_COLS), BF16),
        ],
        compiler_params=_cparams(("arbitrary", "arbitrary")),
        name="in_proj",
    )(xa, mod6, w_in_ext)


def _group_sum(x, ones_ref):
    return _dot_hl(x, ones_ref[...])


def _rwfeat_kernel(p_ref, pp_ref, pn_ref, mu_ref, w0_ref, w2_ref, a0_ref, a2_ref, g2_ref,
                   kk_ref, rk_ref, ones_ref, f_ref, g_ref, bonus_ref, *, nct, nt):
    t = pl.program_id(1)
    p = p_ref[0]
    first = jnp.logical_or(t == 0, t == nct)
    last = jnp.logical_or(t == nct - 1, t == nt - 1)
    prev_row = jnp.where(first, 0.0, pp_ref[0, 7:8, :])
    next_row = jnp.where(last, 0.0, pn_ref[0, 0:1, :])
    rows = lax.broadcasted_iota(jnp.int32, p.shape, 0)
    prev = jnp.where(rows == 0, prev_row, pltpu.roll(p, 1, axis=0))
    nxt = jnp.where(rows == TM - 1, next_row, pltpu.roll(p, TM - 1, axis=0))
    mu = mu_ref[...]
    xs = p + mu[0:1] * (prev - p) + mu[1:2] * (nxt - p)

    D3 = 3 * RW_DIM
    r = xs[:, 0:RW_DIM]
    k = xs[:, RW_DIM:2 * RW_DIM]
    v = xs[:, 2 * RW_DIM:D3]
    w_lo = xs[:, D3:D3 + 2 * RW_LORA]
    a_lo = xs[:, D3 + 2 * RW_LORA:D3 + 4 * RW_LORA]
    g_pre = xs[:, D3 + 4 * RW_LORA:]

    lw = _dot(jnp.tanh(w_lo).astype(BF16), w2_ref[...]) + w0_ref[...]
    logw = jnp.minimum(lw, 0.0) - jnp.log(1.0 + jnp.exp(-jnp.abs(lw))) - 0.5
    decay = jnp.exp(-jnp.exp(logw))
    a = _sigmoid(_dot(a_lo.astype(BF16), a2_ref[...]) + a0_ref[...])

    kkr = k * kk_ref[...]
    kk = kkr * lax.rsqrt(_group_sum(kkr * kkr, ones_ref) + 1e-12)
    g_ref[0] = _dot(_sigmoid(g_pre).astype(BF16), g2_ref[...])
    bonus_ref[0] = _group_sum(r * k * rk_ref[...], ones_ref) * v

    comps = (r, k, v, kk, decay[:, :RW_DIM], a[:, :RW_DIM], decay[:, RW_DIM:], a[:, RW_DIM:])
    N = RW_HEAD_DIM
    for h in range(RW_HEADS):
        for ci, comp in enumerate(comps):
            col = (h * len(comps) + ci) * N
            f_ref[:, col:col + N] = comp[:, h * N:(h + 1) * N]


def _rw_features(p_rw, prm, nct):
    B, S, C = p_rw.shape
    nt = S // TM
    hb = TM // 8
    last_hb = S // 8 - 1
    full = lambda shape: pl.BlockSpec(shape, lambda b, t: (0,) * len(shape))
    return pl.pallas_call(
        functools.partial(_rwfeat_kernel, nct=nct, nt=nt),
        grid=(B, nt),
        in_specs=[
            pl.BlockSpec((1, TM, C), lambda b, t: (b, t, 0)),
            pl.BlockSpec((1, 8, C), lambda b, t: (b, jnp.maximum(t * hb - 1, 0), 0)),
            pl.BlockSpec((1, 8, C), lambda b, t: (b, jnp.minimum((t + 1) * hb, last_hb), 0)),
            full((2, C)),
            full((1, 2 * RW_DIM)), full((2 * RW_LORA, 2 * RW_DIM)),
            full((1, 2 * RW_DIM)), full((2 * RW_LORA, 2 * RW_DIM)),
            full((RW_G_LORA, RW_DIM)),
            full((1, RW_DIM)), full((1, RW_DIM)),
            full((RW_DIM, RW_DIM)),
        ],
        out_specs=[
            pl.BlockSpec((TM, 8 * RW_DIM), lambda b, t: (t, b)),
            pl.BlockSpec((1, TM, RW_DIM), lambda b, t: (b, t, 0)),
            pl.BlockSpec((1, TM, RW_DIM), lambda b, t: (b, t, 0)),
        ],
        out_shape=[
            jax.ShapeDtypeStruct((S, B * 8 * RW_DIM), F32),
            jax.ShapeDtypeStruct((B, S, RW_DIM), F32),
            jax.ShapeDtypeStruct((B, S, RW_DIM), F32),
        ],
        compiler_params=_cparams(("arbitrary", "arbitrary")),
        name="rw_features",
    )(p_rw, p_rw, p_rw, prm["mu"], prm["w0"], prm["w2"], prm["a0"], prm["a2"], prm["g2"],
      prm["kk"], prm["rk"], prm["ones"])


def _scan_kernel(fs_ref, fd_ref, ka_ref, o_ref, s_ref, tmp_ref, *, tb):
    d = pl.program_id(0)
    g = pl.program_id(1)
    N = RW_HEAD_DIM

    @pl.when(g == 0)
    def _():
        s_ref[...] = jnp.zeros_like(s_ref)

    ka = ka_ref[...]

    def step(i, carry):
        tt = jnp.where(d == 0, i, tb - 1 - i)
        r = fs_ref[tt, 0]
        k = fs_ref[tt, 1]
        v = fs_ref[tt, 2]
        kk = fs_ref[tt, 3]
        w = fd_ref[tt, 0]
        a = fd_ref[tt, 1]
        b = a * kk
        kd = k * (1.0 + (a - 1.0) * ka)
        wr = w * r
        br = jnp.sum(b * r, axis=0, keepdims=True)
        kr = jnp.sum(kd * r, axis=0, keepdims=True)
        tmp_ref[0] = wr
        tmp_ref[1] = b
        tmp_ref[2] = kd
        sa = [jnp.zeros_like(v), jnp.zeros_like(v)]
        op = [jnp.zeros_like(v), jnp.zeros_like(v)]
        for j in range(N):
            sk = s_ref[j]
            sa[j % 2] = sa[j % 2] + sk * fs_ref[tt, 3, pl.ds(j, 1), :]
            op[j % 2] = op[j % 2] + sk * tmp_ref[0, pl.ds(j, 1), :]
        sa = sa[0] + sa[1]
        op = op[0] + op[1]
        for j in range(N):
            s_ref[j] = (s_ref[j] * fd_ref[tt, 0, pl.ds(j, 1), :]
                        - sa * tmp_ref[1, pl.ds(j, 1), :]
                        + v * tmp_ref[2, pl.ds(j, 1), :])
        o_ref[0, tt] = op - sa * br + v * kr
        return carry

    lax.fori_loop(0, tb, step, 0)


def _rw_scan(ft, ka_t, n_ctx):
    S, _, N, BH = ft.shape
    tb = SCAN_TB
    nb = S // tb
    ncb = n_ctx // tb

    def tblk(d, g):
        bwd = jnp.where(g < ncb, ncb - 1 - g, nb - 1 - g + ncb)
        return jnp.where(d == 0, g, bwd)

    return pl.pallas_call(
        functools.partial(_scan_kernel, tb=tb),
        grid=(2, nb),
        in_specs=[
            pl.BlockSpec((tb, 4, N, BH), lambda d, g: (tblk(d, g), 0, 0, 0)),
            pl.BlockSpec((tb, 2, N, BH), lambda d, g: (tblk(d, g), 2 + d, 0, 0)),
            pl.BlockSpec((N, BH), lambda d, g: (0, 0)),
        ],
        out_specs=pl.BlockSpec((1, tb, N, BH), lambda d, g: (d, tblk(d, g), 0, 0)),
        out_shape=jax.ShapeDtypeStruct((2, S, N, BH), F32),
        scratch_shapes=[pltpu.VMEM((N, N, BH), F32), pltpu.VMEM((3, N, BH), F32)],
        compiler_params=_cparams(("arbitrary", "arbitrary")),
        name="rw_scan",
    )(ft, ft, ka_t)


def _mlaprep_kernel(p_ref, cos_ref, sin_ref, qn_ref, kvn_ref, wuq_ref, wukv_ref, q_ref, k_ref, v_ref):
    p = p_ref[0]
    cos = cos_ref[...]
    sin = sin_ref[...]
    scale = (MLA_NOPE + MLA_ROPE) ** -0.5

    def rms(x, g):
        return x * lax.rsqrt(jnp.mean(x * x, axis=-1, keepdims=True) + 1e-6) * g

    q = _dot(rms(p[:, :MLA_Q_LORA], qn_ref[...]).astype(BF16), wuq_ref[...])
    kv = _dot(rms(p[:, MLA_Q_LORA:MLA_Q_LORA + MLA_KV_LORA], kvn_ref[...]).astype(BF16), wukv_ref[...])
    c0 = MLA_Q_LORA + MLA_KV_LORA
    kr = (p[:, c0:c0 + 128] * cos + p[:, c0 + 128:c0 + 256] * sin).astype(BF16)
    for h in range(MLA_HEADS):
        qb = h * 384
        q_ref[0, :, h * 256:h * 256 + 128] = (q[:, qb:qb + 128] * scale).astype(BF16)
        q_ref[0, :, h * 256 + 128:h * 256 + 256] = (
            (q[:, qb + 128:qb + 256] * cos + q[:, qb + 256:qb + 384] * sin) * scale).astype(BF16)
        k_ref[0, :, h * 256:h * 256 + 128] = kv[:, h * 256:h * 256 + 128].astype(BF16)
        k_ref[0, :, h * 256 + 128:h * 256 + 256] = kr
        v_ref[0, :, h * 128:(h + 1) * 128] = kv[:, h * 256 + 128:h * 256 + 256].astype(BF16)


def _mla_prep(p_mla, cos128, sin128, prm):
    B, S, C = p_mla.shape
    full = lambda shape: pl.BlockSpec(shape, lambda b, t: (0,) * len(shape))
    H = MLA_HEADS
    return pl.pallas_call(
        _mlaprep_kernel,
        grid=(B, S // TM),
        in_specs=[
            pl.BlockSpec((1, TM, C), lambda b, t: (b, t, 0)),
            pl.BlockSpec((TM, 128), lambda b, t: (t, 0)),
            pl.BlockSpec((TM, 128), lambda b, t: (t, 0)),
            full((1, MLA_Q_LORA)), full((1, MLA_KV_LORA)),
            full((MLA_Q_LORA, H * 384)), full((MLA_KV_LORA, H * 256)),
        ],
        out_specs=[
            pl.BlockSpec((1, TM, H * 256), lambda b, t: (b, t, 0)),
            pl.BlockSpec((1, TM, H * 256), lambda b, t: (b, t, 0)),
            pl.BlockSpec((1, TM, H * 128), lambda b, t: (b, t, 0)),
        ],
        out_shape=[
            jax.ShapeDtypeStruct((B, S, H * 256), BF16),
            jax.ShapeDtypeStruct((B, S, H * 256), BF16),
            jax.ShapeDtypeStruct((B, S, H * 128), BF16),
        ],
        compiler_params=_cparams(("arbitrary", "arbitrary")),
        name="mla_prep",
    )(p_mla, cos128, sin128, prm["q_norm"], prm["kv_norm"], prm["w_uq"], prm["w_ukv"])


def _mla_attn_kernel(q_ref, k_ref, v_ref, o_ref, *, t0, nct, n_ctx):
    t = pl.program_id(1) + t0

    def attend(n_keys):
        for h in range(MLA_HEADS):
            q = q_ref[0, :, h * 256:(h + 1) * 256]
            k = k_ref[0, 0:n_keys, h * 256:(h + 1) * 256]
            s = lax.dot_general(q, k, (((1,), (1,)), ((), ())), preferred_element_type=F32)
            m = jnp.max(s, axis=-1, keepdims=True)
            e = jnp.exp(s - m)
            l = jnp.sum(e, axis=-1, keepdims=True)
            o = _dot(e.astype(BF16), v_ref[0, 0:n_keys, h * 128:(h + 1) * 128])
            o_ref[0, :, h * 128:(h + 1) * 128] = (o / l).astype(BF16)

    S = k_ref.shape[1]
    if t0 < nct:
        @pl.when(t < nct)
        def _():
            attend(n_ctx)

        @pl.when(t >= nct)
        def _():
            attend(S)
    else:
        attend(S)


def _mla_attn(q, k, v, t0, nct, n_ctx):
    B, S, _ = q.shape
    nq = S // TM - t0
    H = MLA_HEADS
    return pl.pallas_call(
        functools.partial(_mla_attn_kernel, t0=t0, nct=nct, n_ctx=n_ctx),
        grid=(B, nq),
        in_specs=[
            pl.BlockSpec((1, TM, H * 256), lambda b, t: (b, t + t0, 0)),
            pl.BlockSpec((1, S, H * 256), lambda b, t: (b, 0, 0)),
            pl.BlockSpec((1, S, H * 128), lambda b, t: (b, 0, 0)),
        ],
        out_specs=pl.BlockSpec((1, TM, H * 128), lambda b, t: (b, t, 0)),
        out_shape=jax.ShapeDtypeStruct((B, nq * TM, H * 128), BF16),
        compiler_params=_cparams(("arbitrary", "arbitrary")),
        name="mla_attn",
    )(q, k, v)


NA_QR = 4
NA_KR = NA_WIN_R + NA_QR
NA_QB = NA_QR * GRID_W


def _na_kernel(p_ref, bias_ref, o_ref, *, n_ctx, rows, with_ctx):
    s_id = pl.program_id(1)
    W = GRID_W
    n_loc = NA_KR * W
    scale = NA_HEAD_DIM ** -0.5
    nq_ctx = n_ctx // NA_QB
    nblk = rows // NA_QR

    def heads_out(q, parts):
        outs = []
        for h in range(NA_HEADS):
            hs = slice(h * NA_HEAD_DIM, (h + 1) * NA_HEAD_DIM)
            qh = q[:, hs] * scale
            ss = []
            for kx, vx, bias in parts:
                s = lax.dot_general(qh, kx[:, hs], (((1,), (1,)), ((), ())), preferred_element_type=F32)
                if bias is not None:
                    s = s + bias(h)
                ss.append(s)
            m = ss[0].max(axis=-1, keepdims=True)
            for s in ss[1:]:
                m = jnp.maximum(m, s.max(axis=-1, keepdims=True))
            acc = 0.0
            l = 0.0
            for s, (kx, vx, bias) in zip(ss, parts):
                e = jnp.exp(s - m)
                l = l + jnp.sum(e, axis=-1, keepdims=True)
                acc = acc + _dot(e.astype(BF16), vx[:, hs])
            outs.append(acc / l)
        return jnp.concatenate(outs, axis=-1).astype(BF16)

    k_c = p_ref[0, 0:n_ctx, NA_DIM:2 * NA_DIM]
    v_c = p_ref[0, 0:n_ctx, 2 * NA_DIM:3 * NA_DIM]

    def lat_block(j):
        i0 = j * NA_QR
        k_start = jnp.clip(i0 - NA_WIN_R // 2, 0, rows - NA_KR)
        pat = jnp.where(j == 0, 0, jnp.where(j == nblk - 1, 2, 1))
        q0 = pl.multiple_of(n_ctx + i0 * W, NA_QB)
        k0 = pl.multiple_of(n_ctx + k_start * W, W)
        q = p_ref[0, pl.ds(q0, NA_QB), 0:NA_DIM]
        k_l = p_ref[0, pl.ds(k0, n_loc), NA_DIM:2 * NA_DIM]
        v_l = p_ref[0, pl.ds(k0, n_loc), 2 * NA_DIM:3 * NA_DIM]
        o_ref[0] = heads_out(q, [(k_l, v_l, lambda h: bias_ref[pat, h]), (k_c, v_c, None)])

    if with_ctx:
        @pl.when(s_id < nq_ctx)
        def _():
            q0 = pl.multiple_of(s_id * NA_QB, NA_QB)
            q = p_ref[0, pl.ds(q0, NA_QB), 0:NA_DIM]
            o_ref[0] = heads_out(q, [(k_c, v_c, None)])

        @pl.when(s_id >= nq_ctx)
        def _():
            lat_block(s_id - nq_ctx)
    else:
        lat_block(s_id)


def _na_attn(p_na, bias_tab, n_ctx, with_ctx):
    B, S, C = p_na.shape
    T = S - n_ctx
    rows = T // GRID_W
    assert rows % NA_QR == 0 and rows >= NA_KR and n_ctx % NA_QB == 0
    nsteps = rows // NA_QR + (n_ctx // NA_QB if with_ctx else 0)
    return pl.pallas_call(
        functools.partial(_na_kernel, n_ctx=n_ctx, rows=rows, with_ctx=with_ctx),
        grid=(B, nsteps),
        in_specs=[
            pl.BlockSpec((1, S, C), lambda b, s: (b, 0, 0)),
            pl.BlockSpec(bias_tab.shape, lambda b, s: (0, 0, 0, 0)),
        ],
        out_specs=pl.BlockSpec((1, NA_QB, NA_DIM), lambda b, s: (b, s, 0)),
        out_shape=jax.ShapeDtypeStruct((B, nsteps * NA_QB, NA_DIM), BF16),
        compiler_params=_cparams(("arbitrary", "arbitrary")),
        name="na_attn",
    )(p_na, bias_tab)


def _na_bias_table(rpb):
    col = jnp.arange(GRID_W)
    c_start = jnp.clip(col - NA_WIN_C // 2, 0, GRID_W - NA_WIN_C)
    in_win = (col[None, :] >= c_start[:, None]) & (col[None, :] < c_start[:, None] + NA_WIN_C)
    dc_idx = jnp.clip(col[None, :] - col[:, None] + NA_WIN_C - 1, 0, 2 * NA_WIN_C - 2)
    qa = jnp.arange(NA_QR)[:, None]
    kc = jnp.arange(NA_KR)[None, :]
    pats = []
    for pat in range(3):
        off = (NA_WIN_R // 2) * pat
        first = (0 * qa, qa, 0 * qa + NA_WIN_R // 2)[pat]
        row_ok = (kc >= first) & (kc < first + NA_WIN_R)
        dr_idx = jnp.clip(kc - qa - off + NA_WIN_R - 1, 0, 2 * NA_WIN_R - 2)
        bias = rpb[:, dr_idx][:, :, :, dc_idx]
        ok = row_ok[:, :, None, None] & in_win[None, None, :, :]
        bias = jnp.where(ok[None], bias, NEG_INF)
        bias = jnp.transpose(bias, (0, 1, 3, 2, 4))
        pats.append(bias.reshape(NA_HEADS, NA_QB, NA_KR * GRID_W))
    return jnp.stack(pats).astype(F32)


def _outproj_kernel(orw_ref, bonus_ref, g_ref, mla_ref, na_ref, x_ref, mod_ref, gnw_ref, gnb_ref, ones_ref,
                    wout_ref, ln1g_ref, ln1b_ref, rwh_ref, rwl_ref, rb_ref,
                    x1_ref, h_ref, lg_ref, *, dn_alpha):
    o = orw_ref[0] + orw_ref[1]
    inv_n = 1.0 / RW_HEAD_DIM
    mu = _group_sum(o, ones_ref) * inv_n
    dlt = o - mu
    var = _group_sum(dlt * dlt, ones_ref) * inv_n
    on = dlt * lax.rsqrt(var + RW_GN_EPS) * gnw_ref[...] + gnb_ref[...]
    rw_y = ((on + bonus_ref[0]) * g_ref[0]).astype(BF16)
    y = (_dot(rw_y, wout_ref[0:RW_DIM, :])
         + _dot(mla_ref[0], wout_ref[RW_DIM:RW_DIM + MLA_DIM, :])
         + _dot(na_ref[0], wout_ref[RW_DIM + MLA_DIM:, :]))
    m = mod_ref[0, 0]
    x1 = _ln(dn_alpha * x_ref[0] + m[2:3] * y) * ln1g_ref[...] + ln1b_ref[...]
    x1_ref[0] = x1
    h = _ln(x1) * (1.0 + m[4:5]) + m[3:4]
    h_ref[0] = h.astype(BF16)
    lg_ref[0] = _dot3(h, rwh_ref[...], rwl_ref[...]) + rb_ref[...]


def _out_proj(o_rw, bonus, g, mla_o, na_o, xa, mod6, prm, t0, nct, dn_alpha):
    B, S, D = xa.shape
    nt = S // TM - t0
    So = nt * TM
    kind = lambda t: (t + t0 >= nct).astype(jnp.int32)
    full = lambda shape: pl.BlockSpec(shape, lambda b, t: (0,) * len(shape))
    off = lambda C: pl.BlockSpec((1, TM, C), lambda b, t: (b, t + t0, 0))
    own = lambda C: pl.BlockSpec((1, TM, C), lambda b, t: (b, t, 0))
    return pl.pallas_call(
        functools.partial(_outproj_kernel, dn_alpha=dn_alpha),
        grid=(B, nt),
        in_specs=[
            pl.BlockSpec((2, TM, RW_DIM), lambda b, t: (0, t + t0, b)),
            off(RW_DIM), off(RW_DIM), own(MLA_DIM), own(NA_DIM), off(D),
            pl.BlockSpec((1, 1, 6, D), lambda b, t: (b, kind(t), 0, 0)),
            full((1, RW_DIM)), full((1, RW_DIM)), full((RW_DIM, RW_DIM)),
            full((D, D)), full((1, D)), full((1, D)),
            full((D, 128)), full((D, 128)), full((1, 128)),
        ],
        out_specs=[own(D), own(D), own(128)],
        out_shape=[
            jax.ShapeDtypeStruct((B, So, D), F32),
            jax.ShapeDtypeStruct((B, So, D), BF16),
            jax.ShapeDtypeStruct((B, So, 128), F32),
        ],
        compiler_params=_cparams(("arbitrary", "arbitrary")),
        name="out_proj",
    )(o_rw, bonus, g, mla_o, na_o, xa, mod6, prm["gn_w"], prm["gn_b"], prm["ones"],
      prm["w_out"], prm["ln1_g"], prm["ln1_b"], prm["router_hi"], prm["router_lo"], prm["router_b"])


GU_CHUNK = 256


def _moe_kernel(be_ref, nu_ref, x_ref, wgu_ref, bgu_ref, wdn_ref, bdn_ref, perm_ref, y_ref,
                wgu_s, wdn_s, act_s):
    i = pl.program_id(0)
    F = wdn_ref.shape[1]
    n_chunks = 2 * F // GU_CHUNK
    half = GU_CHUNK // 2
    valid = i < nu_ref[0]
    e = be_ref[i]
    new_expert = jnp.logical_or(i == 0, e != be_ref[jnp.maximum(i - 1, 0)])

    @pl.when(jnp.logical_and(valid, new_expert))
    def _():
        for c in range(n_chunks):
            cs = slice(c * GU_CHUNK, (c + 1) * GU_CHUNK)
            wgu_s[:, cs] = _dot(wgu_ref[0, :, cs].astype(BF16), perm_ref[...]).astype(BF16)
        wdn_s[...] = wdn_ref[0].astype(BF16)

    @pl.when(valid)
    def _():
        gu = _dot(x_ref[...], wgu_s[...]) + bgu_ref[0]
        for c in range(n_chunks):
            glu = jnp.minimum(gu[:, c * GU_CHUNK:c * GU_CHUNK + half], SWIGLU_LIMIT)
            lin = jnp.clip(gu[:, c * GU_CHUNK + half:(c + 1) * GU_CHUNK], -SWIGLU_LIMIT, SWIGLU_LIMIT)
            act_s[:, c * half:(c + 1) * half] = (glu * _sigmoid(SWIGLU_ALPHA * glu) * (lin + 1.0)).astype(BF16)
        y = _dot(act_s[...], wdn_s[...]) + bdn_ref[0]
        y_ref[...] = y.astype(BF16)

    @pl.when(jnp.logical_not(valid))
    def _():
        y_ref[...] = jnp.zeros_like(y_ref)


def _moe_ffn(x_sorted, block_e, n_used, w_gu, b_gu, w_dn, b_dn):
    n_pad, D = x_sorted.shape
    E, _, F2 = w_gu.shape
    F = F2 // 2
    n_blocks = n_pad // MOE_BLK
    half = GU_CHUNK // 2
    src = np.concatenate([2 * np.arange(half), 2 * np.arange(half) + 1])
    perm = jnp.asarray(np.eye(GU_CHUNK, dtype=np.float32)[:, src], BF16)
    bgu_p = jnp.swapaxes(b_gu.reshape(E, F2 // GU_CHUNK, half, 2), 2, 3).reshape(E, 1, F2)
    grid_spec = pltpu.PrefetchScalarGridSpec(
        num_scalar_prefetch=2,
        grid=(n_blocks,),
        in_specs=[
            pl.BlockSpec((MOE_BLK, D), lambda i, be, nu: (i, 0)),
            pl.BlockSpec((1, D, F2), lambda i, be, nu: (be[i], 0, 0)),
            pl.BlockSpec((1, 1, F2), lambda i, be, nu: (be[i], 0, 0)),
            pl.BlockSpec((1, F, D), lambda i, be, nu: (be[i], 0, 0)),
            pl.BlockSpec((1, 1, D), lambda i, be, nu: (be[i], 0, 0)),
            pl.BlockSpec((GU_CHUNK, GU_CHUNK), lambda i, be, nu: (0, 0)),
        ],
        out_specs=pl.BlockSpec((MOE_BLK, D), lambda i, be, nu: (i, 0)),
        scratch_shapes=[pltpu.VMEM((D, F2), BF16), pltpu.VMEM((F, D), BF16), pltpu.VMEM((MOE_BLK, F), BF16)],
    )
    return pl.pallas_call(
        _moe_kernel,
        grid_spec=grid_spec,
        out_shape=jax.ShapeDtypeStruct((n_pad, D), BF16),
        compiler_params=_cparams(("arbitrary",)),
        name="moe_ffn",
    )(block_e, n_used, x_sorted, w_gu, bgu_p, w_dn, b_dn.reshape(E, 1, D), perm)


def _route(logits, n_tok):
    top_v, top_i = lax.top_k(logits, TOP_K)
    gates = jax.nn.softmax(top_v, axis=-1)
    n_assign = n_tok * TOP_K
    flat_e = top_i.reshape(-1)
    onehot = (flat_e[:, None] == jnp.arange(N_EXPERTS)[None, :]).astype(jnp.int32)
    rank = jnp.take_along_axis(jnp.cumsum(onehot, axis=0), flat_e[:, None], axis=1)[:, 0] - 1
    counts = jnp.sum(onehot, axis=0)
    padded = (counts + MOE_BLK - 1) // MOE_BLK * MOE_BLK
    ends_p = jnp.cumsum(padded)
    dest = (ends_p - padded)[flat_e] + rank
    n_blocks = -(-(n_assign + N_EXPERTS * (MOE_BLK - 1)) // MOE_BLK)
    n_pad = n_blocks * MOE_BLK
    block_e = jnp.minimum(jnp.searchsorted(ends_p, jnp.arange(n_blocks) * MOE_BLK, side='right'),
                          N_EXPERTS - 1).astype(jnp.int32)
    n_used = (ends_p[-1] // MOE_BLK).astype(jnp.int32).reshape(1)
    order = jnp.argsort(flat_e, stable=True).astype(jnp.int32)
    slot_e = jnp.repeat(block_e, MOE_BLK)
    j = jnp.arange(n_pad, dtype=jnp.int32) - (ends_p - padded)[slot_e]
    src = jnp.clip((jnp.cumsum(counts) - counts)[slot_e] + j, 0, n_assign - 1)
    tok_pad = jnp.where(j < counts[slot_e], order[src] // TOP_K, 0).astype(jnp.int32)
    return gates, dest.reshape(n_tok, TOP_K).astype(jnp.int32), tok_pad, block_e, n_used


def _final_kernel(x1_ref, yk_ref, gate_ref, mod_ref, g_ref, b_ref, o_ref, *, dn_alpha):
    D = x1_ref.shape[2]
    m = mod_ref[0, 0]
    gates = gate_ref[0]
    f = yk_ref[0, :, 0:D].astype(F32) * gates[:, 0:1]
    for k in range(1, TOP_K):
        f = f + yk_ref[0, :, k * D:(k + 1) * D].astype(F32) * gates[:, k:k + 1]
    o_ref[0] = _ln(dn_alpha * x1_ref[0] + m[5:6] * f) * g_ref[...] + b_ref[...]


def _final(x1, yk, gates, mod6, ln_g, ln_b, t0, nct, dn_alpha):
    B, So, D = x1.shape
    kind = lambda t: (t + t0 >= nct).astype(jnp.int32)
    blk = pl.BlockSpec((1, TM, D), lambda b, t: (b, t, 0))
    vec = pl.BlockSpec((1, D), lambda b, t: (0, 0))
    return pl.pallas_call(
        functools.partial(_final_kernel, dn_alpha=dn_alpha),
        grid=(B, So // TM),
        in_specs=[blk,
                  pl.BlockSpec((1, TM, TOP_K * D), lambda b, t: (b, t, 0)),
                  pl.BlockSpec((1, TM, TOP_K), lambda b, t: (b, t, 0)),
                  pl.BlockSpec((1, 1, 6, D), lambda b, t: (b, kind(t), 0, 0)), vec, vec],
        out_specs=blk,
        out_shape=jax.ShapeDtypeStruct((B, So, D), F32),
        compiler_params=_cparams(("arbitrary", "arbitrary")),
        name="ffn_residual",
    )(x1, yk, gates, mod6, ln_g, ln_b)


def _rope_tables(n_ctx, T):
    t = jnp.arange(T)
    row = (t // GRID_W).astype(F32)
    col = (t % GRID_W).astype(F32)
    n_freq = MLA_ROPE // 4
    inv = ROPE_BASE ** (-jnp.arange(n_freq, dtype=F32) / n_freq)
    ang = jnp.concatenate([row[:, None] * inv, col[:, None] * inv], axis=-1)
    cos, sin = jnp.cos(ang), jnp.sin(ang)
    z = jnp.zeros((T, 128 - MLA_ROPE), F32)
    cos128 = jnp.concatenate([cos, cos, z], axis=-1)
    sin128 = jnp.concatenate([-sin, sin, z], axis=-1)
    cos_c = jnp.concatenate([jnp.ones((n_ctx, MLA_ROPE), F32), jnp.zeros((n_ctx, 128 - MLA_ROPE), F32)], -1)
    sin_c = jnp.zeros((n_ctx, 128), F32)
    return jnp.concatenate([cos_c, cos128], 0), jnp.concatenate([sin_c, sin128], 0)


def _rope_slabs(w):
    ev, od = w[:, 0::2], w[:, 1::2]
    z = jnp.zeros((w.shape[0], 128 - MLA_ROPE), w.dtype)
    return jnp.concatenate([ev, od, z, od, ev, z], axis=-1)


def _blockdiag2(m):
    z = jnp.zeros_like(m[0])
    return jnp.concatenate([jnp.concatenate([m[0], z], 1), jnp.concatenate([z, m[1]], 1)], 0)


def kernel(x, c, ctx, c_ctx, ada_w, ada_b, w_in, rw_mu, rw_w0, rw_w2, rw_a0, rw_a2, rw_g2, rw_kk, rw_ka, rw_rk, rw_gn_w, rw_gn_b, mla_q_norm, mla_kv_norm, mla_w_uq, mla_w_ukv, na_rpb, w_out, ln1_g, ln1_b, router_w, router_b, w_gu, b_gu, w_dn, b_dn, ln2_g, ln2_b):
    B, T, D = x.shape
    n_ctx = ctx.shape[1]
    depth = ada_w.shape[0]
    S = n_ctx + T
    assert n_ctx % TM == 0 and T % TM == 0 and T % GRID_W == 0 and T // GRID_W >= NA_WIN_R
    nct = n_ctx // TM
    BH = B * RW_HEADS
    dn_alpha = (2 * depth) ** 0.25
    F = w_dn.shape[2]

    R = (B + 1 + 7) // 8 * 8
    cond = jnp.zeros((R, D), F32).at[:B].set(c).at[B].set(c_ctx)
    mod = _ada_mod(cond, ada_w, ada_b)
    mod_l = mod[:, :B].reshape(depth, B, 1, 6, D)
    mod_c = jnp.broadcast_to(mod[:, B].reshape(depth, 1, 1, 6, D), (depth, B, 1, 6, D))
    mod6 = jnp.concatenate([mod_c, mod_l], axis=2)

    cos128, sin128 = _rope_tables(n_ctx, T)
    ones_blk = jnp.kron(jnp.eye(RW_HEADS, dtype=F32), jnp.ones((RW_HEAD_DIM, RW_HEAD_DIM), F32)).astype(BF16)

    xa = jnp.concatenate([ctx, x], axis=1)
    for l in range(depth):
        need_ctx = l < depth - 1
        t0 = 0 if need_ctx else nct

        wi = w_in[l]
        c_m = RW_COLS
        w_in_ext = jnp.concatenate(
            [wi[:, :c_m + MLA_Q_LORA + MLA_KV_LORA], _rope_slabs(wi[:, c_m + MLA_Q_LORA + MLA_KV_LORA:c_m + MLA_COLS]),
             wi[:, c_m + MLA_COLS:]], axis=-1).astype(BF16)
        wuq = mla_w_uq[l].reshape(MLA_Q_LORA, MLA_HEADS, MLA_NOPE + MLA_ROPE)
        wuq_ext = jnp.concatenate(
            [jnp.concatenate([wuq[:, h, :MLA_NOPE], _rope_slabs(wuq[:, h, MLA_NOPE:])], -1) for h in range(MLA_HEADS)],
            axis=-1).astype(BF16)
        rw_prm = dict(
            mu=rw_mu[l],
            w0=rw_w0[l].reshape(1, 2 * RW_DIM), w2=_blockdiag2(rw_w2[l]).astype(BF16),
            a0=rw_a0[l].reshape(1, 2 * RW_DIM), a2=_blockdiag2(rw_a2[l]).astype(BF16),
            g2=rw_g2[l].astype(BF16), kk=rw_kk[l].reshape(1, RW_DIM), rk=rw_rk[l].reshape(1, RW_DIM),
            ones=ones_blk)
        mla_prm = dict(q_norm=mla_q_norm[l].reshape(1, -1), kv_norm=mla_kv_norm[l].reshape(1, -1),
                       w_uq=wuq_ext, w_ukv=mla_w_ukv[l].astype(BF16))
        rt = jnp.zeros((D, 128), F32).at[:, :N_EXPERTS].set(router_w[l])
        rt_hi = rt.astype(BF16)
        out_prm = dict(gn_w=rw_gn_w[l].reshape(1, -1), gn_b=rw_gn_b[l].reshape(1, -1), ones=ones_blk,
                       w_out=w_out[l].astype(BF16), ln1_g=ln1_g[l].reshape(1, -1), ln1_b=ln1_b[l].reshape(1, -1),
                       router_hi=rt_hi, router_lo=(rt - rt_hi.astype(F32)).astype(BF16),
                       router_b=jnp.zeros((1, 128), F32).at[0, :N_EXPERTS].set(router_b[l]))

        p_rw, p_mla, p_na = _in_proj(xa, mod6[l], w_in_ext, nct)
        feat, g_gate, bonus = _rw_features(p_rw, rw_prm, nct)
        ft = jnp.swapaxes(feat.reshape(S, BH, 8 * RW_HEAD_DIM), 1, 2).reshape(S, 8, RW_HEAD_DIM, BH)
        ka_t = jnp.tile(rw_ka[l].reshape(RW_HEADS, RW_HEAD_DIM).T[:, None, :], (1, B, 1)).reshape(RW_HEAD_DIM, BH)
        o_scan = _rw_scan(ft, ka_t, n_ctx)
        o_rw = jnp.swapaxes(o_scan, 2, 3).reshape(2, S, B * RW_DIM)

        q, k, v = _mla_prep(p_mla, cos128, sin128, mla_prm)
        mla_o = _mla_attn(q, k, v, t0, nct, n_ctx)
        na_o = _na_attn(p_na, _na_bias_table(na_rpb[l]), n_ctx, need_ctx)

        x1, h, logits = _out_proj(o_rw, bonus, g_gate, mla_o, na_o, xa, mod6[l], out_prm, t0, nct, dn_alpha)

        So = x1.shape[1]
        n_tok = B * So
        gates, dest, tok_pad, block_e, n_used = _route(logits.reshape(n_tok, 128)[:, :N_EXPERTS], n_tok)
        x_sorted = jnp.take(h.reshape(n_tok, D), tok_pad, axis=0)
        y_sorted = _moe_ffn(x_sorted, block_e, n_used, w_gu[l], b_gu[l], w_dn[l], b_dn[l])
        yk = jnp.take(y_sorted, dest.reshape(-1), axis=0).reshape(B, So, TOP_K * D)

        xa = _final(x1, yk, gates.reshape(B, So, TOP_K), mod6[l], ln2_g[l].reshape(1, -1), ln2_b[l].reshape(1, -1),
                    t0, nct, dn_alpha)
    return xa
```

```python
import functools

import numpy as np
import jax
import jax.numpy as jnp
from jax import lax
from jax.experimental import pallas as pl
from jax.experimental.pallas import tpu as pltpu

F32 = jnp.float32
BF16 = jnp.bfloat16

GRID_W = 64
RW_HEAD_DIM = 64
RW_HEADS = 4
RW_DIM = RW_HEADS * RW_HEAD_DIM
RW_LORA = 64
RW_G_LORA = 128
RW_GN_EPS = 64e-5
RW_COLS = 3 * RW_DIM + 4 * RW_LORA + RW_G_LORA
MLA_HEADS = 4
MLA_NOPE = 128
MLA_ROPE = 64
MLA_V = 128
MLA_Q_LORA = 256
MLA_KV_LORA = 128
MLA_COLS = MLA_Q_LORA + MLA_KV_LORA + MLA_ROPE
MLA_COLS_EXT = MLA_Q_LORA + MLA_KV_LORA + 256
MLA_DIM = MLA_HEADS * MLA_V
NA_HEADS = 4
NA_HEAD_DIM = 64
NA_DIM = NA_HEADS * NA_HEAD_DIM
NA_WIN_R = 8
NA_WIN_C = 16
NA_COLS = 3 * NA_DIM
ROPE_BASE = 10000.0
N_EXPERTS = 32
TOP_K = 4
SWIGLU_ALPHA = 1.702
SWIGLU_LIMIT = 7.0
NEG_INF = -1e30

TM = 256
SCAN_TB = 16
MOE_BLK = 512
VMEM_LIMIT = 56 * 1024 * 1024


def _cparams(sem):
    return pltpu.CompilerParams(dimension_semantics=sem, vmem_limit_bytes=VMEM_LIMIT)


def _ln(x, eps=1e-5):
    mu = jnp.mean(x, axis=-1, keepdims=True)
    d = x - mu
    var = jnp.mean(d * d, axis=-1, keepdims=True)
    return d * lax.rsqrt(var + eps)


def _dot(a, b):
    return jnp.dot(a, b, preferred_element_type=F32)


def _split(a):
    hi = a.astype(BF16)
    lo = (a - hi.astype(F32)).astype(BF16)
    return hi, lo


def _dot_hl(a, b_bf16):
    hi, lo = _split(a)
    return _dot(hi, b_bf16) + _dot(lo, b_bf16)


def _dot3(a, b_hi, b_lo):
    hi, lo = _split(a)
    return _dot(hi, b_hi) + _dot(lo, b_hi) + _dot(hi, b_lo)


def _sigmoid(x):
    return 1.0 / (1.0 + jnp.exp(-x))


def _ada_kernel(cond_ref, w_ref, b_ref, o_ref):
    c = cond_ref[...]
    s = c * _sigmoid(c)
    w = w_ref[0]
    w_hi, w_lo = _split(w)
    o_ref[0] = _dot3(s, w_hi, w_lo) + b_ref[0]


def _ada_mod(cond, ada_w, ada_b):
    L, D, N = ada_w.shape
    R = cond.shape[0]
    tn = 512
    return pl.pallas_call(
        _ada_kernel,
        grid=(L, N // tn),
        in_specs=[
            pl.BlockSpec((R, D), lambda l, j: (0, 0)),
            pl.BlockSpec((1, D, tn), lambda l, j: (l, 0, j)),
            pl.BlockSpec((1, 1, tn), lambda l, j: (l, 0, j)),
        ],
        out_specs=pl.BlockSpec((1, R, tn), lambda l, j: (l, 0, j)),
        out_shape=jax.ShapeDtypeStruct((L, R, N), F32),
        compiler_params=_cparams(("arbitrary", "arbitrary")),
        name="ada_mod",
    )(cond, ada_w, ada_b.reshape(L, 1, N))


def _win_kernel(x_ref, mod_ref, w_ref, prw_ref, pmla_ref, pna_ref):
    x = x_ref[0]
    m = mod_ref[0, 0]
    xm = _ln(x) * (1.0 + m[1:2]) + m[0:1]
    p = _dot(xm.astype(BF16), w_ref[...])
    prw_ref[0] = p[:, :RW_COLS]
    pmla_ref[0] = p[:, RW_COLS:RW_COLS + MLA_COLS_EXT]
    pna_ref[0] = p[:, RW_COLS + MLA_COLS_EXT:].astype(BF16)


def _in_proj(xa, mod6, w_in_ext, nct):
    B, S, D = xa.shape
    NC = w_in_ext.shape[1]
    kind = lambda t: (t >= nct).astype(jnp.int32)
    return pl.pallas_call(
        _win_kernel,
        grid=(B, S // TM),
        in_specs=[
            pl.BlockSpec((1, TM, D), lambda b, t: (b, t, 0)),
            pl.BlockSpec((1, 1, 6, D), lambda b, t: (b, kind(t), 0, 0)),
            pl.BlockSpec((D, NC), lambda b, t: (0, 0)),
        ],
        out_specs=[
            pl.BlockSpec((1, TM, RW_COLS), lambda b, t: (b, t, 0)),
            pl.BlockSpec((1, TM, MLA_COLS_EXT), lambda b, t: (b, t, 0)),
            pl.BlockSpec((1, TM, NA_COLS), lambda b, t: (b, t, 0)),
        ],
        out_shape=[
            jax.ShapeDtypeStruct((B, S, RW_COLS), F32),
            jax.ShapeDtypeStruct((B, S, MLA_COLS_EXT), F32),
            jax.ShapeDtypeStruct((B, S, NA_COLS), BF16),
        ],
        compiler_params=_cparams(("arbitrary", "arbitrary")),
        name="in_proj",
    )(xa, mod6, w_in_ext)


def _group_sum(x, ones_ref):
    return _dot_hl(x, ones_ref[...])


def _rwfeat_kernel(p_ref, pp_ref, pn_ref, mu_ref, w0_ref, w2_ref, a0_ref, a2_ref, g2_ref,
                   kk_ref, rk_ref, ones_ref, f_ref, g_ref, bonus_ref, *, nct, nt):
    t = pl.program_id(1)
    p = p_ref[0]
    first = jnp.logical_or(t == 0, t == nct)
    last = jnp.logical_or(t == nct - 1, t == nt - 1)
    prev_row = jnp.where(first, 0.0, pp_ref[0, 7:8, :])
    next_row = jnp.where(last, 0.0, pn_ref[0, 0:1, :])
    rows = lax.broadcasted_iota(jnp.int32, p.shape, 0)
    prev = jnp.where(rows == 0, prev_row, pltpu.roll(p, 1, axis=0))
    nxt = jnp.where(rows == TM - 1, next_row, pltpu.roll(p, TM - 1, axis=0))
    mu = mu_ref[...]
    xs = p + mu[0:1] * (prev - p) + mu[1:2] * (nxt - p)

    D3 = 3 * RW_DIM
    r = xs[:, 0:RW_DIM]
    k = xs[:, RW_DIM:2 * RW_DIM]
    v = xs[:, 2 * RW_DIM:D3]
    w_lo = xs[:, D3:D3 + 2 * RW_LORA]
    a_lo = xs[:, D3 + 2 * RW_LORA:D3 + 4 * RW_LORA]
    g_pre = xs[:, D3 + 4 * RW_LORA:]

    lw = _dot(jnp.tanh(w_lo).astype(BF16), w2_ref[...]) + w0_ref[...]
    logw = jnp.minimum(lw, 0.0) - jnp.log(1.0 + jnp.exp(-jnp.abs(lw))) - 0.5
    decay = jnp.exp(-jnp.exp(logw))
    a = _sigmoid(_dot(a_lo.astype(BF16), a2_ref[...]) + a0_ref[...])

    kkr = k * kk_ref[...]
    kk = kkr * lax.rsqrt(_group_sum(kkr * kkr, ones_ref) + 1e-12)
    g_ref[0] = _dot(_sigmoid(g_pre).astype(BF16), g2_ref[...])
    bonus_ref[0] = _group_sum(r * k * rk_ref[...], ones_ref) * v

    comps = (r, k, v, kk, decay[:, :RW_DIM], a[:, :RW_DIM], decay[:, RW_DIM:], a[:, RW_DIM:])
    N = RW_HEAD_DIM
    for h in range(RW_HEADS):
        for ci, comp in enumerate(comps):
            col = (h * len(comps) + ci) * N
            f_ref[:, col:col + N] = comp[:, h * N:(h + 1) * N]


def _rw_features(p_rw, prm, nct):
    B, S, C = p_rw.shape
    nt = S // TM
    hb = TM // 8
    last_hb = S // 8 - 1
    full = lambda shape: pl.BlockSpec(shape, lambda b, t: (0,) * len(shape))
    return pl.pallas_call(
        functools.partial(_rwfeat_kernel, nct=nct, nt=nt),
        grid=(B, nt),
        in_specs=[
            pl.BlockSpec((1, TM, C), lambda b, t: (b, t, 0)),
            pl.BlockSpec((1, 8, C), lambda b, t: (b, jnp.maximum(t * hb - 1, 0), 0)),
            pl.BlockSpec((1, 8, C), lambda b, t: (b, jnp.minimum((t + 1) * hb, last_hb), 0)),
            full((2, C)),
            full((1, 2 * RW_DIM)), full((2 * RW_LORA, 2 * RW_DIM)),
            full((1, 2 * RW_DIM)), full((2 * RW_LORA, 2 * RW_DIM)),
            full((RW_G_LORA, RW_DIM)),
            full((1, RW_DIM)), full((1, RW_DIM)),
            full((RW_DIM, RW_DIM)),
        ],
        out_specs=[
            pl.BlockSpec((TM, 8 * RW_DIM), lambda b, t: (t, b)),
            pl.BlockSpec((1, TM, RW_DIM), lambda b, t: (b, t, 0)),
            pl.BlockSpec((1, TM, RW_DIM), lambda b, t: (b, t, 0)),
        ],
        out_shape=[
            jax.ShapeDtypeStruct((S, B * 8 * RW_DIM), F32),
            jax.ShapeDtypeStruct((B, S, RW_DIM), F32),
            jax.ShapeDtypeStruct((B, S, RW_DIM), F32),
        ],
        compiler_params=_cparams(("arbitrary", "arbitrary")),
        name="rw_features",
    )(p_rw, p_rw, p_rw, prm["mu"], prm["w0"], prm["w2"], prm["a0"], prm["a2"], prm["g2"],
      prm["kk"], prm["rk"], prm["ones"])


def _scan_kernel(fs_ref, fd_ref, ka_ref, o_ref, s_ref, tmp_ref, *, tb):
    d = pl.program_id(0)
    g = pl.program_id(1)
    N = RW_HEAD_DIM

    @pl.when(g == 0)
    def _():
        s_ref[...] = jnp.zeros_like(s_ref)

    ka = ka_ref[...]

    def step(i, carry):
        tt = jnp.where(d == 0, i, tb - 1 - i)
        r = fs_ref[tt, 0]
        k = fs_ref[tt, 1]
        v = fs_ref[tt, 2]
        kk = fs_ref[tt, 3]
        w = fd_ref[tt, 0]
        a = fd_ref[tt, 1]
        b = a * kk
        kd = k * (1.0 + (a - 1.0) * ka)
        wr = w * r
        br = jnp.sum(b * r, axis=0, keepdims=True)
        kr = jnp.sum(kd * r, axis=0, keepdims=True)
        tmp_ref[0] = wr
        tmp_ref[1] = b
        tmp_ref[2] = kd
        sa = [jnp.zeros_like(v), jnp.zeros_like(v)]
        op = [jnp.zeros_like(v), jnp.zeros_like(v)]
        for j in range(N):
            sk = s_ref[j]
            sa[j % 2] = sa[j % 2] + sk * fs_ref[tt, 3, pl.ds(j, 1), :]
            op[j % 2] = op[j % 2] + sk * tmp_ref[0, pl.ds(j, 1), :]
        sa = sa[0] + sa[1]
        op = op[0] + op[1]
        for j in range(N):
            s_ref[j] = (s_ref[j] * fd_ref[tt, 0, pl.ds(j, 1), :]
                        - sa * tmp_ref[1, pl.ds(j, 1), :]
                        + v * tmp_ref[2, pl.ds(j, 1), :])
        o_ref[0, tt] = op - sa * br + v * kr
        return carry

    lax.fori_loop(0, tb, step, 0)


def _rw_scan(ft, ka_t, n_ctx):
    S, _, N, BH = ft.shape
    tb = SCAN_TB
    nb = S // tb
    ncb = n_ctx // tb

    def tblk(d, g):
        bwd = jnp.where(g < ncb, ncb - 1 - g, nb - 1 - g + ncb)
        return jnp.where(d == 0, g, bwd)

    return pl.pallas_call(
        functools.partial(_scan_kernel, tb=tb),
        grid=(2, nb),
        in_specs=[
            pl.BlockSpec((tb, 4, N, BH), lambda d, g: (tblk(d, g), 0, 0, 0)),
            pl.BlockSpec((tb, 2, N, BH), lambda d, g: (tblk(d, g), 2 + d, 0, 0)),
            pl.BlockSpec((N, BH), lambda d, g: (0, 0)),
        ],
        out_specs=pl.BlockSpec((1, tb, N, BH), lambda d, g: (d, tblk(d, g), 0, 0)),
        out_shape=jax.ShapeDtypeStruct((2, S, N, BH), F32),
        scratch_shapes=[pltpu.VMEM((N, N, BH), F32), pltpu.VMEM((3, N, BH), F32)],
        compiler_params=_cparams(("arbitrary", "arbitrary")),
        name="rw_scan",
    )(ft, ft, ka_t)


def _mlaprep_kernel(p_ref, cos_ref, sin_ref, qn_ref, kvn_ref, wuq_ref, wukv_ref, q_ref, k_ref, v_ref):
    p = p_ref[0]
    cos = cos_ref[...]
    sin = sin_ref[...]
    scale = (MLA_NOPE + MLA_ROPE) ** -0.5

    def rms(x, g):
        return x * lax.rsqrt(jnp.mean(x * x, axis=-1, keepdims=True) + 1e-6) * g

    q = _dot(rms(p[:, :MLA_Q_LORA], qn_ref[...]).astype(BF16), wuq_ref[...])
    kv = _dot(rms(p[:, MLA_Q_LORA:MLA_Q_LORA + MLA_KV_LORA], kvn_ref[...]).astype(BF16), wukv_ref[...])
    c0 = MLA_Q_LORA + MLA_KV_LORA
    kr = (p[:, c0:c0 + 128] * cos + p[:, c0 + 128:c0 + 256] * sin).astype(BF16)
    for h in range(MLA_HEADS):
        qb = h * 384
        q_ref[0, :, h * 256:h * 256 + 128] = (q[:, qb:qb + 128] * scale).astype(BF16)
        q_ref[0, :, h * 256 + 128:h * 256 + 256] = (
            (q[:, qb + 128:qb + 256] * cos + q[:, qb + 256:qb + 384] * sin) * scale).astype(BF16)
        k_ref[0, :, h * 256:h * 256 + 128] = kv[:, h * 256:h * 256 + 128].astype(BF16)
        k_ref[0, :, h * 256 + 128:h * 256 + 256] = kr
        v_ref[0, :, h * 128:(h + 1) * 128] = kv[:, h * 256 + 128:h * 256 + 256].astype(BF16)


def _mla_prep(p_mla, cos128, sin128, prm):
    B, S, C = p_mla.shape
    full = lambda shape: pl.BlockSpec(shape, lambda b, t: (0,) * len(shape))
    H = MLA_HEADS
    return pl.pallas_call(
        _mlaprep_kernel,
        grid=(B, S // TM),
        in_specs=[
            pl.BlockSpec((1, TM, C), lambda b, t: (b, t, 0)),
            pl.BlockSpec((TM, 128), lambda b, t: (t, 0)),
            pl.BlockSpec((TM, 128), lambda b, t: (t, 0)),
            full((1, MLA_Q_LORA)), full((1, MLA_KV_LORA)),
            full((MLA_Q_LORA, H * 384)), full((MLA_KV_LORA, H * 256)),
        ],
        out_specs=[
            pl.BlockSpec((1, TM, H * 256), lambda b, t: (b, t, 0)),
            pl.BlockSpec((1, TM, H * 256), lambda b, t: (b, t, 0)),
            pl.BlockSpec((1, TM, H * 128), lambda b, t: (b, t, 0)),
        ],
        out_shape=[
            jax.ShapeDtypeStruct((B, S, H * 256), BF16),
            jax.ShapeDtypeStruct((B, S, H * 256), BF16),
            jax.ShapeDtypeStruct((B, S, H * 128), BF16),
        ],
        compiler_params=_cparams(("arbitrary", "arbitrary")),
        name="mla_prep",
    )(p_mla, cos128, sin128, prm["q_norm"], prm["kv_norm"], prm["w_uq"], prm["w_ukv"])


def _mla_attn_kernel(q_ref, k_ref, v_ref, o_ref, *, t0, nct, n_ctx):
    t = pl.program_id(1) + t0

    def attend(n_keys):
        for h in range(MLA_HEADS):
            q = q_ref[0, :, h * 256:(h + 1) * 256]
            k = k_ref[0, 0:n_keys, h * 256:(h + 1) * 256]
            s = lax.dot_general(q, k, (((1,), (1,)), ((), ())), preferred_element_type=F32)
            m = jnp.max(s, axis=-1, keepdims=True)
            e = jnp.exp(s - m)
            l = jnp.sum(e, axis=-1, keepdims=True)
            o = _dot(e.astype(BF16), v_ref[0, 0:n_keys, h * 128:(h + 1) * 128])
            o_ref[0, :, h * 128:(h + 1) * 128] = (o / l).astype(BF16)

    S = k_ref.shape[1]
    if t0 < nct:
        @pl.when(t < nct)
        def _():
            attend(n_ctx)

        @pl.when(t >= nct)
        def _():
            attend(S)
    else:
        attend(S)


def _mla_attn(q, k, v, t0, nct, n_ctx):
    B, S, _ = q.shape
    nq = S // TM - t0
    H = MLA_HEADS
    return pl.pallas_call(
        functools.partial(_mla_attn_kernel, t0=t0, nct=nct, n_ctx=n_ctx),
        grid=(B, nq),
        in_specs=[
            pl.BlockSpec((1, TM, H * 256), lambda b, t: (b, t + t0, 0)),
            pl.BlockSpec((1, S, H * 256), lambda b, t: (b, 0, 0)),
            pl.BlockSpec((1, S, H * 128), lambda b, t: (b, 0, 0)),
        ],
        out_specs=pl.BlockSpec((1, TM, H * 128), lambda b, t: (b, t, 0)),
        out_shape=jax.ShapeDtypeStruct((B, nq * TM, H * 128), BF16),
        compiler_params=_cparams(("arbitrary", "arbitrary")),
        name="mla_attn",
    )(q, k, v)


NA_QR = 4
NA_KR = NA_WIN_R + NA_QR
NA_QB = NA_QR * GRID_W


def _na_kernel(p_ref, bias_ref, o_ref, *, n_ctx, rows, with_ctx):
    s_id = pl.program_id(1)
    W = GRID_W
    n_loc = NA_KR * W
    scale = NA_HEAD_DIM ** -0.5
    nq_ctx = n_ctx // NA_QB
    nblk = rows // NA_QR

    def heads_out(q, parts):
        outs = []
        for h in range(NA_HEADS):
            hs = slice(h * NA_HEAD_DIM, (h + 1) * NA_HEAD_DIM)
            qh = q[:, hs] * scale
            ss = []
            for kx, vx, bias in parts:
                s = lax.dot_general(qh, kx[:, hs], (((1,), (1,)), ((), ())), preferred_element_type=F32)
                if bias is not None:
                    s = s + bias(h)
                ss.append(s)
            m = ss[0].max(axis=-1, keepdims=True)
            for s in ss[1:]:
                m = jnp.maximum(m, s.max(axis=-1, keepdims=True))
            acc = 0.0
            l = 0.0
            for s, (kx, vx, bias) in zip(ss, parts):
                e = jnp.exp(s - m)
                l = l + jnp.sum(e, axis=-1, keepdims=True)
                acc = acc + _dot(e.astype(BF16), vx[:, hs])
            outs.append(acc / l)
        return jnp.concatenate(outs, axis=-1).astype(BF16)

    k_c = p_ref[0, 0:n_ctx, NA_DIM:2 * NA_DIM]
    v_c = p_ref[0, 0:n_ctx, 2 * NA_DIM:3 * NA_DIM]

    def lat_block(j):
        i0 = j * NA_QR
        k_start = jnp.clip(i0 - NA_WIN_R // 2, 0, rows - NA_KR)
        pat = jnp.where(j == 0, 0, jnp.where(j == nblk - 1, 2, 1))
        q0 = pl.multiple_of(n_ctx + i0 * W, NA_QB)
        k0 = pl.multiple_of(n_ctx + k_start * W, W)
        q = p_ref[0, pl.ds(q0, NA_QB), 0:NA_DIM]
        k_l = p_ref[0, pl.ds(k0, n_loc), NA_DIM:2 * NA_DIM]
        v_l = p_ref[0, pl.ds(k0, n_loc), 2 * NA_DIM:3 * NA_DIM]
        o_ref[0] = heads_out(q, [(k_l, v_l, lambda h: bias_ref[pat, h]), (k_c, v_c, None)])

    if with_ctx:
        @pl.when(s_id < nq_ctx)
        def _():
            q0 = pl.multiple_of(s_id * NA_QB, NA_QB)
            q = p_ref[0, pl.ds(q0, NA_QB), 0:NA_DIM]
            o_ref[0] = heads_out(q, [(k_c, v_c, None)])

        @pl.when(s_id >= nq_ctx)
        def _():
            lat_block(s_id - nq_ctx)
    else:
        lat_block(s_id)


def _na_attn(p_na, bias_tab, n_ctx, with_ctx):
    B, S, C = p_na.shape
    T = S - n_ctx
    rows = T // GRID_W
    assert rows % NA_QR == 0 and rows >= NA_KR and n_ctx % NA_QB == 0
    nsteps = rows // NA_QR + (n_ctx // NA_QB if with_ctx else 0)
    return pl.pallas_call(
        functools.partial(_na_kernel, n_ctx=n_ctx, rows=rows, with_ctx=with_ctx),
        grid=(B, nsteps),
        in_specs=[
            pl.BlockSpec((1, S, C), lambda b, s: (b, 0, 0)),
            pl.BlockSpec(bias_tab.shape, lambda b, s: (0, 0, 0, 0)),
        ],
        out_specs=pl.BlockSpec((1, NA_QB, NA_DIM), lambda b, s: (b, s, 0)),
        out_shape=jax.ShapeDtypeStruct((B, nsteps * NA_QB, NA_DIM), BF16),
        compiler_params=_cparams(("arbitrary", "arbitrary")),
        name="na_attn",
    )(p_na, bias_tab)


def _na_bias_table(rpb):
    col = jnp.arange(GRID_W)
    c_start = jnp.clip(col - NA_WIN_C // 2, 0, GRID_W - NA_WIN_C)
    in_win = (col[None, :] >= c_start[:, None]) & (col[None, :] < c_start[:, None] + NA_WIN_C)
    dc_idx = jnp.clip(col[None, :] - col[:, None] + NA_WIN_C - 1, 0, 2 * NA_WIN_C - 2)
    qa = jnp.arange(NA_QR)[:, None]
    kc = jnp.arange(NA_KR)[None, :]
    pats = []
    for pat in range(3):
        off = (NA_WIN_R // 2) * pat
        first = (0 * qa, qa, 0 * qa + NA_WIN_R // 2)[pat]
        row_ok = (kc >= first) & (kc < first + NA_WIN_R)
        dr_idx = jnp.clip(kc - qa - off + NA_WIN_R - 1, 0, 2 * NA_WIN_R - 2)
        bias = rpb[:, dr_idx][:, :, :, dc_idx]
        ok = row_ok[:, :, None, None] & in_win[None, None, :, :]
        bias = jnp.where(ok[None], bias, NEG_INF)
        bias = jnp.transpose(bias, (0, 1, 3, 2, 4))
        pats.append(bias.reshape(NA_HEADS, NA_QB, NA_KR * GRID_W))
    return jnp.stack(pats).astype(F32)


def _outproj_kernel(orw_ref, bonus_ref, g_ref, mla_ref, na_ref, x_ref, mod_ref, gnw_ref, gnb_ref, ones_ref,
                    wout_ref, ln1g_ref, ln1b_ref, rwh_ref, rwl_ref, rb_ref,
                    x1_ref, h_ref, lg_ref, *, dn_alpha):
    o = orw_ref[0] + orw_ref[1]
    inv_n = 1.0 / RW_HEAD_DIM
    mu = _group_sum(o, ones_ref) * inv_n
    dlt = o - mu
    var = _group_sum(dlt * dlt, ones_ref) * inv_n
    on = dlt * lax.rsqrt(var + RW_GN_EPS) * gnw_ref[...] + gnb_ref[...]
    rw_y = ((on + bonus_ref[0]) * g_ref[0]).astype(BF16)
    y = (_dot(rw_y, wout_ref[0:RW_DIM, :])
         + _dot(mla_ref[0], wout_ref[RW_DIM:RW_DIM + MLA_DIM, :])
         + _dot(na_ref[0], wout_ref[RW_DIM + MLA_DIM:, :]))
    m = mod_ref[0, 0]
    x1 = _ln(dn_alpha * x_ref[0] + m[2:3] * y) * ln1g_ref[...] + ln1b_ref[...]
    x1_ref[0] = x1
    h = _ln(x1) * (1.0 + m[4:5]) + m[3:4]
    h_ref[0] = h.astype(BF16)
    lg_ref[0] = _dot3(h, rwh_ref[...], rwl_ref[...]) + rb_ref[...]


def _out_proj(o_rw, bonus, g, mla_o, na_o, xa, mod6, prm, t0, nct, dn_alpha):
    B, S, D = xa.shape
    nt = S // TM - t0
    So = nt * TM
    kind = lambda t: (t + t0 >= nct).astype(jnp.int32)
    full = lambda shape: pl.BlockSpec(shape, lambda b, t: (0,) * len(shape))
    off = lambda C: pl.BlockSpec((1, TM, C), lambda b, t: (b, t + t0, 0))
    own = lambda C: pl.BlockSpec((1, TM, C), lambda b, t: (b, t, 0))
    return pl.pallas_call(
        functools.partial(_outproj_kernel, dn_alpha=dn_alpha),
        grid=(B, nt),
        in_specs=[
            pl.BlockSpec((2, TM, RW_DIM), lambda b, t: (0, t + t0, b)),
            off(RW_DIM), off(RW_DIM), own(MLA_DIM), own(NA_DIM), off(D),
            pl.BlockSpec((1, 1, 6, D), lambda b, t: (b, kind(t), 0, 0)),
            full((1, RW_DIM)), full((1, RW_DIM)), full((RW_DIM, RW_DIM)),
            full((D, D)), full((1, D)), full((1, D)),
            full((D, 128)), full((D, 128)), full((1, 128)),
        ],
        out_specs=[own(D), own(D), own(128)],
        out_shape=[
            jax.ShapeDtypeStruct((B, So, D), F32),
            jax.ShapeDtypeStruct((B, So, D), BF16),
            jax.ShapeDtypeStruct((B, So, 128), F32),
        ],
        compiler_params=_cparams(("arbitrary", "arbitrary")),
        name="out_proj",
    )(o_rw, bonus, g, mla_o, na_o, xa, mod6, prm["gn_w"], prm["gn_b"], prm["ones"],
      prm["w_out"], prm["ln1_g"], prm["ln1_b"], prm["router_hi"], prm["router_lo"], prm["router_b"])


GU_CHUNK = 256


def _moe_kernel(be_ref, nu_ref, x_ref, wgu_ref, bgu_ref, wdn_ref, bdn_ref, perm_ref, y_ref,
                wgu_s, wdn_s, act_s):
    i = pl.program_id(0)
    F = wdn_ref.shape[2]
    n_chunks = 2 * F // GU_CHUNK
    half = GU_CHUNK // 2
    valid = i < nu_ref[0]
    e = be_ref[i]
    new_expert = jnp.logical_or(i == 0, e != be_ref[jnp.maximum(i - 1, 0)])

    @pl.when(jnp.logical_and(valid, new_expert))
    def _():
        for c in range(n_chunks):
            cs = slice(c * GU_CHUNK, (c + 1) * GU_CHUNK)
            wgu_s[:, cs] = _dot(wgu_ref[0, 0, :, cs].astype(BF16), perm_ref[...]).astype(BF16)
        wdn_s[...] = wdn_ref[0, 0].astype(BF16)

    @pl.when(valid)
    def _():
        gu = _dot(x_ref[...], wgu_s[...]) + bgu_ref[0]
        for c in range(n_chunks):
            glu = jnp.minimum(gu[:, c * GU_CHUNK:c * GU_CHUNK + half], SWIGLU_LIMIT)
            lin = jnp.clip(gu[:, c * GU_CHUNK + half:(c + 1) * GU_CHUNK], -SWIGLU_LIMIT, SWIGLU_LIMIT)
            act_s[:, c * half:(c + 1) * half] = (glu * _sigmoid(SWIGLU_ALPHA * glu) * (lin + 1.0)).astype(BF16)
        y = _dot(act_s[...], wdn_s[...]) + bdn_ref[0]
        y_ref[...] = y.astype(BF16)

    @pl.when(jnp.logical_not(valid))
    def _():
        y_ref[...] = jnp.zeros_like(y_ref)


def _moe_ffn(x_sorted, block_e, n_used, w_gu, b_gu, w_dn, b_dn, layer):
    n_pad, D = x_sorted.shape
    _, E, _, F2 = w_gu.shape
    F = F2 // 2
    n_blocks = n_pad // MOE_BLK
    half = GU_CHUNK // 2
    src = np.concatenate([2 * np.arange(half), 2 * np.arange(half) + 1])
    perm = jnp.asarray(np.eye(GU_CHUNK, dtype=np.float32)[:, src], BF16)
    bgu_p = jnp.swapaxes(b_gu.reshape(E, F2 // GU_CHUNK, half, 2), 2, 3).reshape(E, 1, F2)
    grid_spec = pltpu.PrefetchScalarGridSpec(
        num_scalar_prefetch=2,
        grid=(n_blocks,),
        in_specs=[
            pl.BlockSpec((MOE_BLK, D), lambda i, be, nu: (i, 0)),
            pl.BlockSpec((1, 1, D, F2), lambda i, be, nu: (layer, be[i], 0, 0)),
            pl.BlockSpec((1, 1, F2), lambda i, be, nu: (be[i], 0, 0)),
            pl.BlockSpec((1, 1, F, D), lambda i, be, nu: (layer, be[i], 0, 0)),
            pl.BlockSpec((1, 1, D), lambda i, be, nu: (be[i], 0, 0)),
            pl.BlockSpec((GU_CHUNK, GU_CHUNK), lambda i, be, nu: (0, 0)),
        ],
        out_specs=pl.BlockSpec((MOE_BLK, D), lambda i, be, nu: (i, 0)),
        scratch_shapes=[pltpu.VMEM((D, F2), BF16), pltpu.VMEM((F, D), BF16), pltpu.VMEM((MOE_BLK, F), BF16)],
    )
    return pl.pallas_call(
        _moe_kernel,
        grid_spec=grid_spec,
        out_shape=jax.ShapeDtypeStruct((n_pad, D), BF16),
        compiler_params=_cparams(("arbitrary",)),
        name="moe_ffn",
    )(block_e, n_used, x_sorted, w_gu, bgu_p, w_dn, b_dn.reshape(E, 1, D), perm)


def _route(logits, n_tok):
    top_v, top_i = lax.top_k(logits, TOP_K)
    gates = jax.nn.softmax(top_v, axis=-1)
    n_assign = n_tok * TOP_K
    flat_e = top_i.reshape(-1)
    onehot = (flat_e[:, None] == jnp.arange(N_EXPERTS)[None, :]).astype(jnp.int32)
    rank = jnp.take_along_axis(jnp.cumsum(onehot, axis=0), flat_e[:, None], axis=1)[:, 0] - 1
    counts = jnp.sum(onehot, axis=0)
    padded = (counts + MOE_BLK - 1) // MOE_BLK * MOE_BLK
    ends_p = jnp.cumsum(padded)
    dest = (ends_p - padded)[flat_e] + rank
    n_blocks = -(-(n_assign + N_EXPERTS * (MOE_BLK - 1)) // MOE_BLK)
    n_pad = n_blocks * MOE_BLK
    blk_start = jnp.arange(n_blocks, dtype=jnp.int32) * MOE_BLK
    block_e = jnp.minimum(jnp.sum((ends_p[None, :] <= blk_start[:, None]).astype(jnp.int32), axis=1),
                          N_EXPERTS - 1).astype(jnp.int32)
    n_used = (ends_p[-1] // MOE_BLK).astype(jnp.int32).reshape(1)
    order = jnp.argsort(flat_e, stable=True).astype(jnp.int32)
    slot_e = jnp.repeat(block_e, MOE_BLK)
    j = jnp.arange(n_pad, dtype=jnp.int32) - (ends_p - padded)[slot_e]
    src = jnp.clip((jnp.cumsum(counts) - counts)[slot_e] + j, 0, n_assign - 1)
    tok_pad = jnp.where(j < counts[slot_e], order[src] // TOP_K, 0).astype(jnp.int32)
    return gates, dest.reshape(n_tok, TOP_K).astype(jnp.int32), tok_pad, block_e, n_used


def _final_kernel(x1_ref, yk_ref, gate_ref, mod_ref, g_ref, b_ref, o_ref, *, dn_alpha):
    D = x1_ref.shape[2]
    m = mod_ref[0, 0]
    gates = gate_ref[0]
    f = yk_ref[0, 0].astype(F32) * gates[:, 0:1]
    for k in range(1, TOP_K):
        f = f + yk_ref[k, 0].astype(F32) * gates[:, k:k + 1]
    o_ref[0] = _ln(dn_alpha * x1_ref[0] + m[5:6] * f) * g_ref[...] + b_ref[...]


def _final(x1, yk, gates, mod6, ln_g, ln_b, t0, nct, dn_alpha):
    B, So, D = x1.shape
    kind = lambda t: (t + t0 >= nct).astype(jnp.int32)
    blk = pl.BlockSpec((1, TM, D), lambda b, t: (b, t, 0))
    vec = pl.BlockSpec((1, D), lambda b, t: (0, 0))
    return pl.pallas_call(
        functools.partial(_final_kernel, dn_alpha=dn_alpha),
        grid=(B, So // TM),
        in_specs=[blk,
                  pl.BlockSpec((TOP_K, 1, TM, D), lambda b, t: (0, b, t, 0)),
                  pl.BlockSpec((1, TM, TOP_K), lambda b, t: (b, t, 0)),
                  pl.BlockSpec((1, 1, 6, D), lambda b, t: (b, kind(t), 0, 0)), vec, vec],
        out_specs=blk,
        out_shape=jax.ShapeDtypeStruct((B, So, D), F32),
        compiler_params=_cparams(("arbitrary", "arbitrary")),
        name="ffn_residual",
    )(x1, yk, gates, mod6, ln_g, ln_b)


def _rope_tables(n_ctx, T):
    t = jnp.arange(T)
    row = (t // GRID_W).astype(F32)
    col = (t % GRID_W).astype(F32)
    n_freq = MLA_ROPE // 4
    inv = ROPE_BASE ** (-jnp.arange(n_freq, dtype=F32) / n_freq)
    ang = jnp.concatenate([row[:, None] * inv, col[:, None] * inv], axis=-1)
    cos, sin = jnp.cos(ang), jnp.sin(ang)
    z = jnp.zeros((T, 128 - MLA_ROPE), F32)
    cos128 = jnp.concatenate([cos, cos, z], axis=-1)
    sin128 = jnp.concatenate([-sin, sin, z], axis=-1)
    cos_c = jnp.concatenate([jnp.ones((n_ctx, MLA_ROPE), F32), jnp.zeros((n_ctx, 128 - MLA_ROPE), F32)], -1)
    sin_c = jnp.zeros((n_ctx, 128), F32)
    return jnp.concatenate([cos_c, cos128], 0), jnp.concatenate([sin_c, sin128], 0)


def _rope_slabs(w):
    ev, od = w[:, 0::2], w[:, 1::2]
    z = jnp.zeros((w.shape[0], 128 - MLA_ROPE), w.dtype)
    return jnp.concatenate([ev, od, z, od, ev, z], axis=-1)


def _blockdiag2(m):
    z = jnp.zeros_like(m[0])
    return jnp.concatenate([jnp.concatenate([m[0], z], 1), jnp.concatenate([z, m[1]], 1)], 0)


def kernel(x, c, ctx, c_ctx, ada_w, ada_b, w_in, rw_mu, rw_w0, rw_w2, rw_a0, rw_a2, rw_g2, rw_kk, rw_ka, rw_rk, rw_gn_w, rw_gn_b, mla_q_norm, mla_kv_norm, mla_w_uq, mla_w_ukv, na_rpb, w_out, ln1_g, ln1_b, router_w, router_b, w_gu, b_gu, w_dn, b_dn, ln2_g, ln2_b):
    B, T, D = x.shape
    n_ctx = ctx.shape[1]
    depth = ada_w.shape[0]
    S = n_ctx + T
    assert n_ctx % TM == 0 and T % TM == 0 and T % GRID_W == 0 and T // GRID_W >= NA_WIN_R
    nct = n_ctx // TM
    BH = B * RW_HEADS
    dn_alpha = (2 * depth) ** 0.25
    F = w_dn.shape[2]

    R = (B + 1 + 7) // 8 * 8
    cond = jnp.zeros((R, D), F32).at[:B].set(c).at[B].set(c_ctx)
    mod = _ada_mod(cond, ada_w, ada_b)
    mod_l = mod[:, :B].reshape(depth, B, 1, 6, D)
    mod_c = jnp.broadcast_to(mod[:, B].reshape(depth, 1, 1, 6, D), (depth, B, 1, 6, D))
    mod6 = jnp.concatenate([mod_c, mod_l], axis=2)

    cos128, sin128 = _rope_tables(n_ctx, T)
    ones_blk = jnp.kron(jnp.eye(RW_HEADS, dtype=F32), jnp.ones((RW_HEAD_DIM, RW_HEAD_DIM), F32)).astype(BF16)

    xa = jnp.concatenate([ctx, x], axis=1)
    for l in range(depth):
        need_ctx = l < depth - 1
        t0 = 0 if need_ctx else nct

        wi = w_in[l]
        c_m = RW_COLS
        w_in_ext = jnp.concatenate(
            [wi[:, :c_m + MLA_Q_LORA + MLA_KV_LORA], _rope_slabs(wi[:, c_m + MLA_Q_LORA + MLA_KV_LORA:c_m + MLA_COLS]),
             wi[:, c_m + MLA_COLS:]], axis=-1).astype(BF16)
        wuq = mla_w_uq[l].reshape(MLA_Q_LORA, MLA_HEADS, MLA_NOPE + MLA_ROPE)
        wuq_ext = jnp.concatenate(
            [jnp.concatenate([wuq[:, h, :MLA_NOPE], _rope_slabs(wuq[:, h, MLA_NOPE:])], -1) for h in range(MLA_HEADS)],
            axis=-1).astype(BF16)
        rw_prm = dict(
            mu=rw_mu[l],
            w0=rw_w0[l].reshape(1, 2 * RW_DIM), w2=_blockdiag2(rw_w2[l]).astype(BF16),
            a0=rw_a0[l].reshape(1, 2 * RW_DIM), a2=_blockdiag2(rw_a2[l]).astype(BF16),
            g2=rw_g2[l].astype(BF16), kk=rw_kk[l].reshape(1, RW_DIM), rk=rw_rk[l].reshape(1, RW_DIM),
            ones=ones_blk)
        mla_prm = dict(q_norm=mla_q_norm[l].reshape(1, -1), kv_norm=mla_kv_norm[l].reshape(1, -1),
                       w_uq=wuq_ext, w_ukv=mla_w_ukv[l].astype(BF16))
        rt = jnp.zeros((D, 128), F32).at[:, :N_EXPERTS].set(router_w[l])
        rt_hi = rt.astype(BF16)
        out_prm = dict(gn_w=rw_gn_w[l].reshape(1, -1), gn_b=rw_gn_b[l].reshape(1, -1), ones=ones_blk,
                       w_out=w_out[l].astype(BF16), ln1_g=ln1_g[l].reshape(1, -1), ln1_b=ln1_b[l].reshape(1, -1),
                       router_hi=rt_hi, router_lo=(rt - rt_hi.astype(F32)).astype(BF16),
                       router_b=jnp.zeros((1, 128), F32).at[0, :N_EXPERTS].set(router_b[l]))

        p_rw, p_mla, p_na = _in_proj(xa, mod6[l], w_in_ext, nct)
        feat, g_gate, bonus = _rw_features(p_rw, rw_prm, nct)
        ft = jnp.swapaxes(feat.reshape(S, BH, 8 * RW_HEAD_DIM), 1, 2).reshape(S, 8, RW_HEAD_DIM, BH)
        ka_t = jnp.tile(rw_ka[l].reshape(RW_HEADS, RW_HEAD_DIM).T[:, None, :], (1, B, 1)).reshape(RW_HEAD_DIM, BH)
        o_scan = _rw_scan(ft, ka_t, n_ctx)
        o_rw = jnp.swapaxes(o_scan, 2, 3).reshape(2, S, B * RW_DIM)

        q, k, v = _mla_prep(p_mla, cos128, sin128, mla_prm)
        mla_o = _mla_attn(q, k, v, t0, nct, n_ctx)
        na_o = _na_attn(p_na, _na_bias_table(na_rpb[l]), n_ctx, need_ctx)

        x1, h, logits = _out_proj(o_rw, bonus, g_gate, mla_o, na_o, xa, mod6[l], out_prm, t0, nct, dn_alpha)

        So = x1.shape[1]
        n_tok = B * So
        gates, dest, tok_pad, block_e, n_used = _route(logits.reshape(n_tok, 128)[:, :N_EXPERTS], n_tok)
        x_sorted = h.reshape(n_tok, D).at[tok_pad].get(mode="promise_in_bounds")
        y_sorted = _moe_ffn(x_sorted, block_e, n_used, w_gu, b_gu[l], w_dn, b_dn[l], l)
        yk = y_sorted.at[dest.T.reshape(-1)].get(mode="promise_in_bounds").reshape(TOP_K, B, So, D)

        xa = _final(x1, yk, gates.reshape(B, So, TOP_K), mod6[l], ln2_g[l].reshape(1, -1), ln2_b[l].reshape(1, -1),
                    t0, nct, dn_alpha)
    return xa
```

```python
import functools

import numpy as np
import jax
import jax.numpy as jnp
from jax import lax
from jax.experimental import pallas as pl
from jax.experimental.pallas import tpu as pltpu

F32 = jnp.float32
BF16 = jnp.bfloat16

GRID_W = 64
RW_HEAD_DIM = 64
RW_HEADS = 4
RW_DIM = RW_HEADS * RW_HEAD_DIM
RW_LORA = 64
RW_G_LORA = 128
RW_GN_EPS = 64e-5
RW_COLS = 3 * RW_DIM + 4 * RW_LORA + RW_G_LORA
MLA_HEADS = 4
MLA_NOPE = 128
MLA_ROPE = 64
MLA_V = 128
MLA_Q_LORA = 256
MLA_KV_LORA = 128
MLA_COLS = MLA_Q_LORA + MLA_KV_LORA + MLA_ROPE
MLA_COLS_EXT = MLA_Q_LORA + MLA_KV_LORA + 256
MLA_DIM = MLA_HEADS * MLA_V
NA_HEADS = 4
NA_HEAD_DIM = 64
NA_DIM = NA_HEADS * NA_HEAD_DIM
NA_WIN_R = 8
NA_WIN_C = 16
NA_COLS = 3 * NA_DIM
ROPE_BASE = 10000.0
N_EXPERTS = 32
TOP_K = 4
SWIGLU_ALPHA = 1.702
SWIGLU_LIMIT = 7.0
NEG_INF = -1e30

TM = 256
SCAN_TB = 16
MOE_BLK = 512
MOE_PARTS = 4
VMEM_LIMIT = 56 * 1024 * 1024


def _cparams(sem):
    return pltpu.CompilerParams(dimension_semantics=sem, vmem_limit_bytes=VMEM_LIMIT)


def _ln(x, eps=1e-5):
    mu = jnp.mean(x, axis=-1, keepdims=True)
    d = x - mu
    var = jnp.mean(d * d, axis=-1, keepdims=True)
    return d * lax.rsqrt(var + eps)


def _dot(a, b):
    return jnp.dot(a, b, preferred_element_type=F32)


def _split(a):
    hi = a.astype(BF16)
    lo = (a - hi.astype(F32)).astype(BF16)
    return hi, lo


def _dot_hl(a, b_bf16):
    hi, lo = _split(a)
    return _dot(hi, b_bf16) + _dot(lo, b_bf16)


def _dot3(a, b_hi, b_lo):
    hi, lo = _split(a)
    return _dot(hi, b_hi) + _dot(lo, b_hi) + _dot(hi, b_lo)


def _sigmoid(x):
    return 1.0 / (1.0 + jnp.exp(-x))


def _ada_kernel(cond_ref, w_ref, b_ref, o_ref):
    c = cond_ref[...]
    s = c * _sigmoid(c)
    w = w_ref[0]
    w_hi, w_lo = _split(w)
    o_ref[0] = _dot3(s, w_hi, w_lo) + b_ref[0]


def _ada_mod(cond, ada_w, ada_b):
    L, D, N = ada_w.shape
    R = cond.shape[0]
    tn = 512
    return pl.pallas_call(
        _ada_kernel,
        grid=(L, N // tn),
        in_specs=[
            pl.BlockSpec((R, D), lambda l, j: (0, 0)),
            pl.BlockSpec((1, D, tn), lambda l, j: (l, 0, j)),
            pl.BlockSpec((1, 1, tn), lambda l, j: (l, 0, j)),
        ],
        out_specs=pl.BlockSpec((1, R, tn), lambda l, j: (l, 0, j)),
        out_shape=jax.ShapeDtypeStruct((L, R, N), F32),
        compiler_params=_cparams(("arbitrary", "arbitrary")),
        name="ada_mod",
    )(cond, ada_w, ada_b.reshape(L, 1, N))


def _win_kernel(x_ref, mod_ref, w_ref, prw_ref, pmla_ref, pna_ref):
    x = x_ref[0]
    m = mod_ref[0, 0]
    xm = _ln(x) * (1.0 + m[1:2]) + m[0:1]
    p = _dot(xm.astype(BF16), w_ref[...])
    prw_ref[0] = p[:, :RW_COLS]
    pmla_ref[0] = p[:, RW_COLS:RW_COLS + MLA_COLS_EXT]
    pna_ref[0] = p[:, RW_COLS + MLA_COLS_EXT:].astype(BF16)


def _in_proj(xa, mod6, w_in_ext, nct):
    B, S, D = xa.shape
    NC = w_in_ext.shape[1]
    kind = lambda t: (t >= nct).astype(jnp.int32)
    return pl.pallas_call(
        _win_kernel,
        grid=(B, S // TM),
        in_specs=[
            pl.BlockSpec((1, TM, D), lambda b, t: (b, t, 0)),
            pl.BlockSpec((1, 1, 6, D), lambda b, t: (b, kind(t), 0, 0)),
            pl.BlockSpec((D, NC), lambda b, t: (0, 0)),
        ],
        out_specs=[
            pl.BlockSpec((1, TM, RW_COLS), lambda b, t: (b, t, 0)),
            pl.BlockSpec((1, TM, MLA_COLS_EXT), lambda b, t: (b, t, 0)),
            pl.BlockSpec((1, TM, NA_COLS), lambda b, t: (b, t, 0)),
        ],
        out_shape=[
            jax.ShapeDtypeStruct((B, S, RW_COLS), F32),
            jax.ShapeDtypeStruct((B, S, MLA_COLS_EXT), F32),
            jax.ShapeDtypeStruct((B, S, NA_COLS), BF16),
        ],
        compiler_params=_cparams(("arbitrary", "arbitrary")),
        name="in_proj",
    )(xa, mod6, w_in_ext)


def _group_sum(x, ones_ref):
    return _dot_hl(x, ones_ref[...])


def _rwfeat_kernel(p_ref, pp_ref, pn_ref, mu_ref, w0_ref, w2_ref, a0_ref, a2_ref, g2_ref,
                   kk_ref, rk_ref, ones_ref, f_ref, g_ref, bonus_ref, *, nct, nt):
    t = pl.program_id(1)
    p = p_ref[0]
    first = jnp.logical_or(t == 0, t == nct)
    last = jnp.logical_or(t == nct - 1, t == nt - 1)
    prev_row = jnp.where(first, 0.0, pp_ref[0, 7:8, :])
    next_row = jnp.where(last, 0.0, pn_ref[0, 0:1, :])
    rows = lax.broadcasted_iota(jnp.int32, p.shape, 0)
    prev = jnp.where(rows == 0, prev_row, pltpu.roll(p, 1, axis=0))
    nxt = jnp.where(rows == TM - 1, next_row, pltpu.roll(p, TM - 1, axis=0))
    mu = mu_ref[...]
    xs = p + mu[0:1] * (prev - p) + mu[1:2] * (nxt - p)

    D3 = 3 * RW_DIM
    r = xs[:, 0:RW_DIM]
    k = xs[:, RW_DIM:2 * RW_DIM]
    v = xs[:, 2 * RW_DIM:D3]
    w_lo = xs[:, D3:D3 + 2 * RW_LORA]
    a_lo = xs[:, D3 + 2 * RW_LORA:D3 + 4 * RW_LORA]
    g_pre = xs[:, D3 + 4 * RW_LORA:]

    lw = _dot(jnp.tanh(w_lo).astype(BF16), w2_ref[...]) + w0_ref[...]
    logw = jnp.minimum(lw, 0.0) - jnp.log(1.0 + jnp.exp(-jnp.abs(lw))) - 0.5
    decay = jnp.exp(-jnp.exp(logw))
    a = _sigmoid(_dot(a_lo.astype(BF16), a2_ref[...]) + a0_ref[...])

    kkr = k * kk_ref[...]
    kk = kkr * lax.rsqrt(_group_sum(kkr * kkr, ones_ref) + 1e-12)
    g_ref[0] = _dot(_sigmoid(g_pre).astype(BF16), g2_ref[...])
    bonus_ref[0] = _group_sum(r * k * rk_ref[...], ones_ref) * v

    comps = (r, k, v, kk, decay[:, :RW_DIM], a[:, :RW_DIM], decay[:, RW_DIM:], a[:, RW_DIM:])
    N = RW_HEAD_DIM
    for h in range(RW_HEADS):
        for ci, comp in enumerate(comps):
            col = (h * len(comps) + ci) * N
            f_ref[:, col:col + N] = comp[:, h * N:(h + 1) * N]


def _rw_features(p_rw, prm, nct):
    B, S, C = p_rw.shape
    nt = S // TM
    hb = TM // 8
    last_hb = S // 8 - 1
    full = lambda shape: pl.BlockSpec(shape, lambda b, t: (0,) * len(shape))
    return pl.pallas_call(
        functools.partial(_rwfeat_kernel, nct=nct, nt=nt),
        grid=(B, nt),
        in_specs=[
            pl.BlockSpec((1, TM, C), lambda b, t: (b, t, 0)),
            pl.BlockSpec((1, 8, C), lambda b, t: (b, jnp.maximum(t * hb - 1, 0), 0)),
            pl.BlockSpec((1, 8, C), lambda b, t: (b, jnp.minimum((t + 1) * hb, last_hb), 0)),
            full((2, C)),
            full((1, 2 * RW_DIM)), full((2 * RW_LORA, 2 * RW_DIM)),
            full((1, 2 * RW_DIM)), full((2 * RW_LORA, 2 * RW_DIM)),
            full((RW_G_LORA, RW_DIM)),
            full((1, RW_DIM)), full((1, RW_DIM)),
            full((RW_DIM, RW_DIM)),
        ],
        out_specs=[
            pl.BlockSpec((TM, 8 * RW_DIM), lambda b, t: (t, b)),
            pl.BlockSpec((1, TM, RW_DIM), lambda b, t: (b, t, 0)),
            pl.BlockSpec((1, TM, RW_DIM), lambda b, t: (b, t, 0)),
        ],
        out_shape=[
            jax.ShapeDtypeStruct((S, B * 8 * RW_DIM), F32),
            jax.ShapeDtypeStruct((B, S, RW_DIM), F32),
            jax.ShapeDtypeStruct((B, S, RW_DIM), F32),
        ],
        compiler_params=_cparams(("arbitrary", "arbitrary")),
        name="rw_features",
    )(p_rw, p_rw, p_rw, prm["mu"], prm["w0"], prm["w2"], prm["a0"], prm["a2"], prm["g2"],
      prm["kk"], prm["rk"], prm["ones"])


def _scan_kernel(fs_ref, fd_ref, ka_ref, o_ref, s_ref, tmp_ref, *, tb):
    d = pl.program_id(0)
    g = pl.program_id(1)
    N = RW_HEAD_DIM

    @pl.when(g == 0)
    def _():
        s_ref[...] = jnp.zeros_like(s_ref)

    ka = ka_ref[...]

    def step(i, carry):
        tt = jnp.where(d == 0, i, tb - 1 - i)
        r = fs_ref[tt, 0]
        k = fs_ref[tt, 1]
        v = fs_ref[tt, 2]
        kk = fs_ref[tt, 3]
        w = fd_ref[tt, 0]
        a = fd_ref[tt, 1]
        b = a * kk
        kd = k * (1.0 + (a - 1.0) * ka)
        wr = w * r
        br = jnp.sum(b * r, axis=0, keepdims=True)
        kr = jnp.sum(kd * r, axis=0, keepdims=True)
        tmp_ref[0] = wr
        tmp_ref[1] = b
        tmp_ref[2] = kd
        sa = [jnp.zeros_like(v), jnp.zeros_like(v)]
        op = [jnp.zeros_like(v), jnp.zeros_like(v)]
        for j in range(N):
            sk = s_ref[j]
            sa[j % 2] = sa[j % 2] + sk * fs_ref[tt, 3, pl.ds(j, 1), :]
            op[j % 2] = op[j % 2] + sk * tmp_ref[0, pl.ds(j, 1), :]
        sa = sa[0] + sa[1]
        op = op[0] + op[1]
        for j in range(N):
            s_ref[j] = (s_ref[j] * fd_ref[tt, 0, pl.ds(j, 1), :]
                        - sa * tmp_ref[1, pl.ds(j, 1), :]
                        + v * tmp_ref[2, pl.ds(j, 1), :])
        o_ref[0, tt] = op - sa * br + v * kr
        return carry

    lax.fori_loop(0, tb, step, 0)


def _rw_scan(ft, ka_t, n_ctx):
    S, _, N, BH = ft.shape
    tb = SCAN_TB
    nb = S // tb
    ncb = n_ctx // tb

    def tblk(d, g):
        bwd = jnp.where(g < ncb, ncb - 1 - g, nb - 1 - g + ncb)
        return jnp.where(d == 0, g, bwd)

    return pl.pallas_call(
        functools.partial(_scan_kernel, tb=tb),
        grid=(2, nb),
        in_specs=[
            pl.BlockSpec((tb, 4, N, BH), lambda d, g: (tblk(d, g), 0, 0, 0)),
            pl.BlockSpec((tb, 2, N, BH), lambda d, g: (tblk(d, g), 2 + d, 0, 0)),
            pl.BlockSpec((N, BH), lambda d, g: (0, 0)),
        ],
        out_specs=pl.BlockSpec((1, tb, N, BH), lambda d, g: (d, tblk(d, g), 0, 0)),
        out_shape=jax.ShapeDtypeStruct((2, S, N, BH), F32),
        scratch_shapes=[pltpu.VMEM((N, N, BH), F32), pltpu.VMEM((3, N, BH), F32)],
        compiler_params=_cparams(("arbitrary", "arbitrary")),
        name="rw_scan",
    )(ft, ft, ka_t)


def _mlaprep_kernel(p_ref, cos_ref, sin_ref, qn_ref, kvn_ref, wuq_ref, wukv_ref, q_ref, k_ref, v_ref):
    p = p_ref[0]
    cos = cos_ref[...]
    sin = sin_ref[...]
    scale = (MLA_NOPE + MLA_ROPE) ** -0.5

    def rms(x, g):
        return x * lax.rsqrt(jnp.mean(x * x, axis=-1, keepdims=True) + 1e-6) * g

    q = _dot(rms(p[:, :MLA_Q_LORA], qn_ref[...]).astype(BF16), wuq_ref[...])
    kv = _dot(rms(p[:, MLA_Q_LORA:MLA_Q_LORA + MLA_KV_LORA], kvn_ref[...]).astype(BF16), wukv_ref[...])
    c0 = MLA_Q_LORA + MLA_KV_LORA
    kr = (p[:, c0:c0 + 128] * cos + p[:, c0 + 128:c0 + 256] * sin).astype(BF16)
    for h in range(MLA_HEADS):
        qb = h * 384
        q_ref[0, :, h * 256:h * 256 + 128] = (q[:, qb:qb + 128] * scale).astype(BF16)
        q_ref[0, :, h * 256 + 128:h * 256 + 256] = (
            (q[:, qb + 128:qb + 256] * cos + q[:, qb + 256:qb + 384] * sin) * scale).astype(BF16)
        k_ref[0, :, h * 256:h * 256 + 128] = kv[:, h * 256:h * 256 + 128].astype(BF16)
        k_ref[0, :, h * 256 + 128:h * 256 + 256] = kr
        v_ref[0, :, h * 128:(h + 1) * 128] = kv[:, h * 256 + 128:h * 256 + 256].astype(BF16)


def _mla_prep(p_mla, cos128, sin128, prm):
    B, S, C = p_mla.shape
    full = lambda shape: pl.BlockSpec(shape, lambda b, t: (0,) * len(shape))
    H = MLA_HEADS
    return pl.pallas_call(
        _mlaprep_kernel,
        grid=(B, S // TM),
        in_specs=[
            pl.BlockSpec((1, TM, C), lambda b, t: (b, t, 0)),
            pl.BlockSpec((TM, 128), lambda b, t: (t, 0)),
            pl.BlockSpec((TM, 128), lambda b, t: (t, 0)),
            full((1, MLA_Q_LORA)), full((1, MLA_KV_LORA)),
            full((MLA_Q_LORA, H * 384)), full((MLA_KV_LORA, H * 256)),
        ],
        out_specs=[
            pl.BlockSpec((1, TM, H * 256), lambda b, t: (b, t, 0)),
            pl.BlockSpec((1, TM, H * 256), lambda b, t: (b, t, 0)),
            pl.BlockSpec((1, TM, H * 128), lambda b, t: (b, t, 0)),
        ],
        out_shape=[
            jax.ShapeDtypeStruct((B, S, H * 256), BF16),
            jax.ShapeDtypeStruct((B, S, H * 256), BF16),
            jax.ShapeDtypeStruct((B, S, H * 128), BF16),
        ],
        compiler_params=_cparams(("arbitrary", "arbitrary")),
        name="mla_prep",
    )(p_mla, cos128, sin128, prm["q_norm"], prm["kv_norm"], prm["w_uq"], prm["w_ukv"])


def _mla_attn_kernel(q_ref, k_ref, v_ref, o_ref, *, t0, nct, n_ctx):
    t = pl.program_id(1) + t0

    def attend(n_keys):
        for h in range(MLA_HEADS):
            q = q_ref[0, :, h * 256:(h + 1) * 256]
            k = k_ref[0, 0:n_keys, h * 256:(h + 1) * 256]
            s = lax.dot_general(q, k, (((1,), (1,)), ((), ())), preferred_element_type=F32)
            m = jnp.max(s, axis=-1, keepdims=True)
            e = jnp.exp(s - m)
            l = jnp.sum(e, axis=-1, keepdims=True)
            o = _dot(e.astype(BF16), v_ref[0, 0:n_keys, h * 128:(h + 1) * 128])
            o_ref[0, :, h * 128:(h + 1) * 128] = (o / l).astype(BF16)

    S = k_ref.shape[1]
    if t0 < nct:
        @pl.when(t < nct)
        def _():
            attend(n_ctx)

        @pl.when(t >= nct)
        def _():
            attend(S)
    else:
        attend(S)


def _mla_attn(q, k, v, t0, nct, n_ctx):
    B, S, _ = q.shape
    nq = S // TM - t0
    H = MLA_HEADS
    return pl.pallas_call(
        functools.partial(_mla_attn_kernel, t0=t0, nct=nct, n_ctx=n_ctx),
        grid=(B, nq),
        in_specs=[
            pl.BlockSpec((1, TM, H * 256), lambda b, t: (b, t + t0, 0)),
            pl.BlockSpec((1, S, H * 256), lambda b, t: (b, 0, 0)),
            pl.BlockSpec((1, S, H * 128), lambda b, t: (b, 0, 0)),
        ],
        out_specs=pl.BlockSpec((1, TM, H * 128), lambda b, t: (b, t, 0)),
        out_shape=jax.ShapeDtypeStruct((B, nq * TM, H * 128), BF16),
        compiler_params=_cparams(("arbitrary", "arbitrary")),
        name="mla_attn",
    )(q, k, v)


NA_QR = 4
NA_KR = NA_WIN_R + NA_QR
NA_QB = NA_QR * GRID_W


def _na_kernel(p_ref, bias_ref, o_ref, *, n_ctx, rows, with_ctx):
    s_id = pl.program_id(1)
    W = GRID_W
    n_loc = NA_KR * W
    scale = NA_HEAD_DIM ** -0.5
    nq_ctx = n_ctx // NA_QB
    nblk = rows // NA_QR

    def heads_out(q, parts):
        outs = []
        for h in range(NA_HEADS):
            hs = slice(h * NA_HEAD_DIM, (h + 1) * NA_HEAD_DIM)
            qh = q[:, hs] * scale
            ss = []
            for kx, vx, bias in parts:
                s = lax.dot_general(qh, kx[:, hs], (((1,), (1,)), ((), ())), preferred_element_type=F32)
                if bias is not None:
                    s = s + bias(h)
                ss.append(s)
            m = ss[0].max(axis=-1, keepdims=True)
            for s in ss[1:]:
                m = jnp.maximum(m, s.max(axis=-1, keepdims=True))
            acc = 0.0
            l = 0.0
            for s, (kx, vx, bias) in zip(ss, parts):
                e = jnp.exp(s - m)
                l = l + jnp.sum(e, axis=-1, keepdims=True)
                acc = acc + _dot(e.astype(BF16), vx[:, hs])
            outs.append(acc / l)
        return jnp.concatenate(outs, axis=-1).astype(BF16)

    k_c = p_ref[0, 0:n_ctx, NA_DIM:2 * NA_DIM]
    v_c = p_ref[0, 0:n_ctx, 2 * NA_DIM:3 * NA_DIM]

    def lat_block(j):
        i0 = j * NA_QR
        k_start = jnp.clip(i0 - NA_WIN_R // 2, 0, rows - NA_KR)
        pat = jnp.where(j == 0, 0, jnp.where(j == nblk - 1, 2, 1))
        q0 = pl.multiple_of(n_ctx + i0 * W, NA_QB)
        k0 = pl.multiple_of(n_ctx + k_start * W, W)
        q = p_ref[0, pl.ds(q0, NA_QB), 0:NA_DIM]
        k_l = p_ref[0, pl.ds(k0, n_loc), NA_DIM:2 * NA_DIM]
        v_l = p_ref[0, pl.ds(k0, n_loc), 2 * NA_DIM:3 * NA_DIM]
        o_ref[0] = heads_out(q, [(k_l, v_l, lambda h: bias_ref[pat, h]), (k_c, v_c, None)])

    if with_ctx:
        @pl.when(s_id < nq_ctx)
        def _():
            q0 = pl.multiple_of(s_id * NA_QB, NA_QB)
            q = p_ref[0, pl.ds(q0, NA_QB), 0:NA_DIM]
            o_ref[0] = heads_out(q, [(k_c, v_c, None)])

        @pl.when(s_id >= nq_ctx)
        def _():
            lat_block(s_id - nq_ctx)
    else:
        lat_block(s_id)


def _na_attn(p_na, bias_tab, n_ctx, with_ctx):
    B, S, C = p_na.shape
    T = S - n_ctx
    rows = T // GRID_W
    assert rows % NA_QR == 0 and rows >= NA_KR and n_ctx % NA_QB == 0
    nsteps = rows // NA_QR + (n_ctx // NA_QB if with_ctx else 0)
    return pl.pallas_call(
        functools.partial(_na_kernel, n_ctx=n_ctx, rows=rows, with_ctx=with_ctx),
        grid=(B, nsteps),
        in_specs=[
            pl.BlockSpec((1, S, C), lambda b, s: (b, 0, 0)),
            pl.BlockSpec(bias_tab.shape, lambda b, s: (0, 0, 0, 0)),
        ],
        out_specs=pl.BlockSpec((1, NA_QB, NA_DIM), lambda b, s: (b, s, 0)),
        out_shape=jax.ShapeDtypeStruct((B, nsteps * NA_QB, NA_DIM), BF16),
        compiler_params=_cparams(("arbitrary", "arbitrary")),
        name="na_attn",
    )(p_na, bias_tab)


def _na_bias_table(rpb):
    col = jnp.arange(GRID_W)
    c_start = jnp.clip(col - NA_WIN_C // 2, 0, GRID_W - NA_WIN_C)
    in_win = (col[None, :] >= c_start[:, None]) & (col[None, :] < c_start[:, None] + NA_WIN_C)
    dc_idx = jnp.clip(col[None, :] - col[:, None] + NA_WIN_C - 1, 0, 2 * NA_WIN_C - 2)
    qa = jnp.arange(NA_QR)[:, None]
    kc = jnp.arange(NA_KR)[None, :]
    pats = []
    for pat in range(3):
        off = (NA_WIN_R // 2) * pat
        first = (0 * qa, qa, 0 * qa + NA_WIN_R // 2)[pat]
        row_ok = (kc >= first) & (kc < first + NA_WIN_R)
        dr_idx = jnp.clip(kc - qa - off + NA_WIN_R - 1, 0, 2 * NA_WIN_R - 2)
        bias = rpb[:, dr_idx][:, :, :, dc_idx]
        ok = row_ok[:, :, None, None] & in_win[None, None, :, :]
        bias = jnp.where(ok[None], bias, NEG_INF)
        bias = jnp.transpose(bias, (0, 1, 3, 2, 4))
        pats.append(bias.reshape(NA_HEADS, NA_QB, NA_KR * GRID_W))
    return jnp.stack(pats).astype(F32)


def _outproj_kernel(orw_ref, bonus_ref, g_ref, mla_ref, na_ref, x_ref, mod_ref, gnw_ref, gnb_ref, ones_ref,
                    wout_ref, ln1g_ref, ln1b_ref, rwh_ref, rwl_ref, rb_ref,
                    x1_ref, h_ref, lg_ref, *, dn_alpha):
    o = orw_ref[0] + orw_ref[1]
    inv_n = 1.0 / RW_HEAD_DIM
    mu = _group_sum(o, ones_ref) * inv_n
    dlt = o - mu
    var = _group_sum(dlt * dlt, ones_ref) * inv_n
    on = dlt * lax.rsqrt(var + RW_GN_EPS) * gnw_ref[...] + gnb_ref[...]
    rw_y = ((on + bonus_ref[0]) * g_ref[0]).astype(BF16)
    y = (_dot(rw_y, wout_ref[0:RW_DIM, :])
         + _dot(mla_ref[0], wout_ref[RW_DIM:RW_DIM + MLA_DIM, :])
         + _dot(na_ref[0], wout_ref[RW_DIM + MLA_DIM:, :]))
    m = mod_ref[0, 0]
    x1 = _ln(dn_alpha * x_ref[0] + m[2:3] * y) * ln1g_ref[...] + ln1b_ref[...]
    x1_ref[0] = x1
    h = _ln(x1) * (1.0 + m[4:5]) + m[3:4]
    h_ref[0] = h.astype(BF16)
    lg_ref[0] = _dot3(h, rwh_ref[...], rwl_ref[...]) + rb_ref[...]


def _out_proj(o_rw, bonus, g, mla_o, na_o, xa, mod6, prm, t0, nct, dn_alpha):
    B, S, D = xa.shape
    nt = S // TM - t0
    So = nt * TM
    kind = lambda t: (t + t0 >= nct).astype(jnp.int32)
    full = lambda shape: pl.BlockSpec(shape, lambda b, t: (0,) * len(shape))
    off = lambda C: pl.BlockSpec((1, TM, C), lambda b, t: (b, t + t0, 0))
    own = lambda C: pl.BlockSpec((1, TM, C), lambda b, t: (b, t, 0))
    return pl.pallas_call(
        functools.partial(_outproj_kernel, dn_alpha=dn_alpha),
        grid=(B, nt),
        in_specs=[
            pl.BlockSpec((2, TM, RW_DIM), lambda b, t: (0, t + t0, b)),
            off(RW_DIM), off(RW_DIM), own(MLA_DIM), own(NA_DIM), off(D),
            pl.BlockSpec((1, 1, 6, D), lambda b, t: (b, kind(t), 0, 0)),
            full((1, RW_DIM)), full((1, RW_DIM)), full((RW_DIM, RW_DIM)),
            full((D, D)), full((1, D)), full((1, D)),
            full((D, 128)), full((D, 128)), full((1, 128)),
        ],
        out_specs=[own(D), own(D), own(128)],
        out_shape=[
            jax.ShapeDtypeStruct((B, So, D), F32),
            jax.ShapeDtypeStruct((B, So, D), BF16),
            jax.ShapeDtypeStruct((B, So, 128), F32),
        ],
        compiler_params=_cparams(("arbitrary", "arbitrary")),
        name="out_proj",
    )(o_rw, bonus, g, mla_o, na_o, xa, mod6, prm["gn_w"], prm["gn_b"], prm["ones"],
      prm["w_out"], prm["ln1_g"], prm["ln1_b"], prm["router_hi"], prm["router_lo"], prm["router_b"])


GU_CHUNK = 256


def _moe_kernel(be_ref, nu_ref, x_ref, wgu_ref, bgu_ref, wdn_ref, bdn_ref, perm_ref, y_ref,
                wgu_s, wdn_s, act_s):
    i = pl.program_id(0)
    F = wdn_ref.shape[2]
    n_chunks = 2 * F // GU_CHUNK
    half = GU_CHUNK // 2
    valid = i < nu_ref[0]
    e = be_ref[i]
    new_expert = jnp.logical_or(i == 0, e != be_ref[jnp.maximum(i - 1, 0)])

    @pl.when(jnp.logical_and(valid, new_expert))
    def _():
        for c in range(n_chunks):
            cs = slice(c * GU_CHUNK, (c + 1) * GU_CHUNK)
            wgu_s[:, cs] = _dot(wgu_ref[0, 0, :, cs].astype(BF16), perm_ref[...]).astype(BF16)
        wdn_s[...] = wdn_ref[0, 0].astype(BF16)

    @pl.when(valid)
    def _():
        gu = _dot(x_ref[...], wgu_s[...]) + bgu_ref[0]
        for c in range(n_chunks):
            glu = jnp.minimum(gu[:, c * GU_CHUNK:c * GU_CHUNK + half], SWIGLU_LIMIT)
            lin = jnp.clip(gu[:, c * GU_CHUNK + half:(c + 1) * GU_CHUNK], -SWIGLU_LIMIT, SWIGLU_LIMIT)
            act_s[:, c * half:(c + 1) * half] = (glu * _sigmoid(SWIGLU_ALPHA * glu) * (lin + 1.0)).astype(BF16)
        y = _dot(act_s[...], wdn_s[...]) + bdn_ref[0]
        y_ref[...] = y.astype(BF16)

    @pl.when(jnp.logical_not(valid))
    def _():
        y_ref[...] = jnp.zeros_like(y_ref)


def _moe_kernel_into(be_ref, nu_ref, x_ref, wgu_ref, bgu_ref, wdn_ref, bdn_ref, perm_ref, yprev_ref, y_ref,
                     wgu_s, wdn_s, act_s):
    del yprev_ref
    _moe_kernel(be_ref, nu_ref, x_ref, wgu_ref, bgu_ref, wdn_ref, bdn_ref, perm_ref, y_ref, wgu_s, wdn_s, act_s)


def _moe_ffn(x_part, block_e, n_used, w_gu, b_gu, w_dn, b_dn, layer, blk0, n_pad, y_prev):
    D = x_part.shape[1]
    _, E, _, F2 = w_gu.shape
    F = F2 // 2
    n_blocks = x_part.shape[0] // MOE_BLK
    half = GU_CHUNK // 2
    src = np.concatenate([2 * np.arange(half), 2 * np.arange(half) + 1])
    perm = jnp.asarray(np.eye(GU_CHUNK, dtype=np.float32)[:, src], BF16)
    bgu_p = jnp.swapaxes(b_gu.reshape(E, F2 // GU_CHUNK, half, 2), 2, 3).reshape(E, 1, F2)
    grid_spec = pltpu.PrefetchScalarGridSpec(
        num_scalar_prefetch=2,
        grid=(n_blocks,),
        in_specs=[
            pl.BlockSpec((MOE_BLK, D), lambda i, be, nu: (i, 0)),
            pl.BlockSpec((1, 1, D, F2), lambda i, be, nu: (layer, be[i], 0, 0)),
            pl.BlockSpec((1, 1, F2), lambda i, be, nu: (be[i], 0, 0)),
            pl.BlockSpec((1, 1, F, D), lambda i, be, nu: (layer, be[i], 0, 0)),
            pl.BlockSpec((1, 1, D), lambda i, be, nu: (be[i], 0, 0)),
            pl.BlockSpec((GU_CHUNK, GU_CHUNK), lambda i, be, nu: (0, 0)),
        ] + ([] if y_prev is None else [pl.BlockSpec(memory_space=pl.ANY)]),
        out_specs=pl.BlockSpec((MOE_BLK, D), lambda i, be, nu: (i + blk0, 0)),
        scratch_shapes=[pltpu.VMEM((D, F2), BF16), pltpu.VMEM((F, D), BF16), pltpu.VMEM((MOE_BLK, F), BF16)],
    )
    args = (block_e, n_used, x_part, w_gu, bgu_p, w_dn, b_dn.reshape(E, 1, D), perm)
    return pl.pallas_call(
        _moe_kernel if y_prev is None else _moe_kernel_into,
        grid_spec=grid_spec,
        out_shape=jax.ShapeDtypeStruct((n_pad, D), BF16),
        input_output_aliases={} if y_prev is None else {len(args): 0},
        compiler_params=_cparams(("arbitrary",)),
        name="moe_ffn",
    )(*args, *(() if y_prev is None else (y_prev,)))


def _route(logits, n_tok):
    top_v, top_i = lax.top_k(logits, TOP_K)
    gates = jax.nn.softmax(top_v, axis=-1)
    n_assign = n_tok * TOP_K
    flat_e = top_i.reshape(-1)
    onehot = (flat_e[:, None] == jnp.arange(N_EXPERTS)[None, :]).astype(jnp.int32)
    rank = jnp.take_along_axis(jnp.cumsum(onehot, axis=0), flat_e[:, None], axis=1)[:, 0] - 1
    counts = jnp.sum(onehot, axis=0)
    padded = (counts + MOE_BLK - 1) // MOE_BLK * MOE_BLK
    ends_p = jnp.cumsum(padded)
    dest = (ends_p - padded)[flat_e] + rank
    n_blocks = -(-(n_assign + N_EXPERTS * (MOE_BLK - 1)) // MOE_BLK)
    n_pad = n_blocks * MOE_BLK
    blk_start = jnp.arange(n_blocks, dtype=jnp.int32) * MOE_BLK
    block_e = jnp.minimum(jnp.sum((ends_p[None, :] <= blk_start[:, None]).astype(jnp.int32), axis=1),
                          N_EXPERTS - 1).astype(jnp.int32)
    n_used = (ends_p[-1] // MOE_BLK).astype(jnp.int32).reshape(1)
    order = jnp.argsort(flat_e, stable=True).astype(jnp.int32)
    slot_e = jnp.repeat(block_e, MOE_BLK)
    j = jnp.arange(n_pad, dtype=jnp.int32) - (ends_p - padded)[slot_e]
    src = jnp.clip((jnp.cumsum(counts) - counts)[slot_e] + j, 0, n_assign - 1)
    tok_pad = jnp.where(j < counts[slot_e], order[src] // TOP_K, 0).astype(jnp.int32)
    return gates, dest.reshape(n_tok, TOP_K).astype(jnp.int32), tok_pad, block_e, n_used


def _final_kernel(x1_ref, yk_ref, gate_ref, mod_ref, g_ref, b_ref, o_ref, *, dn_alpha):
    D = x1_ref.shape[2]
    m = mod_ref[0, 0]
    gates = gate_ref[0]
    f = yk_ref[0, 0].astype(F32) * gates[:, 0:1]
    for k in range(1, TOP_K):
        f = f + yk_ref[k, 0].astype(F32) * gates[:, k:k + 1]
    o_ref[0] = _ln(dn_alpha * x1_ref[0] + m[5:6] * f) * g_ref[...] + b_ref[...]


def _final(x1, yk, gates, mod6, ln_g, ln_b, t0, nct, dn_alpha):
    B, So, D = x1.shape
    kind = lambda t: (t + t0 >= nct).astype(jnp.int32)
    blk = pl.BlockSpec((1, TM, D), lambda b, t: (b, t, 0))
    vec = pl.BlockSpec((1, D), lambda b, t: (0, 0))
    return pl.pallas_call(
        functools.partial(_final_kernel, dn_alpha=dn_alpha),
        grid=(B, So // TM),
        in_specs=[blk,
                  pl.BlockSpec((TOP_K, 1, TM, D), lambda b, t: (0, b, t, 0)),
                  pl.BlockSpec((1, TM, TOP_K), lambda b, t: (b, t, 0)),
                  pl.BlockSpec((1, 1, 6, D), lambda b, t: (b, kind(t), 0, 0)), vec, vec],
        out_specs=blk,
        out_shape=jax.ShapeDtypeStruct((B, So, D), F32),
        compiler_params=_cparams(("arbitrary", "arbitrary")),
        name="ffn_residual",
    )(x1, yk, gates, mod6, ln_g, ln_b)


def _rope_tables(n_ctx, T):
    t = jnp.arange(T)
    row = (t // GRID_W).astype(F32)
    col = (t % GRID_W).astype(F32)
    n_freq = MLA_ROPE // 4
    inv = ROPE_BASE ** (-jnp.arange(n_freq, dtype=F32) / n_freq)
    ang = jnp.concatenate([row[:, None] * inv, col[:, None] * inv], axis=-1)
    cos, sin = jnp.cos(ang), jnp.sin(ang)
    z = jnp.zeros((T, 128 - MLA_ROPE), F32)
    cos128 = jnp.concatenate([cos, cos, z], axis=-1)
    sin128 = jnp.concatenate([-sin, sin, z], axis=-1)
    cos_c = jnp.concatenate([jnp.ones((n_ctx, MLA_ROPE), F32), jnp.zeros((n_ctx, 128 - MLA_ROPE), F32)], -1)
    sin_c = jnp.zeros((n_ctx, 128), F32)
    return jnp.concatenate([cos_c, cos128], 0), jnp.concatenate([sin_c, sin128], 0)


def _rope_slabs(w):
    ev, od = w[:, 0::2], w[:, 1::2]
    z = jnp.zeros((w.shape[0], 128 - MLA_ROPE), w.dtype)
    return jnp.concatenate([ev, od, z, od, ev, z], axis=-1)


def _blockdiag2(m):
    z = jnp.zeros_like(m[0])
    return jnp.concatenate([jnp.concatenate([m[0], z], 1), jnp.concatenate([z, m[1]], 1)], 0)


def kernel(x, c, ctx, c_ctx, ada_w, ada_b, w_in, rw_mu, rw_w0, rw_w2, rw_a0, rw_a2, rw_g2, rw_kk, rw_ka, rw_rk, rw_gn_w, rw_gn_b, mla_q_norm, mla_kv_norm, mla_w_uq, mla_w_ukv, na_rpb, w_out, ln1_g, ln1_b, router_w, router_b, w_gu, b_gu, w_dn, b_dn, ln2_g, ln2_b):
    B, T, D = x.shape
    n_ctx = ctx.shape[1]
    depth = ada_w.shape[0]
    S = n_ctx + T
    assert n_ctx % TM == 0 and T % TM == 0 and T % GRID_W == 0 and T // GRID_W >= NA_WIN_R
    nct = n_ctx // TM
    BH = B * RW_HEADS
    dn_alpha = (2 * depth) ** 0.25
    F = w_dn.shape[2]

    R = (B + 1 + 7) // 8 * 8
    cond = jnp.zeros((R, D), F32).at[:B].set(c).at[B].set(c_ctx)
    mod = _ada_mod(cond, ada_w, ada_b)
    mod_l = mod[:, :B].reshape(depth, B, 1, 6, D)
    mod_c = jnp.broadcast_to(mod[:, B].reshape(depth, 1, 1, 6, D), (depth, B, 1, 6, D))
    mod6 = jnp.concatenate([mod_c, mod_l], axis=2)

    cos128, sin128 = _rope_tables(n_ctx, T)
    ones_blk = jnp.kron(jnp.eye(RW_HEADS, dtype=F32), jnp.ones((RW_HEAD_DIM, RW_HEAD_DIM), F32)).astype(BF16)

    na_bias = [_na_bias_table(na_rpb[l]) for l in range(depth)]

    xa = jnp.concatenate([ctx, x], axis=1)
    for l in range(depth):
        need_ctx = l < depth - 1
        t0 = 0 if need_ctx else nct

        wi = w_in[l]
        c_m = RW_COLS
        w_in_ext = jnp.concatenate(
            [wi[:, :c_m + MLA_Q_LORA + MLA_KV_LORA], _rope_slabs(wi[:, c_m + MLA_Q_LORA + MLA_KV_LORA:c_m + MLA_COLS]),
             wi[:, c_m + MLA_COLS:]], axis=-1).astype(BF16)
        wuq = mla_w_uq[l].reshape(MLA_Q_LORA, MLA_HEADS, MLA_NOPE + MLA_ROPE)
        wuq_ext = jnp.concatenate(
            [jnp.concatenate([wuq[:, h, :MLA_NOPE], _rope_slabs(wuq[:, h, MLA_NOPE:])], -1) for h in range(MLA_HEADS)],
            axis=-1).astype(BF16)
        rw_prm = dict(
            mu=rw_mu[l],
            w0=rw_w0[l].reshape(1, 2 * RW_DIM), w2=_blockdiag2(rw_w2[l]).astype(BF16),
            a0=rw_a0[l].reshape(1, 2 * RW_DIM), a2=_blockdiag2(rw_a2[l]).astype(BF16),
            g2=rw_g2[l].astype(BF16), kk=rw_kk[l].reshape(1, RW_DIM), rk=rw_rk[l].reshape(1, RW_DIM),
            ones=ones_blk)
        mla_prm = dict(q_norm=mla_q_norm[l].reshape(1, -1), kv_norm=mla_kv_norm[l].reshape(1, -1),
                       w_uq=wuq_ext, w_ukv=mla_w_ukv[l].astype(BF16))
        rt = jnp.zeros((D, 128), F32).at[:, :N_EXPERTS].set(router_w[l])
        rt_hi = rt.astype(BF16)
        out_prm = dict(gn_w=rw_gn_w[l].reshape(1, -1), gn_b=rw_gn_b[l].reshape(1, -1), ones=ones_blk,
                       w_out=w_out[l].astype(BF16), ln1_g=ln1_g[l].reshape(1, -1), ln1_b=ln1_b[l].reshape(1, -1),
                       router_hi=rt_hi, router_lo=(rt - rt_hi.astype(F32)).astype(BF16),
                       router_b=jnp.zeros((1, 128), F32).at[0, :N_EXPERTS].set(router_b[l]))

        p_rw, p_mla, p_na = _in_proj(xa, mod6[l], w_in_ext, nct)
        feat, g_gate, bonus = _rw_features(p_rw, rw_prm, nct)
        ft = jnp.swapaxes(feat.reshape(S, BH, 8 * RW_HEAD_DIM), 1, 2).reshape(S, 8, RW_HEAD_DIM, BH)
        ka_t = jnp.tile(rw_ka[l].reshape(RW_HEADS, RW_HEAD_DIM).T[:, None, :], (1, B, 1)).reshape(RW_HEAD_DIM, BH)
        q, k, v = _mla_prep(p_mla, cos128, sin128, mla_prm)
        mla_o = _mla_attn(q, k, v, t0, nct, n_ctx)
        na_o = _na_attn(p_na, na_bias[l], n_ctx, need_ctx)

        o_scan = _rw_scan(ft, ka_t, n_ctx)
        o_rw = jnp.swapaxes(o_scan, 2, 3).reshape(2, S, B * RW_DIM)

        x1, h, logits = _out_proj(o_rw, bonus, g_gate, mla_o, na_o, xa, mod6[l], out_prm, t0, nct, dn_alpha)

        So = x1.shape[1]
        n_tok = B * So
        gates, dest, tok_pad, block_e, n_used = _route(logits.reshape(n_tok, 128)[:, :N_EXPERTS], n_tok)
        n_pad = tok_pad.shape[0]
        n_blocks = n_pad // MOE_BLK
        parts = MOE_PARTS if n_blocks % MOE_PARTS == 0 else 1
        pb = n_blocks // parts
        h2 = h.reshape(n_tok, D)
        y_sorted = None
        for c in range(parts):
            x_part = h2.at[tok_pad[c * pb * MOE_BLK:(c + 1) * pb * MOE_BLK]].get(mode="promise_in_bounds")
            y_sorted = _moe_ffn(x_part, block_e[c * pb:(c + 1) * pb], jnp.clip(n_used - c * pb, 0, pb),
                                w_gu, b_gu[l], w_dn, b_dn[l], l, c * pb, n_pad, y_sorted)
        yk = y_sorted.at[dest.T.reshape(-1)].get(mode="promise_in_bounds").reshape(TOP_K, B, So, D)

        xa = _final(x1, yk, gates.reshape(B, So, TOP_K), mod6[l], ln2_g[l].reshape(1, -1), ln2_b[l].reshape(1, -1),
                    t0, nct, dn_alpha)
    return xa
```

```python
import functools

import numpy as np
import jax
import jax.numpy as jnp
from jax import lax
from jax.experimental import pallas as pl
from jax.experimental.pallas import tpu as pltpu
from jax.experimental.pallas import tpu_sc as plsc

F32 = jnp.float32
BF16 = jnp.bfloat16

GRID_W = 64
RW_HEAD_DIM = 64
RW_HEADS = 4
RW_DIM = RW_HEADS * RW_HEAD_DIM
RW_LORA = 64
RW_G_LORA = 128
RW_GN_EPS = 64e-5
RW_COLS = 3 * RW_DIM + 4 * RW_LORA + RW_G_LORA
MLA_HEADS = 4
MLA_NOPE = 128
MLA_ROPE = 64
MLA_V = 128
MLA_Q_LORA = 256
MLA_KV_LORA = 128
MLA_COLS = MLA_Q_LORA + MLA_KV_LORA + MLA_ROPE
MLA_COLS_EXT = MLA_Q_LORA + MLA_KV_LORA + 256
MLA_DIM = MLA_HEADS * MLA_V
NA_HEADS = 4
NA_HEAD_DIM = 64
NA_DIM = NA_HEADS * NA_HEAD_DIM
NA_WIN_R = 8
NA_WIN_C = 16
NA_COLS = 3 * NA_DIM
ROPE_BASE = 10000.0
N_EXPERTS = 32
TOP_K = 4
SWIGLU_ALPHA = 1.702
SWIGLU_LIMIT = 7.0
NEG_INF = -1e30

TM = 256
SCAN_TB = 16
MOE_BLK = 512
MOE_PARTS = 4
VMEM_LIMIT = 56 * 1024 * 1024


def _cparams(sem):
    return pltpu.CompilerParams(dimension_semantics=sem, vmem_limit_bytes=VMEM_LIMIT)


def _ln(x, eps=1e-5):
    mu = jnp.mean(x, axis=-1, keepdims=True)
    d = x - mu
    var = jnp.mean(d * d, axis=-1, keepdims=True)
    return d * lax.rsqrt(var + eps)


def _dot(a, b):
    return jnp.dot(a, b, preferred_element_type=F32)


def _split(a):
    hi = a.astype(BF16)
    lo = (a - hi.astype(F32)).astype(BF16)
    return hi, lo


def _dot_hl(a, b_bf16):
    hi, lo = _split(a)
    return _dot(hi, b_bf16) + _dot(lo, b_bf16)


def _dot3(a, b_hi, b_lo):
    hi, lo = _split(a)
    return _dot(hi, b_hi) + _dot(lo, b_hi) + _dot(hi, b_lo)


def _sigmoid(x):
    return 1.0 / (1.0 + jnp.exp(-x))


def _pack_bf16_pair(x):
    w = x.shape[1] // 2
    u = pltpu.bitcast(x.astype(BF16).astype(F32), jnp.uint32)
    lo = lax.shift_right_logical(u[:, :w], jnp.uint32(16))
    hi = lax.bitwise_and(u[:, w:], jnp.uint32(0xFFFF0000))
    return pltpu.bitcast(lax.bitwise_or(lo, hi), jnp.int32)


def _unpack_bf16_pair(p):
    u = pltpu.bitcast(p, jnp.uint32)
    lo = pltpu.bitcast(lax.shift_left(u, jnp.uint32(16)), F32)
    hi = pltpu.bitcast(lax.bitwise_and(u, jnp.uint32(0xFFFF0000)), F32)
    return jnp.concatenate([lo, hi], axis=-1)


def _ada_kernel(cond_ref, w_ref, b_ref, o_ref):
    c = cond_ref[...]
    s = c * _sigmoid(c)
    w = w_ref[0]
    w_hi, w_lo = _split(w)
    o_ref[0] = _dot3(s, w_hi, w_lo) + b_ref[0]


def _ada_mod(cond, ada_w, ada_b):
    L, D, N = ada_w.shape
    R = cond.shape[0]
    tn = 512
    return pl.pallas_call(
        _ada_kernel,
        grid=(L, N // tn),
        in_specs=[
            pl.BlockSpec((R, D), lambda l, j: (0, 0)),
            pl.BlockSpec((1, D, tn), lambda l, j: (l, 0, j)),
            pl.BlockSpec((1, 1, tn), lambda l, j: (l, 0, j)),
        ],
        out_specs=pl.BlockSpec((1, R, tn), lambda l, j: (l, 0, j)),
        out_shape=jax.ShapeDtypeStruct((L, R, N), F32),
        compiler_params=_cparams(("arbitrary", "arbitrary")),
        name="ada_mod",
    )(cond, ada_w, ada_b.reshape(L, 1, N))


def _win_kernel(x_ref, mod_ref, w_ref, prw_ref, pmla_ref, pna_ref):
    x = x_ref[0]
    m = mod_ref[0, 0]
    xm = _ln(x) * (1.0 + m[1:2]) + m[0:1]
    p = _dot(xm.astype(BF16), w_ref[...])
    prw_ref[0] = p[:, :RW_COLS]
    pmla_ref[0] = p[:, RW_COLS:RW_COLS + MLA_COLS_EXT]
    pna_ref[0] = p[:, RW_COLS + MLA_COLS_EXT:].astype(BF16)


def _in_proj(xa, mod6, w_in_ext, nct):
    B, S, D = xa.shape
    NC = w_in_ext.shape[1]
    kind = lambda t: (t >= nct).astype(jnp.int32)
    return pl.pallas_call(
        _win_kernel,
        grid=(B, S // TM),
        in_specs=[
            pl.BlockSpec((1, TM, D), lambda b, t: (b, t, 0)),
            pl.BlockSpec((1, 1, 6, D), lambda b, t: (b, kind(t), 0, 0)),
            pl.BlockSpec((D, NC), lambda b, t: (0, 0)),
        ],
        out_specs=[
            pl.BlockSpec((1, TM, RW_COLS), lambda b, t: (b, t, 0)),
            pl.BlockSpec((1, TM, MLA_COLS_EXT), lambda b, t: (b, t, 0)),
            pl.BlockSpec((1, TM, NA_COLS), lambda b, t: (b, t, 0)),
        ],
        out_shape=[
            jax.ShapeDtypeStruct((B, S, RW_COLS), F32),
            jax.ShapeDtypeStruct((B, S, MLA_COLS_EXT), F32),
            jax.ShapeDtypeStruct((B, S, NA_COLS), BF16),
        ],
        compiler_params=_cparams(("arbitrary", "arbitrary")),
        name="in_proj",
    )(xa, mod6, w_in_ext)


def _group_sum(x, ones_ref):
    return _dot_hl(x, ones_ref[...])


def _rwfeat_kernel(p_ref, pp_ref, pn_ref, mu_ref, w0_ref, w2_ref, a0_ref, a2_ref, g2_ref,
                   kk_ref, rk_ref, ones_ref, f_ref, g_ref, bonus_ref, *, nct, nt):
    t = pl.program_id(1)
    p = p_ref[0]
    first = jnp.logical_or(t == 0, t == nct)
    last = jnp.logical_or(t == nct - 1, t == nt - 1)
    prev_row = jnp.where(first, 0.0, pp_ref[0, 7:8, :])
    next_row = jnp.where(last, 0.0, pn_ref[0, 0:1, :])
    rows = lax.broadcasted_iota(jnp.int32, p.shape, 0)
    prev = jnp.where(rows == 0, prev_row, pltpu.roll(p, 1, axis=0))
    nxt = jnp.where(rows == TM - 1, next_row, pltpu.roll(p, TM - 1, axis=0))
    mu = mu_ref[...]
    xs = p + mu[0:1] * (prev - p) + mu[1:2] * (nxt - p)

    D3 = 3 * RW_DIM
    r = xs[:, 0:RW_DIM]
    k = xs[:, RW_DIM:2 * RW_DIM]
    v = xs[:, 2 * RW_DIM:D3]
    w_lo = xs[:, D3:D3 + 2 * RW_LORA]
    a_lo = xs[:, D3 + 2 * RW_LORA:D3 + 4 * RW_LORA]
    g_pre = xs[:, D3 + 4 * RW_LORA:]

    lw = _dot(jnp.tanh(w_lo).astype(BF16), w2_ref[...]) + w0_ref[...]
    logw = jnp.minimum(lw, 0.0) - jnp.log(1.0 + jnp.exp(-jnp.abs(lw))) - 0.5
    decay = jnp.exp(-jnp.exp(logw))
    a = _sigmoid(_dot(a_lo.astype(BF16), a2_ref[...]) + a0_ref[...])

    kkr = k * kk_ref[...]
    kk = kkr * lax.rsqrt(_group_sum(kkr * kkr, ones_ref) + 1e-12)
    g_ref[0] = _dot(_sigmoid(g_pre).astype(BF16), g2_ref[...])
    bonus_ref[0] = _group_sum(r * k * rk_ref[...], ones_ref) * v

    comps = (r, k, v, kk, decay[:, :RW_DIM], a[:, :RW_DIM], decay[:, RW_DIM:], a[:, RW_DIM:])
    N = RW_HEAD_DIM
    for h in range(RW_HEADS):
        for ci, comp in enumerate(comps):
            col = (h * len(comps) + ci) * N
            f_ref[:, col:col + N] = comp[:, h * N:(h + 1) * N]


def _rw_features(p_rw, prm, nct):
    B, S, C = p_rw.shape
    nt = S // TM
    hb = TM // 8
    last_hb = S // 8 - 1
    full = lambda shape: pl.BlockSpec(shape, lambda b, t: (0,) * len(shape))
    return pl.pallas_call(
        functools.partial(_rwfeat_kernel, nct=nct, nt=nt),
        grid=(B, nt),
        in_specs=[
            pl.BlockSpec((1, TM, C), lambda b, t: (b, t, 0)),
            pl.BlockSpec((1, 8, C), lambda b, t: (b, jnp.maximum(t * hb - 1, 0), 0)),
            pl.BlockSpec((1, 8, C), lambda b, t: (b, jnp.minimum((t + 1) * hb, last_hb), 0)),
            full((2, C)),
            full((1, 2 * RW_DIM)), full((2 * RW_LORA, 2 * RW_DIM)),
            full((1, 2 * RW_DIM)), full((2 * RW_LORA, 2 * RW_DIM)),
            full((RW_G_LORA, RW_DIM)),
            full((1, RW_DIM)), full((1, RW_DIM)),
            full((RW_DIM, RW_DIM)),
        ],
        out_specs=[
            pl.BlockSpec((TM, 8 * RW_DIM), lambda b, t: (t, b)),
            pl.BlockSpec((1, TM, RW_DIM), lambda b, t: (b, t, 0)),
            pl.BlockSpec((1, TM, RW_DIM), lambda b, t: (b, t, 0)),
        ],
        out_shape=[
            jax.ShapeDtypeStruct((S, B * 8 * RW_DIM), F32),
            jax.ShapeDtypeStruct((B, S, RW_DIM), F32),
            jax.ShapeDtypeStruct((B, S, RW_DIM), F32),
        ],
        compiler_params=_cparams(("arbitrary", "arbitrary")),
        name="rw_features",
    )(p_rw, p_rw, p_rw, prm["mu"], prm["w0"], prm["w2"], prm["a0"], prm["a2"], prm["g2"],
      prm["kk"], prm["rk"], prm["ones"])


def _scan_kernel(fs_ref, fd_ref, ka_ref, o_ref, s_ref, tmp_ref, *, tb):
    d = pl.program_id(0)
    g = pl.program_id(1)
    N = RW_HEAD_DIM

    @pl.when(g == 0)
    def _():
        s_ref[...] = jnp.zeros_like(s_ref)

    ka = ka_ref[...]

    def step(i, carry):
        tt = jnp.where(d == 0, i, tb - 1 - i)
        r = fs_ref[tt, 0]
        k = fs_ref[tt, 1]
        v = fs_ref[tt, 2]
        kk = fs_ref[tt, 3]
        w = fd_ref[tt, 0]
        a = fd_ref[tt, 1]
        b = a * kk
        kd = k * (1.0 + (a - 1.0) * ka)
        wr = w * r
        br = jnp.sum(b * r, axis=0, keepdims=True)
        kr = jnp.sum(kd * r, axis=0, keepdims=True)
        tmp_ref[0] = wr
        tmp_ref[1] = b
        tmp_ref[2] = kd
        sa = [jnp.zeros_like(v), jnp.zeros_like(v)]
        op = [jnp.zeros_like(v), jnp.zeros_like(v)]
        for j in range(N):
            sk = s_ref[j]
            sa[j % 2] = sa[j % 2] + sk * fs_ref[tt, 3, pl.ds(j, 1), :]
            op[j % 2] = op[j % 2] + sk * tmp_ref[0, pl.ds(j, 1), :]
        sa = sa[0] + sa[1]
        op = op[0] + op[1]
        for j in range(N):
            s_ref[j] = (s_ref[j] * fd_ref[tt, 0, pl.ds(j, 1), :]
                        - sa * tmp_ref[1, pl.ds(j, 1), :]
                        + v * tmp_ref[2, pl.ds(j, 1), :])
        o_ref[0, tt] = op - sa * br + v * kr
        return carry

    lax.fori_loop(0, tb, step, 0)


def _rw_scan(ft, ka_t, n_ctx):
    S, _, N, BH = ft.shape
    tb = SCAN_TB
    nb = S // tb
    ncb = n_ctx // tb

    def tblk(d, g):
        bwd = jnp.where(g < ncb, ncb - 1 - g, nb - 1 - g + ncb)
        return jnp.where(d == 0, g, bwd)

    return pl.pallas_call(
        functools.partial(_scan_kernel, tb=tb),
        grid=(2, nb),
        in_specs=[
            pl.BlockSpec((tb, 4, N, BH), lambda d, g: (tblk(d, g), 0, 0, 0)),
            pl.BlockSpec((tb, 2, N, BH), lambda d, g: (tblk(d, g), 2 + d, 0, 0)),
            pl.BlockSpec((N, BH), lambda d, g: (0, 0)),
        ],
        out_specs=pl.BlockSpec((1, tb, N, BH), lambda d, g: (d, tblk(d, g), 0, 0)),
        out_shape=jax.ShapeDtypeStruct((2, S, N, BH), F32),
        scratch_shapes=[pltpu.VMEM((N, N, BH), F32), pltpu.VMEM((3, N, BH), F32)],
        compiler_params=_cparams(("arbitrary", "arbitrary")),
        name="rw_scan",
    )(ft, ft, ka_t)


def _mlaprep_kernel(p_ref, cos_ref, sin_ref, qn_ref, kvn_ref, wuq_ref, wukv_ref, q_ref, k_ref, v_ref):
    p = p_ref[0]
    cos = cos_ref[...]
    sin = sin_ref[...]
    scale = (MLA_NOPE + MLA_ROPE) ** -0.5

    def rms(x, g):
        return x * lax.rsqrt(jnp.mean(x * x, axis=-1, keepdims=True) + 1e-6) * g

    q = _dot(rms(p[:, :MLA_Q_LORA], qn_ref[...]).astype(BF16), wuq_ref[...])
    kv = _dot(rms(p[:, MLA_Q_LORA:MLA_Q_LORA + MLA_KV_LORA], kvn_ref[...]).astype(BF16), wukv_ref[...])
    c0 = MLA_Q_LORA + MLA_KV_LORA
    kr = (p[:, c0:c0 + 128] * cos + p[:, c0 + 128:c0 + 256] * sin).astype(BF16)
    for h in range(MLA_HEADS):
        qb = h * 384
        q_ref[0, :, h * 256:h * 256 + 128] = (q[:, qb:qb + 128] * scale).astype(BF16)
        q_ref[0, :, h * 256 + 128:h * 256 + 256] = (
            (q[:, qb + 128:qb + 256] * cos + q[:, qb + 256:qb + 384] * sin) * scale).astype(BF16)
        k_ref[0, :, h * 256:h * 256 + 128] = kv[:, h * 256:h * 256 + 128].astype(BF16)
        k_ref[0, :, h * 256 + 128:h * 256 + 256] = kr
        v_ref[0, :, h * 128:(h + 1) * 128] = kv[:, h * 256 + 128:h * 256 + 256].astype(BF16)


def _mla_prep(p_mla, cos128, sin128, prm):
    B, S, C = p_mla.shape
    full = lambda shape: pl.BlockSpec(shape, lambda b, t: (0,) * len(shape))
    H = MLA_HEADS
    return pl.pallas_call(
        _mlaprep_kernel,
        grid=(B, S // TM),
        in_specs=[
            pl.BlockSpec((1, TM, C), lambda b, t: (b, t, 0)),
            pl.BlockSpec((TM, 128), lambda b, t: (t, 0)),
            pl.BlockSpec((TM, 128), lambda b, t: (t, 0)),
            full((1, MLA_Q_LORA)), full((1, MLA_KV_LORA)),
            full((MLA_Q_LORA, H * 384)), full((MLA_KV_LORA, H * 256)),
        ],
        out_specs=[
            pl.BlockSpec((1, TM, H * 256), lambda b, t: (b, t, 0)),
            pl.BlockSpec((1, TM, H * 256), lambda b, t: (b, t, 0)),
            pl.BlockSpec((1, TM, H * 128), lambda b, t: (b, t, 0)),
        ],
        out_shape=[
            jax.ShapeDtypeStruct((B, S, H * 256), BF16),
            jax.ShapeDtypeStruct((B, S, H * 256), BF16),
            jax.ShapeDtypeStruct((B, S, H * 128), BF16),
        ],
        compiler_params=_cparams(("arbitrary", "arbitrary")),
        name="mla_prep",
    )(p_mla, cos128, sin128, prm["q_norm"], prm["kv_norm"], prm["w_uq"], prm["w_ukv"])


def _mla_attn_kernel(q_ref, k_ref, v_ref, o_ref, *, t0, nct, n_ctx):
    t = pl.program_id(1) + t0

    def attend(n_keys):
        for h in range(MLA_HEADS):
            q = q_ref[0, :, h * 256:(h + 1) * 256]
            k = k_ref[0, 0:n_keys, h * 256:(h + 1) * 256]
            s = lax.dot_general(q, k, (((1,), (1,)), ((), ())), preferred_element_type=F32)
            m = jnp.max(s, axis=-1, keepdims=True)
            e = jnp.exp(s - m)
            l = jnp.sum(e, axis=-1, keepdims=True)
            o = _dot(e.astype(BF16), v_ref[0, 0:n_keys, h * 128:(h + 1) * 128])
            o_ref[0, :, h * 128:(h + 1) * 128] = (o / l).astype(BF16)

    S = k_ref.shape[1]
    if t0 < nct:
        @pl.when(t < nct)
        def _():
            attend(n_ctx)

        @pl.when(t >= nct)
        def _():
            attend(S)
    else:
        attend(S)


def _mla_attn(q, k, v, t0, nct, n_ctx):
    B, S, _ = q.shape
    nq = S // TM - t0
    H = MLA_HEADS
    return pl.pallas_call(
        functools.partial(_mla_attn_kernel, t0=t0, nct=nct, n_ctx=n_ctx),
        grid=(B, nq),
        in_specs=[
            pl.BlockSpec((1, TM, H * 256), lambda b, t: (b, t + t0, 0)),
            pl.BlockSpec((1, S, H * 256), lambda b, t: (b, 0, 0)),
            pl.BlockSpec((1, S, H * 128), lambda b, t: (b, 0, 0)),
        ],
        out_specs=pl.BlockSpec((1, TM, H * 128), lambda b, t: (b, t, 0)),
        out_shape=jax.ShapeDtypeStruct((B, nq * TM, H * 128), BF16),
        compiler_params=_cparams(("arbitrary", "arbitrary")),
        name="mla_attn",
    )(q, k, v)


NA_QR = 4
NA_KR = NA_WIN_R + NA_QR
NA_QB = NA_QR * GRID_W


def _na_kernel(p_ref, bias_ref, o_ref, *, n_ctx, rows, with_ctx):
    s_id = pl.program_id(1)
    W = GRID_W
    n_loc = NA_KR * W
    scale = NA_HEAD_DIM ** -0.5
    nq_ctx = n_ctx // NA_QB
    nblk = rows // NA_QR

    def heads_out(q, parts):
        outs = []
        for h in range(NA_HEADS):
            hs = slice(h * NA_HEAD_DIM, (h + 1) * NA_HEAD_DIM)
            qh = q[:, hs] * scale
            ss = []
            for kx, vx, bias in parts:
                s = lax.dot_general(qh, kx[:, hs], (((1,), (1,)), ((), ())), preferred_element_type=F32)
                if bias is not None:
                    s = s + bias(h)
                ss.append(s)
            m = ss[0].max(axis=-1, keepdims=True)
            for s in ss[1:]:
                m = jnp.maximum(m, s.max(axis=-1, keepdims=True))
            acc = 0.0
            l = 0.0
            for s, (kx, vx, bias) in zip(ss, parts):
                e = jnp.exp(s - m)
                l = l + jnp.sum(e, axis=-1, keepdims=True)
                acc = acc + _dot(e.astype(BF16), vx[:, hs])
            outs.append(acc / l)
        return jnp.concatenate(outs, axis=-1).astype(BF16)

    k_c = p_ref[0, 0:n_ctx, NA_DIM:2 * NA_DIM]
    v_c = p_ref[0, 0:n_ctx, 2 * NA_DIM:3 * NA_DIM]

    def lat_block(j):
        i0 = j * NA_QR
        k_start = jnp.clip(i0 - NA_WIN_R // 2, 0, rows - NA_KR)
        pat = jnp.where(j == 0, 0, jnp.where(j == nblk - 1, 2, 1))
        q0 = pl.multiple_of(n_ctx + i0 * W, NA_QB)
        k0 = pl.multiple_of(n_ctx + k_start * W, W)
        q = p_ref[0, pl.ds(q0, NA_QB), 0:NA_DIM]
        k_l = p_ref[0, pl.ds(k0, n_loc), NA_DIM:2 * NA_DIM]
        v_l = p_ref[0, pl.ds(k0, n_loc), 2 * NA_DIM:3 * NA_DIM]
        o_ref[0] = heads_out(q, [(k_l, v_l, lambda h: bias_ref[pat, h]), (k_c, v_c, None)])

    if with_ctx:
        @pl.when(s_id < nq_ctx)
        def _():
            q0 = pl.multiple_of(s_id * NA_QB, NA_QB)
            q = p_ref[0, pl.ds(q0, NA_QB), 0:NA_DIM]
            o_ref[0] = heads_out(q, [(k_c, v_c, None)])

        @pl.when(s_id >= nq_ctx)
        def _():
            lat_block(s_id - nq_ctx)
    else:
        lat_block(s_id)


def _na_attn(p_na, bias_tab, n_ctx, with_ctx):
    B, S, C = p_na.shape
    T = S - n_ctx
    rows = T // GRID_W
    assert rows % NA_QR == 0 and rows >= NA_KR and n_ctx % NA_QB == 0
    nsteps = rows // NA_QR + (n_ctx // NA_QB if with_ctx else 0)
    return pl.pallas_call(
        functools.partial(_na_kernel, n_ctx=n_ctx, rows=rows, with_ctx=with_ctx),
        grid=(B, nsteps),
        in_specs=[
            pl.BlockSpec((1, S, C), lambda b, s: (b, 0, 0)),
            pl.BlockSpec(bias_tab.shape, lambda b, s: (0, 0, 0, 0)),
        ],
        out_specs=pl.BlockSpec((1, NA_QB, NA_DIM), lambda b, s: (b, s, 0)),
        out_shape=jax.ShapeDtypeStruct((B, nsteps * NA_QB, NA_DIM), BF16),
        compiler_params=_cparams(("arbitrary", "arbitrary")),
        name="na_attn",
    )(p_na, bias_tab)


def _na_bias_table(rpb):
    col = jnp.arange(GRID_W)
    c_start = jnp.clip(col - NA_WIN_C // 2, 0, GRID_W - NA_WIN_C)
    in_win = (col[None, :] >= c_start[:, None]) & (col[None, :] < c_start[:, None] + NA_WIN_C)
    dc_idx = jnp.clip(col[None, :] - col[:, None] + NA_WIN_C - 1, 0, 2 * NA_WIN_C - 2)
    qa = jnp.arange(NA_QR)[:, None]
    kc = jnp.arange(NA_KR)[None, :]
    pats = []
    for pat in range(3):
        off = (NA_WIN_R // 2) * pat
        first = (0 * qa, qa, 0 * qa + NA_WIN_R // 2)[pat]
        row_ok = (kc >= first) & (kc < first + NA_WIN_R)
        dr_idx = jnp.clip(kc - qa - off + NA_WIN_R - 1, 0, 2 * NA_WIN_R - 2)
        bias = rpb[:, dr_idx][:, :, :, dc_idx]
        ok = row_ok[:, :, None, None] & in_win[None, None, :, :]
        bias = jnp.where(ok[None], bias, NEG_INF)
        bias = jnp.transpose(bias, (0, 1, 3, 2, 4))
        pats.append(bias.reshape(NA_HEADS, NA_QB, NA_KR * GRID_W))
    return jnp.stack(pats).astype(F32)


def _outproj_kernel(orw_ref, bonus_ref, g_ref, mla_ref, na_ref, x_ref, mod_ref, gnw_ref, gnb_ref, ones_ref,
                    wout_ref, ln1g_ref, ln1b_ref, rwh_ref, rwl_ref, rb_ref,
                    x1_ref, h_ref, lg_ref, *, dn_alpha):
    o = orw_ref[0] + orw_ref[1]
    inv_n = 1.0 / RW_HEAD_DIM
    mu = _group_sum(o, ones_ref) * inv_n
    dlt = o - mu
    var = _group_sum(dlt * dlt, ones_ref) * inv_n
    on = dlt * lax.rsqrt(var + RW_GN_EPS) * gnw_ref[...] + gnb_ref[...]
    rw_y = ((on + bonus_ref[0]) * g_ref[0]).astype(BF16)
    y = (_dot(rw_y, wout_ref[0:RW_DIM, :])
         + _dot(mla_ref[0], wout_ref[RW_DIM:RW_DIM + MLA_DIM, :])
         + _dot(na_ref[0], wout_ref[RW_DIM + MLA_DIM:, :]))
    m = mod_ref[0, 0]
    x1 = _ln(dn_alpha * x_ref[0] + m[2:3] * y) * ln1g_ref[...] + ln1b_ref[...]
    x1_ref[0] = x1
    h = _ln(x1) * (1.0 + m[4:5]) + m[3:4]
    h_ref[0] = _pack_bf16_pair(h)
    lg_ref[0] = _dot3(h, rwh_ref[...], rwl_ref[...]) + rb_ref[...]


def _out_proj(o_rw, bonus, g, mla_o, na_o, xa, mod6, prm, t0, nct, dn_alpha):
    B, S, D = xa.shape
    nt = S // TM - t0
    So = nt * TM
    kind = lambda t: (t + t0 >= nct).astype(jnp.int32)
    full = lambda shape: pl.BlockSpec(shape, lambda b, t: (0,) * len(shape))
    off = lambda C: pl.BlockSpec((1, TM, C), lambda b, t: (b, t + t0, 0))
    own = lambda C: pl.BlockSpec((1, TM, C), lambda b, t: (b, t, 0))
    return pl.pallas_call(
        functools.partial(_outproj_kernel, dn_alpha=dn_alpha),
        grid=(B, nt),
        in_specs=[
            pl.BlockSpec((2, TM, RW_DIM), lambda b, t: (0, t + t0, b)),
            off(RW_DIM), off(RW_DIM), own(MLA_DIM), own(NA_DIM), off(D),
            pl.BlockSpec((1, 1, 6, D), lambda b, t: (b, kind(t), 0, 0)),
            full((1, RW_DIM)), full((1, RW_DIM)), full((RW_DIM, RW_DIM)),
            full((D, D)), full((1, D)), full((1, D)),
            full((D, 128)), full((D, 128)), full((1, 128)),
        ],
        out_specs=[own(D), own(D // 2), own(128)],
        out_shape=[
            jax.ShapeDtypeStruct((B, So, D), F32),
            jax.ShapeDtypeStruct((B, So, D // 2), jnp.int32),
            jax.ShapeDtypeStruct((B, So, 128), F32),
        ],
        compiler_params=_cparams(("arbitrary", "arbitrary")),
        name="out_proj",
    )(o_rw, bonus, g, mla_o, na_o, xa, mod6, prm["gn_w"], prm["gn_b"], prm["ones"],
      prm["w_out"], prm["ln1_g"], prm["ln1_b"], prm["router_hi"], prm["router_lo"], prm["router_b"])


GU_CHUNK = 256


def _moe_kernel(be_ref, nu_ref, x_ref, wgu_ref, bgu_ref, wdn_ref, bdn_ref, perm_ref, y_ref,
                wgu_s, wdn_s, act_s):
    i = pl.program_id(0)
    F = wdn_ref.shape[2]
    n_chunks = 2 * F // GU_CHUNK
    half = GU_CHUNK // 2
    valid = i < nu_ref[0]
    e = be_ref[i]
    new_expert = jnp.logical_or(i == 0, e != be_ref[jnp.maximum(i - 1, 0)])

    @pl.when(jnp.logical_and(valid, new_expert))
    def _():
        for c in range(n_chunks):
            cs = slice(c * GU_CHUNK, (c + 1) * GU_CHUNK)
            wgu_s[:, cs] = _dot(wgu_ref[0, 0, :, cs].astype(BF16), perm_ref[...]).astype(BF16)
        wdn_s[...] = wdn_ref[0, 0].astype(BF16)

    @pl.when(valid)
    def _():
        gu = _dot(_unpack_bf16_pair(x_ref[...]).astype(BF16), wgu_s[...]) + bgu_ref[0]
        for c in range(n_chunks):
            glu = jnp.minimum(gu[:, c * GU_CHUNK:c * GU_CHUNK + half], SWIGLU_LIMIT)
            lin = jnp.clip(gu[:, c * GU_CHUNK + half:(c + 1) * GU_CHUNK], -SWIGLU_LIMIT, SWIGLU_LIMIT)
            act_s[:, c * half:(c + 1) * half] = (glu * _sigmoid(SWIGLU_ALPHA * glu) * (lin + 1.0)).astype(BF16)
        y = _dot(act_s[...], wdn_s[...]) + bdn_ref[0]
        y_ref[...] = _pack_bf16_pair(y)

    @pl.when(jnp.logical_not(valid))
    def _():
        y_ref[...] = jnp.zeros_like(y_ref)


def _moe_kernel_into(be_ref, nu_ref, x_ref, wgu_ref, bgu_ref, wdn_ref, bdn_ref, perm_ref, yprev_ref, y_ref,
                     wgu_s, wdn_s, act_s):
    del yprev_ref
    _moe_kernel(be_ref, nu_ref, x_ref, wgu_ref, bgu_ref, wdn_ref, bdn_ref, perm_ref, y_ref, wgu_s, wdn_s, act_s)


def _moe_ffn(x_part, block_e, n_used, w_gu, b_gu, w_dn, b_dn, layer, blk0, n_pad, y_prev):
    D = w_gu.shape[2]
    Dp = D // 2
    _, E, _, F2 = w_gu.shape
    F = F2 // 2
    n_blocks = x_part.shape[0] // MOE_BLK
    half = GU_CHUNK // 2
    src = np.concatenate([2 * np.arange(half), 2 * np.arange(half) + 1])
    perm = jnp.asarray(np.eye(GU_CHUNK, dtype=np.float32)[:, src], BF16)
    bgu_p = jnp.swapaxes(b_gu.reshape(E, F2 // GU_CHUNK, half, 2), 2, 3).reshape(E, 1, F2)
    grid_spec = pltpu.PrefetchScalarGridSpec(
        num_scalar_prefetch=2,
        grid=(n_blocks,),
        in_specs=[
            pl.BlockSpec((MOE_BLK, Dp), lambda i, be, nu: (i, 0)),
            pl.BlockSpec((1, 1, D, F2), lambda i, be, nu: (layer, be[i], 0, 0)),
            pl.BlockSpec((1, 1, F2), lambda i, be, nu: (be[i], 0, 0)),
            pl.BlockSpec((1, 1, F, D), lambda i, be, nu: (layer, be[i], 0, 0)),
            pl.BlockSpec((1, 1, D), lambda i, be, nu: (be[i], 0, 0)),
            pl.BlockSpec((GU_CHUNK, GU_CHUNK), lambda i, be, nu: (0, 0)),
        ] + ([] if y_prev is None else [pl.BlockSpec(memory_space=pl.ANY)]),
        out_specs=pl.BlockSpec((MOE_BLK, Dp), lambda i, be, nu: (i + blk0, 0)),
        scratch_shapes=[pltpu.VMEM((D, F2), BF16), pltpu.VMEM((F, D), BF16), pltpu.VMEM((MOE_BLK, F), BF16)],
    )
    args = (block_e, n_used, x_part, w_gu, bgu_p, w_dn, b_dn.reshape(E, 1, D), perm)
    return pl.pallas_call(
        _moe_kernel if y_prev is None else _moe_kernel_into,
        grid_spec=grid_spec,
        out_shape=jax.ShapeDtypeStruct((n_pad, Dp), jnp.int32),
        input_output_aliases={} if y_prev is None else {len(args): 0},
        compiler_params=_cparams(("arbitrary",)),
        name="moe_ffn",
    )(*args, *(() if y_prev is None else (y_prev,)))


SC_CORES = 2
SC_SUBCORES = 16
SC_WORKERS = SC_CORES * SC_SUBCORES
SC_ROWS = 64
SC_NBUF = 2


def _sc_gather(table, idx):
    V, W = table.shape
    n = idx.shape[0]
    assert n % (SC_WORKERS * SC_ROWS * SC_NBUF) == 0
    steps = n // (SC_WORKERS * SC_ROWS)
    idx3 = idx.reshape(SC_WORKERS, steps, SC_ROWS)
    mesh = plsc.VectorSubcoreMesh(core_axis_name="c", subcore_axis_name="s")

    @functools.partial(
        pl.kernel, mesh=mesh,
        out_type=jax.ShapeDtypeStruct((n, W), table.dtype),
        scratch_types=[pltpu.VMEM((steps, SC_ROWS), jnp.int32),
                       pltpu.VMEM((SC_NBUF, SC_ROWS, W), table.dtype),
                       pltpu.SemaphoreType.DMA((SC_NBUF,)),
                       pltpu.SemaphoreType.DMA((SC_NBUF,))],
        name="sc_row_gather",
    )
    def gather_kernel(table_hbm, idx_hbm, out_hbm, idx_v, rows_v, sem_g, sem_o):
        wid = lax.axis_index("s") * SC_CORES + lax.axis_index("c")
        step0 = wid * steps
        pltpu.sync_copy(idx_hbm.at[wid], idx_v)

        def gather(j, b):
            return pltpu.make_async_copy(table_hbm.at[idx_v.at[j + b]], rows_v.at[b], sem_g.at[b])

        def put(j, b):
            row0 = pl.multiple_of((step0 + j + b) * SC_ROWS, SC_ROWS)
            return pltpu.make_async_copy(rows_v.at[b], out_hbm.at[pl.ds(row0, SC_ROWS)], sem_o.at[b])

        @pl.loop(0, steps, step=SC_NBUF)
        def _(j):
            for b in range(SC_NBUF):
                gather(j, b).start()
            for b in range(SC_NBUF):
                gather(j, b).wait()
                put(j, b).start()
            for b in range(SC_NBUF):
                put(j, b).wait()

    return gather_kernel(table, idx3)


def _route(logits, n_tok):
    top_v, top_i = lax.top_k(logits, TOP_K)
    gates = jax.nn.softmax(top_v, axis=-1)
    n_assign = n_tok * TOP_K
    flat_e = top_i.reshape(-1)
    onehot = (flat_e[:, None] == jnp.arange(N_EXPERTS)[None, :]).astype(jnp.int32)
    rank = jnp.take_along_axis(jnp.cumsum(onehot, axis=0), flat_e[:, None], axis=1)[:, 0] - 1
    counts = jnp.sum(onehot, axis=0)
    padded = (counts + MOE_BLK - 1) // MOE_BLK * MOE_BLK
    ends_p = jnp.cumsum(padded)
    dest = (ends_p - padded)[flat_e] + rank
    n_blocks = -(-(n_assign + N_EXPERTS * (MOE_BLK - 1)) // MOE_BLK)
    n_pad = n_blocks * MOE_BLK
    blk_start = jnp.arange(n_blocks, dtype=jnp.int32) * MOE_BLK
    block_e = jnp.minimum(jnp.sum((ends_p[None, :] <= blk_start[:, None]).astype(jnp.int32), axis=1),
                          N_EXPERTS - 1).astype(jnp.int32)
    n_used = (ends_p[-1] // MOE_BLK).astype(jnp.int32).reshape(1)
    order = jnp.argsort(flat_e, stable=True).astype(jnp.int32)
    slot_e = jnp.repeat(block_e, MOE_BLK)
    j = jnp.arange(n_pad, dtype=jnp.int32) - (ends_p - padded)[slot_e]
    src = jnp.clip((jnp.cumsum(counts) - counts)[slot_e] + j, 0, n_assign - 1)
    tok_pad = jnp.where(j < counts[slot_e], order[src] // TOP_K, 0).astype(jnp.int32)
    return gates, dest.reshape(n_tok, TOP_K).astype(jnp.int32), tok_pad, block_e, n_used


def _final_kernel(x1_ref, yk_ref, gate_ref, mod_ref, g_ref, b_ref, o_ref, *, dn_alpha):
    D = x1_ref.shape[2]
    m = mod_ref[0, 0]
    gates = gate_ref[0]
    f = _unpack_bf16_pair(yk_ref[0, 0]) * gates[:, 0:1]
    for k in range(1, TOP_K):
        f = f + _unpack_bf16_pair(yk_ref[k, 0]) * gates[:, k:k + 1]
    o_ref[0] = _ln(dn_alpha * x1_ref[0] + m[5:6] * f) * g_ref[...] + b_ref[...]


def _final(x1, yk, gates, mod6, ln_g, ln_b, t0, nct, dn_alpha):
    B, So, D = x1.shape
    kind = lambda t: (t + t0 >= nct).astype(jnp.int32)
    blk = pl.BlockSpec((1, TM, D), lambda b, t: (b, t, 0))
    vec = pl.BlockSpec((1, D), lambda b, t: (0, 0))
    return pl.pallas_call(
        functools.partial(_final_kernel, dn_alpha=dn_alpha),
        grid=(B, So // TM),
        in_specs=[blk,
                  pl.BlockSpec((TOP_K, 1, TM, D // 2), lambda b, t: (0, b, t, 0)),
                  pl.BlockSpec((1, TM, TOP_K), lambda b, t: (b, t, 0)),
                  pl.BlockSpec((1, 1, 6, D), lambda b, t: (b, kind(t), 0, 0)), vec, vec],
        out_specs=blk,
        out_shape=jax.ShapeDtypeStruct((B, So, D), F32),
        compiler_params=_cparams(("arbitrary", "arbitrary")),
        name="ffn_residual",
    )(x1, yk, gates, mod6, ln_g, ln_b)


def _rope_tables(n_ctx, T):
    t = jnp.arange(T)
    row = (t // GRID_W).astype(F32)
    col = (t % GRID_W).astype(F32)
    n_freq = MLA_ROPE // 4
    inv = ROPE_BASE ** (-jnp.arange(n_freq, dtype=F32) / n_freq)
    ang = jnp.concatenate([row[:, None] * inv, col[:, None] * inv], axis=-1)
    cos, sin = jnp.cos(ang), jnp.sin(ang)
    z = jnp.zeros((T, 128 - MLA_ROPE), F32)
    cos128 = jnp.concatenate([cos, cos, z], axis=-1)
    sin128 = jnp.concatenate([-sin, sin, z], axis=-1)
    cos_c = jnp.concatenate([jnp.ones((n_ctx, MLA_ROPE), F32), jnp.zeros((n_ctx, 128 - MLA_ROPE), F32)], -1)
    sin_c = jnp.zeros((n_ctx, 128), F32)
    return jnp.concatenate([cos_c, cos128], 0), jnp.concatenate([sin_c, sin128], 0)


def _rope_slabs(w):
    ev, od = w[:, 0::2], w[:, 1::2]
    z = jnp.zeros((w.shape[0], 128 - MLA_ROPE), w.dtype)
    return jnp.concatenate([ev, od, z, od, ev, z], axis=-1)


def _blockdiag2(m):
    z = jnp.zeros_like(m[0])
    return jnp.concatenate([jnp.concatenate([m[0], z], 1), jnp.concatenate([z, m[1]], 1)], 0)


def kernel(x, c, ctx, c_ctx, ada_w, ada_b, w_in, rw_mu, rw_w0, rw_w2, rw_a0, rw_a2, rw_g2, rw_kk, rw_ka, rw_rk, rw_gn_w, rw_gn_b, mla_q_norm, mla_kv_norm, mla_w_uq, mla_w_ukv, na_rpb, w_out, ln1_g, ln1_b, router_w, router_b, w_gu, b_gu, w_dn, b_dn, ln2_g, ln2_b):
    B, T, D = x.shape
    n_ctx = ctx.shape[1]
    depth = ada_w.shape[0]
    S = n_ctx + T
    assert n_ctx % TM == 0 and T % TM == 0 and T % GRID_W == 0 and T // GRID_W >= NA_WIN_R
    nct = n_ctx // TM
    BH = B * RW_HEADS
    dn_alpha = (2 * depth) ** 0.25
    F = w_dn.shape[2]

    R = (B + 1 + 7) // 8 * 8
    cond = jnp.zeros((R, D), F32).at[:B].set(c).at[B].set(c_ctx)
    mod = _ada_mod(cond, ada_w, ada_b)
    mod_l = mod[:, :B].reshape(depth, B, 1, 6, D)
    mod_c = jnp.broadcast_to(mod[:, B].reshape(depth, 1, 1, 6, D), (depth, B, 1, 6, D))
    mod6 = jnp.concatenate([mod_c, mod_l], axis=2)

    cos128, sin128 = _rope_tables(n_ctx, T)
    ones_blk = jnp.kron(jnp.eye(RW_HEADS, dtype=F32), jnp.ones((RW_HEAD_DIM, RW_HEAD_DIM), F32)).astype(BF16)

    na_bias = [_na_bias_table(na_rpb[l]) for l in range(depth)]

    xa = jnp.concatenate([ctx, x], axis=1)
    for l in range(depth):
        need_ctx = l < depth - 1
        t0 = 0 if need_ctx else nct

        wi = w_in[l]
        c_m = RW_COLS
        w_in_ext = jnp.concatenate(
            [wi[:, :c_m + MLA_Q_LORA + MLA_KV_LORA], _rope_slabs(wi[:, c_m + MLA_Q_LORA + MLA_KV_LORA:c_m + MLA_COLS]),
             wi[:, c_m + MLA_COLS:]], axis=-1).astype(BF16)
        wuq = mla_w_uq[l].reshape(MLA_Q_LORA, MLA_HEADS, MLA_NOPE + MLA_ROPE)
        wuq_ext = jnp.concatenate(
            [jnp.concatenate([wuq[:, h, :MLA_NOPE], _rope_slabs(wuq[:, h, MLA_NOPE:])], -1) for h in range(MLA_HEADS)],
            axis=-1).astype(BF16)
        rw_prm = dict(
            mu=rw_mu[l],
            w0=rw_w0[l].reshape(1, 2 * RW_DIM), w2=_blockdiag2(rw_w2[l]).astype(BF16),
            a0=rw_a0[l].reshape(1, 2 * RW_DIM), a2=_blockdiag2(rw_a2[l]).astype(BF16),
            g2=rw_g2[l].astype(BF16), kk=rw_kk[l].reshape(1, RW_DIM), rk=rw_rk[l].reshape(1, RW_DIM),
            ones=ones_blk)
        mla_prm = dict(q_norm=mla_q_norm[l].reshape(1, -1), kv_norm=mla_kv_norm[l].reshape(1, -1),
                       w_uq=wuq_ext, w_ukv=mla_w_ukv[l].astype(BF16))
        rt = jnp.zeros((D, 128), F32).at[:, :N_EXPERTS].set(router_w[l])
        rt_hi = rt.astype(BF16)
        out_prm = dict(gn_w=rw_gn_w[l].reshape(1, -1), gn_b=rw_gn_b[l].reshape(1, -1), ones=ones_blk,
                       w_out=w_out[l].astype(BF16), ln1_g=ln1_g[l].reshape(1, -1), ln1_b=ln1_b[l].reshape(1, -1),
                       router_hi=rt_hi, router_lo=(rt - rt_hi.astype(F32)).astype(BF16),
                       router_b=jnp.zeros((1, 128), F32).at[0, :N_EXPERTS].set(router_b[l]))

        p_rw, p_mla, p_na = _in_proj(xa, mod6[l], w_in_ext, nct)
        feat, g_gate, bonus = _rw_features(p_rw, rw_prm, nct)
        ft = jnp.swapaxes(feat.reshape(S, BH, 8 * RW_HEAD_DIM), 1, 2).reshape(S, 8, RW_HEAD_DIM, BH)
        ka_t = jnp.tile(rw_ka[l].reshape(RW_HEADS, RW_HEAD_DIM).T[:, None, :], (1, B, 1)).reshape(RW_HEAD_DIM, BH)
        q, k, v = _mla_prep(p_mla, cos128, sin128, mla_prm)
        mla_o = _mla_attn(q, k, v, t0, nct, n_ctx)
        na_o = _na_attn(p_na, na_bias[l], n_ctx, need_ctx)

        o_scan = _rw_scan(ft, ka_t, n_ctx)
        o_rw = jnp.swapaxes(o_scan, 2, 3).reshape(2, S, B * RW_DIM)

        x1, h, logits = _out_proj(o_rw, bonus, g_gate, mla_o, na_o, xa, mod6[l], out_prm, t0, nct, dn_alpha)

        So = x1.shape[1]
        n_tok = B * So
        gates, dest, tok_pad, block_e, n_used = _route(logits.reshape(n_tok, 128)[:, :N_EXPERTS], n_tok)
        n_pad = tok_pad.shape[0]
        n_blocks = n_pad // MOE_BLK
        parts = MOE_PARTS if n_blocks % MOE_PARTS == 0 else 1
        pb = n_blocks // parts
        h2 = h.reshape(n_tok, D // 2)
        y_sorted = None
        for c in range(parts):
            x_part = _sc_gather(h2, tok_pad[c * pb * MOE_BLK:(c + 1) * pb * MOE_BLK])
            y_sorted = _moe_ffn(x_part, block_e[c * pb:(c + 1) * pb], jnp.clip(n_used - c * pb, 0, pb),
                                w_gu, b_gu[l], w_dn, b_dn[l], l, c * pb, n_pad, y_sorted)
        yk = _sc_gather(y_sorted, dest.T.reshape(-1)).reshape(TOP_K, B, So, D // 2)

        xa = _final(x1, yk, gates.reshape(B, So, TOP_K), mod6[l], ln2_g[l].reshape(1, -1), ln2_b[l].reshape(1, -1),
                    t0, nct, dn_alpha)
    return xa
```

```python
import functools

import numpy as np
import jax
import jax.numpy as jnp
from jax import lax
from jax.experimental import pallas as pl
from jax.experimental.pallas import tpu as pltpu
from jax.experimental.pallas import tpu_sc as plsc

F32 = jnp.float32
BF16 = jnp.bfloat16

GRID_W = 64
RW_HEAD_DIM = 64
RW_HEADS = 4
RW_DIM = RW_HEADS * RW_HEAD_DIM
RW_LORA = 64
RW_G_LORA = 128
RW_GN_EPS = 64e-5
RW_COLS = 3 * RW_DIM + 4 * RW_LORA + RW_G_LORA
MLA_HEADS = 4
MLA_NOPE = 128
MLA_ROPE = 64
MLA_V = 128
MLA_Q_LORA = 256
MLA_KV_LORA = 128
MLA_COLS = MLA_Q_LORA + MLA_KV_LORA + MLA_ROPE
MLA_COLS_EXT = MLA_Q_LORA + MLA_KV_LORA + 256
MLA_DIM = MLA_HEADS * MLA_V
NA_HEADS = 4
NA_HEAD_DIM = 64
NA_DIM = NA_HEADS * NA_HEAD_DIM
NA_WIN_R = 8
NA_WIN_C = 16
NA_COLS = 3 * NA_DIM
ROPE_BASE = 10000.0
N_EXPERTS = 32
TOP_K = 4
SWIGLU_ALPHA = 1.702
SWIGLU_LIMIT = 7.0
NEG_INF = -1e30

TM = 256
SCAN_TB = 16
MOE_BLK = 512
MOE_PARTS = 4
VMEM_LIMIT = 56 * 1024 * 1024


def _cparams(sem):
    return pltpu.CompilerParams(dimension_semantics=sem, vmem_limit_bytes=VMEM_LIMIT)


def _ln(x, eps=1e-5):
    mu = jnp.mean(x, axis=-1, keepdims=True)
    d = x - mu
    var = jnp.mean(d * d, axis=-1, keepdims=True)
    return d * lax.rsqrt(var + eps)


def _dot(a, b):
    return jnp.dot(a, b, preferred_element_type=F32)


def _split(a):
    hi = a.astype(BF16)
    lo = (a - hi.astype(F32)).astype(BF16)
    return hi, lo


def _dot_hl(a, b_bf16):
    hi, lo = _split(a)
    return _dot(hi, b_bf16) + _dot(lo, b_bf16)


def _dot3(a, b_hi, b_lo):
    hi, lo = _split(a)
    return _dot(hi, b_hi) + _dot(lo, b_hi) + _dot(hi, b_lo)


def _sigmoid(x):
    return 1.0 / (1.0 + jnp.exp(-x))


def _pack_bf16_pair(x):
    w = x.shape[1] // 2
    u = pltpu.bitcast(x.astype(BF16).astype(F32), jnp.uint32)
    lo = lax.shift_right_logical(u[:, :w], jnp.uint32(16))
    hi = lax.bitwise_and(u[:, w:], jnp.uint32(0xFFFF0000))
    return pltpu.bitcast(lax.bitwise_or(lo, hi), jnp.int32)


def _unpack_bf16_pair(p):
    u = pltpu.bitcast(p, jnp.uint32)
    lo = pltpu.bitcast(lax.shift_left(u, jnp.uint32(16)), F32)
    hi = pltpu.bitcast(lax.bitwise_and(u, jnp.uint32(0xFFFF0000)), F32)
    return jnp.concatenate([lo, hi], axis=-1)


def _ada_kernel(cond_ref, w_ref, b_ref, o_ref):
    c = cond_ref[...]
    s = c * _sigmoid(c)
    w = w_ref[0]
    w_hi, w_lo = _split(w)
    o_ref[0] = _dot3(s, w_hi, w_lo) + b_ref[0]


def _ada_mod(cond, ada_w, ada_b):
    L, D, N = ada_w.shape
    R = cond.shape[0]
    tn = 512
    return pl.pallas_call(
        _ada_kernel,
        grid=(L, N // tn),
        in_specs=[
            pl.BlockSpec((R, D), lambda l, j: (0, 0)),
            pl.BlockSpec((1, D, tn), lambda l, j: (l, 0, j)),
            pl.BlockSpec((1, 1, tn), lambda l, j: (l, 0, j)),
        ],
        out_specs=pl.BlockSpec((1, R, tn), lambda l, j: (l, 0, j)),
        out_shape=jax.ShapeDtypeStruct((L, R, N), F32),
        compiler_params=_cparams(("arbitrary", "arbitrary")),
        name="ada_mod",
    )(cond, ada_w, ada_b.reshape(L, 1, N))


def _win_kernel(x_ref, mod_ref, w_ref, prw_ref, pmla_ref, pna_ref):
    x = x_ref[0]
    m = mod_ref[0, 0]
    xm = _ln(x) * (1.0 + m[1:2]) + m[0:1]
    p = _dot(xm.astype(BF16), w_ref[...])
    prw_ref[0] = p[:, :RW_COLS]
    pmla_ref[0] = p[:, RW_COLS:RW_COLS + MLA_COLS_EXT]
    pna_ref[0] = p[:, RW_COLS + MLA_COLS_EXT:].astype(BF16)


def _in_proj(xa, mod6, w_in_ext, nct):
    B, S, D = xa.shape
    NC = w_in_ext.shape[1]
    kind = lambda t: (t >= nct).astype(jnp.int32)
    return pl.pallas_call(
        _win_kernel,
        grid=(B, S // TM),
        in_specs=[
            pl.BlockSpec((1, TM, D), lambda b, t: (b, t, 0)),
            pl.BlockSpec((1, 1, 6, D), lambda b, t: (b, kind(t), 0, 0)),
            pl.BlockSpec((D, NC), lambda b, t: (0, 0)),
        ],
        out_specs=[
            pl.BlockSpec((1, TM, RW_COLS), lambda b, t: (b, t, 0)),
            pl.BlockSpec((1, TM, MLA_COLS_EXT), lambda b, t: (b, t, 0)),
            pl.BlockSpec((1, TM, NA_COLS), lambda b, t: (b, t, 0)),
        ],
        out_shape=[
            jax.ShapeDtypeStruct((B, S, RW_COLS), F32),
            jax.ShapeDtypeStruct((B, S, MLA_COLS_EXT), F32),
            jax.ShapeDtypeStruct((B, S, NA_COLS), BF16),
        ],
        compiler_params=_cparams(("arbitrary", "arbitrary")),
        name="in_proj",
    )(xa, mod6, w_in_ext)


def _group_sum(x, ones_ref):
    return _dot_hl(x, ones_ref[...])


def _rwfeat_kernel(p_ref, pp_ref, pn_ref, mu_ref, w0_ref, w2_ref, a0_ref, a2_ref, g2_ref,
                   kk_ref, rk_ref, ones_ref, f_ref, g_ref, bonus_ref, *, nct, nt):
    t = pl.program_id(1)
    p = p_ref[0]
    first = jnp.logical_or(t == 0, t == nct)
    last = jnp.logical_or(t == nct - 1, t == nt - 1)
    prev_row = jnp.where(first, 0.0, pp_ref[0, 7:8, :])
    next_row = jnp.where(last, 0.0, pn_ref[0, 0:1, :])
    rows = lax.broadcasted_iota(jnp.int32, p.shape, 0)
    prev = jnp.where(rows == 0, prev_row, pltpu.roll(p, 1, axis=0))
    nxt = jnp.where(rows == TM - 1, next_row, pltpu.roll(p, TM - 1, axis=0))
    mu = mu_ref[...]
    xs = p + mu[0:1] * (prev - p) + mu[1:2] * (nxt - p)

    D3 = 3 * RW_DIM
    r = xs[:, 0:RW_DIM]
    k = xs[:, RW_DIM:2 * RW_DIM]
    v = xs[:, 2 * RW_DIM:D3]
    w_lo = xs[:, D3:D3 + 2 * RW_LORA]
    a_lo = xs[:, D3 + 2 * RW_LORA:D3 + 4 * RW_LORA]
    g_pre = xs[:, D3 + 4 * RW_LORA:]

    lw = _dot(jnp.tanh(w_lo).astype(BF16), w2_ref[...]) + w0_ref[...]
    logw = jnp.minimum(lw, 0.0) - jnp.log(1.0 + jnp.exp(-jnp.abs(lw))) - 0.5
    decay = jnp.exp(-jnp.exp(logw))
    a = _sigmoid(_dot(a_lo.astype(BF16), a2_ref[...]) + a0_ref[...])

    kkr = k * kk_ref[...]
    kk = kkr * lax.rsqrt(_group_sum(kkr * kkr, ones_ref) + 1e-12)
    g_ref[0] = _dot(_sigmoid(g_pre).astype(BF16), g2_ref[...])
    bonus_ref[0] = _group_sum(r * k * rk_ref[...], ones_ref) * v

    comps = (r, k, v, kk, decay[:, :RW_DIM], a[:, :RW_DIM], decay[:, RW_DIM:], a[:, RW_DIM:])
    N = RW_HEAD_DIM
    for h in range(RW_HEADS):
        for ci, comp in enumerate(comps):
            col = (h * len(comps) + ci) * N
            f_ref[:, col:col + N] = comp[:, h * N:(h + 1) * N]


def _rw_features(p_rw, prm, nct):
    B, S, C = p_rw.shape
    nt = S // TM
    hb = TM // 8
    last_hb = S // 8 - 1
    full = lambda shape: pl.BlockSpec(shape, lambda b, t: (0,) * len(shape))
    return pl.pallas_call(
        functools.partial(_rwfeat_kernel, nct=nct, nt=nt),
        grid=(B, nt),
        in_specs=[
            pl.BlockSpec((1, TM, C), lambda b, t: (b, t, 0)),
            pl.BlockSpec((1, 8, C), lambda b, t: (b, jnp.maximum(t * hb - 1, 0), 0)),
            pl.BlockSpec((1, 8, C), lambda b, t: (b, jnp.minimum((t + 1) * hb, last_hb), 0)),
            full((2, C)),
            full((1, 2 * RW_DIM)), full((2 * RW_LORA, 2 * RW_DIM)),
            full((1, 2 * RW_DIM)), full((2 * RW_LORA, 2 * RW_DIM)),
            full((RW_G_LORA, RW_DIM)),
            full((1, RW_DIM)), full((1, RW_DIM)),
            full((RW_DIM, RW_DIM)),
        ],
        out_specs=[
            pl.BlockSpec((TM, 8 * RW_DIM), lambda b, t: (t, b)),
            pl.BlockSpec((1, TM, RW_DIM), lambda b, t: (b, t, 0)),
            pl.BlockSpec((1, TM, RW_DIM), lambda b, t: (b, t, 0)),
        ],
        out_shape=[
            jax.ShapeDtypeStruct((S, B * 8 * RW_DIM), F32),
            jax.ShapeDtypeStruct((B, S, RW_DIM), F32),
            jax.ShapeDtypeStruct((B, S, RW_DIM), F32),
        ],
        compiler_params=_cparams(("arbitrary", "arbitrary")),
        name="rw_features",
    )(p_rw, p_rw, p_rw, prm["mu"], prm["w0"], prm["w2"], prm["a0"], prm["a2"], prm["g2"],
      prm["kk"], prm["rk"], prm["ones"])


def _scan_kernel(fs_ref, fd_ref, ka_ref, after_ref, o_ref, s_ref, tmp_ref, *, tb):
    del after_ref
    d = pl.program_id(0)
    g = pl.program_id(1)
    N = RW_HEAD_DIM

    @pl.when(g == 0)
    def _():
        s_ref[...] = jnp.zeros_like(s_ref)

    ka = ka_ref[...]

    def step(i, carry):
        tt = jnp.where(d == 0, i, tb - 1 - i)
        r = fs_ref[tt, 0]
        k = fs_ref[tt, 1]
        v = fs_ref[tt, 2]
        kk = fs_ref[tt, 3]
        w = fd_ref[tt, 0]
        a = fd_ref[tt, 1]
        b = a * kk
        kd = k * (1.0 + (a - 1.0) * ka)
        wr = w * r
        br = jnp.sum(b * r, axis=0, keepdims=True)
        kr = jnp.sum(kd * r, axis=0, keepdims=True)
        tmp_ref[0] = wr
        tmp_ref[1] = b
        tmp_ref[2] = kd
        sa = [jnp.zeros_like(v), jnp.zeros_like(v)]
        op = [jnp.zeros_like(v), jnp.zeros_like(v)]
        for j in range(N):
            sk = s_ref[j]
            sa[j % 2] = sa[j % 2] + sk * fs_ref[tt, 3, pl.ds(j, 1), :]
            op[j % 2] = op[j % 2] + sk * tmp_ref[0, pl.ds(j, 1), :]
        sa = sa[0] + sa[1]
        op = op[0] + op[1]
        for j in range(N):
            s_ref[j] = (s_ref[j] * fd_ref[tt, 0, pl.ds(j, 1), :]
                        - sa * tmp_ref[1, pl.ds(j, 1), :]
                        + v * tmp_ref[2, pl.ds(j, 1), :])
        o_ref[0, tt] = op - sa * br + v * kr
        return carry

    lax.fori_loop(0, tb, step, 0)


def _rw_scan(ft, ka_t, n_ctx, after):
    S, _, N, BH = ft.shape
    tb = SCAN_TB
    nb = S // tb
    ncb = n_ctx // tb

    def tblk(d, g):
        bwd = jnp.where(g < ncb, ncb - 1 - g, nb - 1 - g + ncb)
        return jnp.where(d == 0, g, bwd)

    return pl.pallas_call(
        functools.partial(_scan_kernel, tb=tb),
        grid=(2, nb),
        in_specs=[
            pl.BlockSpec((tb, 4, N, BH), lambda d, g: (tblk(d, g), 0, 0, 0)),
            pl.BlockSpec((tb, 2, N, BH), lambda d, g: (tblk(d, g), 2 + d, 0, 0)),
            pl.BlockSpec((N, BH), lambda d, g: (0, 0)),
            pl.BlockSpec(memory_space=pl.ANY),
        ],
        out_specs=pl.BlockSpec((1, tb, N, BH), lambda d, g: (d, tblk(d, g), 0, 0)),
        out_shape=jax.ShapeDtypeStruct((2, S, N, BH), F32),
        scratch_shapes=[pltpu.VMEM((N, N, BH), F32), pltpu.VMEM((3, N, BH), F32)],
        compiler_params=_cparams(("arbitrary", "arbitrary")),
        name="rw_scan",
    )(ft, ft, ka_t, after)


def _mlaprep_kernel(p_ref, cos_ref, sin_ref, qn_ref, kvn_ref, wuq_ref, wukv_ref, q_ref, k_ref, v_ref):
    p = p_ref[0]
    cos = cos_ref[...]
    sin = sin_ref[...]
    scale = (MLA_NOPE + MLA_ROPE) ** -0.5

    def rms(x, g):
        return x * lax.rsqrt(jnp.mean(x * x, axis=-1, keepdims=True) + 1e-6) * g

    q = _dot(rms(p[:, :MLA_Q_LORA], qn_ref[...]).astype(BF16), wuq_ref[...])
    kv = _dot(rms(p[:, MLA_Q_LORA:MLA_Q_LORA + MLA_KV_LORA], kvn_ref[...]).astype(BF16), wukv_ref[...])
    c0 = MLA_Q_LORA + MLA_KV_LORA
    kr = (p[:, c0:c0 + 128] * cos + p[:, c0 + 128:c0 + 256] * sin).astype(BF16)
    for h in range(MLA_HEADS):
        qb = h * 384
        q_ref[0, :, h * 256:h * 256 + 128] = (q[:, qb:qb + 128] * scale).astype(BF16)
        q_ref[0, :, h * 256 + 128:h * 256 + 256] = (
            (q[:, qb + 128:qb + 256] * cos + q[:, qb + 256:qb + 384] * sin) * scale).astype(BF16)
        k_ref[0, :, h * 256:h * 256 + 128] = kv[:, h * 256:h * 256 + 128].astype(BF16)
        k_ref[0, :, h * 256 + 128:h * 256 + 256] = kr
        v_ref[0, :, h * 128:(h + 1) * 128] = kv[:, h * 256 + 128:h * 256 + 256].astype(BF16)


def _mla_prep(p_mla, cos128, sin128, prm):
    B, S, C = p_mla.shape
    full = lambda shape: pl.BlockSpec(shape, lambda b, t: (0,) * len(shape))
    H = MLA_HEADS
    return pl.pallas_call(
        _mlaprep_kernel,
        grid=(B, S // TM),
        in_specs=[
            pl.BlockSpec((1, TM, C), lambda b, t: (b, t, 0)),
            pl.BlockSpec((TM, 128), lambda b, t: (t, 0)),
            pl.BlockSpec((TM, 128), lambda b, t: (t, 0)),
            full((1, MLA_Q_LORA)), full((1, MLA_KV_LORA)),
            full((MLA_Q_LORA, H * 384)), full((MLA_KV_LORA, H * 256)),
        ],
        out_specs=[
            pl.BlockSpec((1, TM, H * 256), lambda b, t: (b, t, 0)),
            pl.BlockSpec((1, TM, H * 256), lambda b, t: (b, t, 0)),
            pl.BlockSpec((1, TM, H * 128), lambda b, t: (b, t, 0)),
        ],
        out_shape=[
            jax.ShapeDtypeStruct((B, S, H * 256), BF16),
            jax.ShapeDtypeStruct((B, S, H * 256), BF16),
            jax.ShapeDtypeStruct((B, S, H * 128), BF16),
        ],
        compiler_params=_cparams(("arbitrary", "arbitrary")),
        name="mla_prep",
    )(p_mla, cos128, sin128, prm["q_norm"], prm["kv_norm"], prm["w_uq"], prm["w_ukv"])


def _mla_attn_kernel(q_ref, k_ref, v_ref, o_ref, *, t0, nct, n_ctx):
    t = pl.program_id(1) + t0

    def attend(n_keys):
        for h in range(MLA_HEADS):
            q = q_ref[0, :, h * 256:(h + 1) * 256]
            k = k_ref[0, 0:n_keys, h * 256:(h + 1) * 256]
            s = lax.dot_general(q, k, (((1,), (1,)), ((), ())), preferred_element_type=F32)
            m = jnp.max(s, axis=-1, keepdims=True)
            e = jnp.exp(s - m)
            l = jnp.sum(e, axis=-1, keepdims=True)
            o = _dot(e.astype(BF16), v_ref[0, 0:n_keys, h * 128:(h + 1) * 128])
            o_ref[0, :, h * 128:(h + 1) * 128] = (o / l).astype(BF16)

    S = k_ref.shape[1]
    if t0 < nct:
        @pl.when(t < nct)
        def _():
            attend(n_ctx)

        @pl.when(t >= nct)
        def _():
            attend(S)
    else:
        attend(S)


def _mla_attn(q, k, v, t0, nct, n_ctx):
    B, S, _ = q.shape
    nq = S // TM - t0
    H = MLA_HEADS
    return pl.pallas_call(
        functools.partial(_mla_attn_kernel, t0=t0, nct=nct, n_ctx=n_ctx),
        grid=(B, nq),
        in_specs=[
            pl.BlockSpec((1, TM, H * 256), lambda b, t: (b, t + t0, 0)),
            pl.BlockSpec((1, S, H * 256), lambda b, t: (b, 0, 0)),
            pl.BlockSpec((1, S, H * 128), lambda b, t: (b, 0, 0)),
        ],
        out_specs=pl.BlockSpec((1, TM, H * 128), lambda b, t: (b, t, 0)),
        out_shape=jax.ShapeDtypeStruct((B, nq * TM, H * 128), BF16),
        compiler_params=_cparams(("arbitrary", "arbitrary")),
        name="mla_attn",
    )(q, k, v)


NA_QR = 4
NA_KR = NA_WIN_R + NA_QR
NA_QB = NA_QR * GRID_W


def _na_kernel(p_ref, bias_ref, o_ref, *, n_ctx, rows, with_ctx):
    s_id = pl.program_id(1)
    W = GRID_W
    n_loc = NA_KR * W
    scale = NA_HEAD_DIM ** -0.5
    nq_ctx = n_ctx // NA_QB
    nblk = rows // NA_QR

    def heads_out(q, parts):
        outs = []
        for h in range(NA_HEADS):
            hs = slice(h * NA_HEAD_DIM, (h + 1) * NA_HEAD_DIM)
            qh = q[:, hs] * scale
            ss = []
            for kx, vx, bias in parts:
                s = lax.dot_general(qh, kx[:, hs], (((1,), (1,)), ((), ())), preferred_element_type=F32)
                if bias is not None:
                    s = s + bias(h)
                ss.append(s)
            m = ss[0].max(axis=-1, keepdims=True)
            for s in ss[1:]:
                m = jnp.maximum(m, s.max(axis=-1, keepdims=True))
            acc = 0.0
            l = 0.0
            for s, (kx, vx, bias) in zip(ss, parts):
                e = jnp.exp(s - m)
                l = l + jnp.sum(e, axis=-1, keepdims=True)
                acc = acc + _dot(e.astype(BF16), vx[:, hs])
            outs.append(acc / l)
        return jnp.concatenate(outs, axis=-1).astype(BF16)

    k_c = p_ref[0, 0:n_ctx, NA_DIM:2 * NA_DIM]
    v_c = p_ref[0, 0:n_ctx, 2 * NA_DIM:3 * NA_DIM]

    def lat_block(j):
        i0 = j * NA_QR
        k_start = jnp.clip(i0 - NA_WIN_R // 2, 0, rows - NA_KR)
        pat = jnp.where(j == 0, 0, jnp.where(j == nblk - 1, 2, 1))
        q0 = pl.multiple_of(n_ctx + i0 * W, NA_QB)
        k0 = pl.multiple_of(n_ctx + k_start * W, W)
        q = p_ref[0, pl.ds(q0, NA_QB), 0:NA_DIM]
        k_l = p_ref[0, pl.ds(k0, n_loc), NA_DIM:2 * NA_DIM]
        v_l = p_ref[0, pl.ds(k0, n_loc), 2 * NA_DIM:3 * NA_DIM]
        o_ref[0] = heads_out(q, [(k_l, v_l, lambda h: bias_ref[pat, h]), (k_c, v_c, None)])

    if with_ctx:
        @pl.when(s_id < nq_ctx)
        def _():
            q0 = pl.multiple_of(s_id * NA_QB, NA_QB)
            q = p_ref[0, pl.ds(q0, NA_QB), 0:NA_DIM]
            o_ref[0] = heads_out(q, [(k_c, v_c, None)])

        @pl.when(s_id >= nq_ctx)
        def _():
            lat_block(s_id - nq_ctx)
    else:
        lat_block(s_id)


def _na_attn(p_na, bias_tab, n_ctx, with_ctx):
    B, S, C = p_na.shape
    T = S - n_ctx
    rows = T // GRID_W
    assert rows % NA_QR == 0 and rows >= NA_KR and n_ctx % NA_QB == 0
    nsteps = rows // NA_QR + (n_ctx // NA_QB if with_ctx else 0)
    return pl.pallas_call(
        functools.partial(_na_kernel, n_ctx=n_ctx, rows=rows, with_ctx=with_ctx),
        grid=(B, nsteps),
        in_specs=[
            pl.BlockSpec((1, S, C), lambda b, s: (b, 0, 0)),
            pl.BlockSpec(bias_tab.shape, lambda b, s: (0, 0, 0, 0)),
        ],
        out_specs=pl.BlockSpec((1, NA_QB, NA_DIM), lambda b, s: (b, s, 0)),
        out_shape=jax.ShapeDtypeStruct((B, nsteps * NA_QB, NA_DIM), BF16),
        compiler_params=_cparams(("arbitrary", "arbitrary")),
        name="na_attn",
    )(p_na, bias_tab)


def _na_bias_table(rpb):
    col = np.arange(GRID_W)
    c_start = np.clip(col - NA_WIN_C // 2, 0, GRID_W - NA_WIN_C)
    in_win = (col[None, :] >= c_start[:, None]) & (col[None, :] < c_start[:, None] + NA_WIN_C)
    dc_idx = np.clip(col[None, :] - col[:, None] + NA_WIN_C - 1, 0, 2 * NA_WIN_C - 2)
    qa = np.arange(NA_QR)[:, None]
    kc = np.arange(NA_KR)[None, :]
    row_ok, dr_idx = [], []
    for pat in range(3):
        off = (NA_WIN_R // 2) * pat
        first = (0 * qa, qa, 0 * qa + NA_WIN_R // 2)[pat]
        row_ok.append((kc >= first) & (kc < first + NA_WIN_R))
        dr_idx.append(np.clip(kc - qa - off + NA_WIN_R - 1, 0, 2 * NA_WIN_R - 2))
    sel_c = np.eye(2 * NA_WIN_C - 1, dtype=np.float32)[dc_idx]
    sel_r = np.eye(2 * NA_WIN_R - 1, dtype=np.float32)[np.stack(dr_idx)]
    hp = lax.Precision.HIGHEST
    t = jnp.einsum('hrs,qks->hrqk', rpb, sel_c, precision=hp)
    bias = jnp.einsum('pacr,hrqk->phaqck', sel_r, t, precision=hp)
    ok = jnp.asarray(np.stack(row_ok))[:, None, :, None, :, None] & jnp.asarray(in_win)[None, None, None, :, None, :]
    bias = jnp.where(ok, bias, NEG_INF)
    return bias.reshape(3, NA_HEADS, NA_QB, NA_KR * GRID_W).astype(F32)


def _outproj_kernel(orw_ref, bonus_ref, g_ref, mla_ref, na_ref, x_ref, mod_ref, gnw_ref, gnb_ref, ones_ref,
                    wout_ref, ln1g_ref, ln1b_ref, rwh_ref, rwl_ref, rb_ref,
                    x1_ref, h_ref, lg_ref, *, dn_alpha):
    o = orw_ref[0] + orw_ref[1]
    inv_n = 1.0 / RW_HEAD_DIM
    mu = _group_sum(o, ones_ref) * inv_n
    dlt = o - mu
    var = _group_sum(dlt * dlt, ones_ref) * inv_n
    on = dlt * lax.rsqrt(var + RW_GN_EPS) * gnw_ref[...] + gnb_ref[...]
    rw_y = ((on + bonus_ref[0]) * g_ref[0]).astype(BF16)
    y = (_dot(rw_y, wout_ref[0:RW_DIM, :])
         + _dot(mla_ref[0], wout_ref[RW_DIM:RW_DIM + MLA_DIM, :])
         + _dot(na_ref[0], wout_ref[RW_DIM + MLA_DIM:, :]))
    m = mod_ref[0, 0]
    x1 = _ln(dn_alpha * x_ref[0] + m[2:3] * y) * ln1g_ref[...] + ln1b_ref[...]
    x1_ref[0] = x1
    h = _ln(x1) * (1.0 + m[4:5]) + m[3:4]
    h_ref[0] = _pack_bf16_pair(h)
    lg_ref[0] = _dot3(h, rwh_ref[...], rwl_ref[...]) + rb_ref[...]


def _out_proj(o_rw, bonus, g, mla_o, na_o, xa, mod6, prm, t0, nct, dn_alpha):
    B, S, D = xa.shape
    nt = S // TM - t0
    So = nt * TM
    kind = lambda t: (t + t0 >= nct).astype(jnp.int32)
    full = lambda shape: pl.BlockSpec(shape, lambda b, t: (0,) * len(shape))
    off = lambda C: pl.BlockSpec((1, TM, C), lambda b, t: (b, t + t0, 0))
    own = lambda C: pl.BlockSpec((1, TM, C), lambda b, t: (b, t, 0))
    return pl.pallas_call(
        functools.partial(_outproj_kernel, dn_alpha=dn_alpha),
        grid=(B, nt),
        in_specs=[
            pl.BlockSpec((2, TM, RW_DIM), lambda b, t: (0, t + t0, b)),
            off(RW_DIM), off(RW_DIM), own(MLA_DIM), own(NA_DIM), off(D),
            pl.BlockSpec((1, 1, 6, D), lambda b, t: (b, kind(t), 0, 0)),
            full((1, RW_DIM)), full((1, RW_DIM)), full((RW_DIM, RW_DIM)),
            full((D, D)), full((1, D)), full((1, D)),
            full((D, 128)), full((D, 128)), full((1, 128)),
        ],
        out_specs=[own(D), own(D // 2), own(128)],
        out_shape=[
            jax.ShapeDtypeStruct((B, So, D), F32),
            jax.ShapeDtypeStruct((B, So, D // 2), jnp.int32),
            jax.ShapeDtypeStruct((B, So, 128), F32),
        ],
        compiler_params=_cparams(("arbitrary", "arbitrary")),
        name="out_proj",
    )(o_rw, bonus, g, mla_o, na_o, xa, mod6, prm["gn_w"], prm["gn_b"], prm["ones"],
      prm["w_out"], prm["ln1_g"], prm["ln1_b"], prm["router_hi"], prm["router_lo"], prm["router_b"])


GU_CHUNK = 256


def _moe_kernel(be_ref, nu_ref, x_ref, wgu_ref, bgu_ref, wdn_ref, bdn_ref, perm_ref, y_ref,
                wgu_s, wdn_s, act_s):
    i = pl.program_id(0)
    F = wdn_ref.shape[2]
    n_chunks = 2 * F // GU_CHUNK
    half = GU_CHUNK // 2
    valid = i < nu_ref[0]
    e = be_ref[i]
    new_expert = jnp.logical_or(i == 0, e != be_ref[jnp.maximum(i - 1, 0)])

    @pl.when(jnp.logical_and(valid, new_expert))
    def _():
        for c in range(n_chunks):
            cs = slice(c * GU_CHUNK, (c + 1) * GU_CHUNK)
            wgu_s[:, cs] = _dot(wgu_ref[0, 0, :, cs].astype(BF16), perm_ref[...]).astype(BF16)
        wdn_s[...] = wdn_ref[0, 0].astype(BF16)

    @pl.when(valid)
    def _():
        gu = _dot(_unpack_bf16_pair(x_ref[...]).astype(BF16), wgu_s[...]) + bgu_ref[0]
        for c in range(n_chunks):
            glu = jnp.minimum(gu[:, c * GU_CHUNK:c * GU_CHUNK + half], SWIGLU_LIMIT)
            lin = jnp.clip(gu[:, c * GU_CHUNK + half:(c + 1) * GU_CHUNK], -SWIGLU_LIMIT, SWIGLU_LIMIT)
            act_s[:, c * half:(c + 1) * half] = (glu * _sigmoid(SWIGLU_ALPHA * glu) * (lin + 1.0)).astype(BF16)
        y = _dot(act_s[...], wdn_s[...]) + bdn_ref[0]
        y_ref[...] = _pack_bf16_pair(y)

    @pl.when(jnp.logical_not(valid))
    def _():
        y_ref[...] = jnp.zeros_like(y_ref)


def _moe_kernel_into(be_ref, nu_ref, x_ref, wgu_ref, bgu_ref, wdn_ref, bdn_ref, perm_ref, yprev_ref, y_ref,
                     wgu_s, wdn_s, act_s):
    del yprev_ref
    _moe_kernel(be_ref, nu_ref, x_ref, wgu_ref, bgu_ref, wdn_ref, bdn_ref, perm_ref, y_ref, wgu_s, wdn_s, act_s)


def _moe_ffn(x_part, block_e, n_used, w_gu, b_gu, w_dn, b_dn, layer, blk0, n_pad, y_prev):
    D = w_gu.shape[2]
    Dp = D // 2
    _, E, _, F2 = w_gu.shape
    F = F2 // 2
    n_blocks = x_part.shape[0] // MOE_BLK
    half = GU_CHUNK // 2
    src = np.concatenate([2 * np.arange(half), 2 * np.arange(half) + 1])
    perm = jnp.asarray(np.eye(GU_CHUNK, dtype=np.float32)[:, src], BF16)
    bgu_p = jnp.swapaxes(b_gu.reshape(E, F2 // GU_CHUNK, half, 2), 2, 3).reshape(E, 1, F2)
    grid_spec = pltpu.PrefetchScalarGridSpec(
        num_scalar_prefetch=2,
        grid=(n_blocks,),
        in_specs=[
            pl.BlockSpec((MOE_BLK, Dp), lambda i, be, nu: (i, 0)),
            pl.BlockSpec((1, 1, D, F2), lambda i, be, nu: (layer, be[i], 0, 0)),
            pl.BlockSpec((1, 1, F2), lambda i, be, nu: (be[i], 0, 0)),
            pl.BlockSpec((1, 1, F, D), lambda i, be, nu: (layer, be[i], 0, 0)),
            pl.BlockSpec((1, 1, D), lambda i, be, nu: (be[i], 0, 0)),
            pl.BlockSpec((GU_CHUNK, GU_CHUNK), lambda i, be, nu: (0, 0)),
        ] + ([] if y_prev is None else [pl.BlockSpec(memory_space=pl.ANY)]),
        out_specs=pl.BlockSpec((MOE_BLK, Dp), lambda i, be, nu: (i + blk0, 0)),
        scratch_shapes=[pltpu.VMEM((D, F2), BF16), pltpu.VMEM((F, D), BF16), pltpu.VMEM((MOE_BLK, F), BF16)],
    )
    args = (block_e, n_used, x_part, w_gu, bgu_p, w_dn, b_dn.reshape(E, 1, D), perm)
    return pl.pallas_call(
        _moe_kernel if y_prev is None else _moe_kernel_into,
        grid_spec=grid_spec,
        out_shape=jax.ShapeDtypeStruct((n_pad, Dp), jnp.int32),
        input_output_aliases={} if y_prev is None else {len(args): 0},
        compiler_params=_cparams(("arbitrary",)),
        name="moe_ffn",
    )(*args, *(() if y_prev is None else (y_prev,)))


SC_CORES = 2
SC_SUBCORES = 16
SC_WORKERS = SC_CORES * SC_SUBCORES
SC_ROWS = 64
SC_NBUF = 2


def _sc_gather(table, idx):
    V, W = table.shape
    n = idx.shape[0]
    assert n % (SC_WORKERS * SC_ROWS * SC_NBUF) == 0
    steps = n // (SC_WORKERS * SC_ROWS)
    idx3 = idx.reshape(SC_WORKERS, steps, SC_ROWS)
    mesh = plsc.VectorSubcoreMesh(core_axis_name="c", subcore_axis_name="s")

    @functools.partial(
        pl.kernel, mesh=mesh,
        out_type=jax.ShapeDtypeStruct((n, W), table.dtype),
        scratch_types=[pltpu.VMEM((steps, SC_ROWS), jnp.int32),
                       pltpu.VMEM((SC_NBUF, SC_ROWS, W), table.dtype),
                       pltpu.SemaphoreType.DMA((SC_NBUF,)),
                       pltpu.SemaphoreType.DMA((SC_NBUF,))],
        name="sc_row_gather",
    )
    def gather_kernel(table_hbm, idx_hbm, out_hbm, idx_v, rows_v, sem_g, sem_o):
        wid = lax.axis_index("s") * SC_CORES + lax.axis_index("c")
        step0 = wid * steps
        pltpu.sync_copy(idx_hbm.at[wid], idx_v)

        def gather(j, b):
            return pltpu.make_async_copy(table_hbm.at[idx_v.at[j + b]], rows_v.at[b], sem_g.at[b])

        def put(j, b):
            row0 = pl.multiple_of((step0 + j + b) * SC_ROWS, SC_ROWS)
            return pltpu.make_async_copy(rows_v.at[b], out_hbm.at[pl.ds(row0, SC_ROWS)], sem_o.at[b])

        @pl.loop(0, steps, step=SC_NBUF)
        def _(j):
            for b in range(SC_NBUF):
                gather(j, b).start()
            for b in range(SC_NBUF):
                gather(j, b).wait()
                put(j, b).start()
            for b in range(SC_NBUF):
                put(j, b).wait()

    return gather_kernel(table, idx3)


def _route(logits, n_tok):
    top_v, top_i = lax.top_k(logits, TOP_K)
    gates = jax.nn.softmax(top_v, axis=-1)
    n_assign = n_tok * TOP_K
    flat_e = top_i.reshape(-1)
    onehot = (flat_e[:, None] == jnp.arange(N_EXPERTS)[None, :]).astype(jnp.int32)
    rank = jnp.take_along_axis(jnp.cumsum(onehot, axis=0), flat_e[:, None], axis=1)[:, 0] - 1
    counts = jnp.sum(onehot, axis=0)
    padded = (counts + MOE_BLK - 1) // MOE_BLK * MOE_BLK
    ends_p = jnp.cumsum(padded)
    dest = (ends_p - padded)[flat_e] + rank
    n_blocks = -(-(n_assign + N_EXPERTS * (MOE_BLK - 1)) // MOE_BLK)
    n_pad = n_blocks * MOE_BLK
    blk_start = jnp.arange(n_blocks, dtype=jnp.int32) * MOE_BLK
    block_e = jnp.minimum(jnp.sum((ends_p[None, :] <= blk_start[:, None]).astype(jnp.int32), axis=1),
                          N_EXPERTS - 1).astype(jnp.int32)
    n_used = (ends_p[-1] // MOE_BLK).astype(jnp.int32).reshape(1)
    order = jnp.argsort(flat_e, stable=True).astype(jnp.int32)
    slot_e = jnp.repeat(block_e, MOE_BLK)
    j = jnp.arange(n_pad, dtype=jnp.int32) - (ends_p - padded)[slot_e]
    src = jnp.clip((jnp.cumsum(counts) - counts)[slot_e] + j, 0, n_assign - 1)
    tok_pad = jnp.where(j < counts[slot_e], order[src] // TOP_K, 0).astype(jnp.int32)
    return gates, dest.reshape(n_tok, TOP_K).astype(jnp.int32), tok_pad, block_e, n_used


def _final_kernel(x1_ref, yk_ref, gate_ref, mod_ref, g_ref, b_ref, o_ref, *, dn_alpha):
    D = x1_ref.shape[2]
    m = mod_ref[0, 0]
    gates = gate_ref[0]
    f = _unpack_bf16_pair(yk_ref[0, 0]) * gates[:, 0:1]
    for k in range(1, TOP_K):
        f = f + _unpack_bf16_pair(yk_ref[k, 0]) * gates[:, k:k + 1]
    o_ref[0] = _ln(dn_alpha * x1_ref[0] + m[5:6] * f) * g_ref[...] + b_ref[...]


def _final(x1, yk, gates, mod6, ln_g, ln_b, t0, nct, dn_alpha):
    B, So, D = x1.shape
    kind = lambda t: (t + t0 >= nct).astype(jnp.int32)
    blk = pl.BlockSpec((1, TM, D), lambda b, t: (b, t, 0))
    vec = pl.BlockSpec((1, D), lambda b, t: (0, 0))
    return pl.pallas_call(
        functools.partial(_final_kernel, dn_alpha=dn_alpha),
        grid=(B, So // TM),
        in_specs=[blk,
                  pl.BlockSpec((TOP_K, 1, TM, D // 2), lambda b, t: (0, b, t, 0)),
                  pl.BlockSpec((1, TM, TOP_K), lambda b, t: (b, t, 0)),
                  pl.BlockSpec((1, 1, 6, D), lambda b, t: (b, kind(t), 0, 0)), vec, vec],
        out_specs=blk,
        out_shape=jax.ShapeDtypeStruct((B, So, D), F32),
        compiler_params=_cparams(("arbitrary", "arbitrary")),
        name="ffn_residual",
    )(x1, yk, gates, mod6, ln_g, ln_b)


def _rope_tables(n_ctx, T):
    t = jnp.arange(T)
    row = (t // GRID_W).astype(F32)
    col = (t % GRID_W).astype(F32)
    n_freq = MLA_ROPE // 4
    inv = ROPE_BASE ** (-jnp.arange(n_freq, dtype=F32) / n_freq)
    ang = jnp.concatenate([row[:, None] * inv, col[:, None] * inv], axis=-1)
    cos, sin = jnp.cos(ang), jnp.sin(ang)
    z = jnp.zeros((T, 128 - MLA_ROPE), F32)
    cos128 = jnp.concatenate([cos, cos, z], axis=-1)
    sin128 = jnp.concatenate([-sin, sin, z], axis=-1)
    cos_c = jnp.concatenate([jnp.ones((n_ctx, MLA_ROPE), F32), jnp.zeros((n_ctx, 128 - MLA_ROPE), F32)], -1)
    sin_c = jnp.zeros((n_ctx, 128), F32)
    return jnp.concatenate([cos_c, cos128], 0), jnp.concatenate([sin_c, sin128], 0)


def _rope_slabs(w):
    ev, od = w[:, 0::2], w[:, 1::2]
    z = jnp.zeros((w.shape[0], 128 - MLA_ROPE), w.dtype)
    return jnp.concatenate([ev, od, z, od, ev, z], axis=-1)


def _blockdiag2(m):
    z = jnp.zeros_like(m[0])
    return jnp.concatenate([jnp.concatenate([m[0], z], 1), jnp.concatenate([z, m[1]], 1)], 0)


def kernel(x, c, ctx, c_ctx, ada_w, ada_b, w_in, rw_mu, rw_w0, rw_w2, rw_a0, rw_a2, rw_g2, rw_kk, rw_ka, rw_rk, rw_gn_w, rw_gn_b, mla_q_norm, mla_kv_norm, mla_w_uq, mla_w_ukv, na_rpb, w_out, ln1_g, ln1_b, router_w, router_b, w_gu, b_gu, w_dn, b_dn, ln2_g, ln2_b):
    B, T, D = x.shape
    n_ctx = ctx.shape[1]
    depth = ada_w.shape[0]
    S = n_ctx + T
    assert n_ctx % TM == 0 and T % TM == 0 and T % GRID_W == 0 and T // GRID_W >= NA_WIN_R
    nct = n_ctx // TM
    BH = B * RW_HEADS
    dn_alpha = (2 * depth) ** 0.25
    F = w_dn.shape[2]

    R = (B + 1 + 7) // 8 * 8
    cond = jnp.zeros((R, D), F32).at[:B].set(c).at[B].set(c_ctx)
    mod = _ada_mod(cond, ada_w, ada_b)
    mod_l = mod[:, :B].reshape(depth, B, 1, 6, D)
    mod_c = jnp.broadcast_to(mod[:, B].reshape(depth, 1, 1, 6, D), (depth, B, 1, 6, D))
    mod6 = jnp.concatenate([mod_c, mod_l], axis=2)

    cos128, sin128 = _rope_tables(n_ctx, T)
    ones_blk = jnp.kron(jnp.eye(RW_HEADS, dtype=F32), jnp.ones((RW_HEAD_DIM, RW_HEAD_DIM), F32)).astype(BF16)

    na_bias = [_na_bias_table(na_rpb[l]) for l in range(depth)]

    xa = jnp.concatenate([ctx, x], axis=1)
    for l in range(depth):
        need_ctx = l < depth - 1
        t0 = 0 if need_ctx else nct

        wi = w_in[l]
        c_m = RW_COLS
        w_in_ext = jnp.concatenate(
            [wi[:, :c_m + MLA_Q_LORA + MLA_KV_LORA], _rope_slabs(wi[:, c_m + MLA_Q_LORA + MLA_KV_LORA:c_m + MLA_COLS]),
             wi[:, c_m + MLA_COLS:]], axis=-1).astype(BF16)
        wuq = mla_w_uq[l].reshape(MLA_Q_LORA, MLA_HEADS, MLA_NOPE + MLA_ROPE)
        wuq_ext = jnp.concatenate(
            [jnp.concatenate([wuq[:, h, :MLA_NOPE], _rope_slabs(wuq[:, h, MLA_NOPE:])], -1) for h in range(MLA_HEADS)],
            axis=-1).astype(BF16)
        rw_prm = dict(
            mu=rw_mu[l],
            w0=rw_w0[l].reshape(1, 2 * RW_DIM), w2=_blockdiag2(rw_w2[l]).astype(BF16),
            a0=rw_a0[l].reshape(1, 2 * RW_DIM), a2=_blockdiag2(rw_a2[l]).astype(BF16),
            g2=rw_g2[l].astype(BF16), kk=rw_kk[l].reshape(1, RW_DIM), rk=rw_rk[l].reshape(1, RW_DIM),
            ones=ones_blk)
        mla_prm = dict(q_norm=mla_q_norm[l].reshape(1, -1), kv_norm=mla_kv_norm[l].reshape(1, -1),
                       w_uq=wuq_ext, w_ukv=mla_w_ukv[l].astype(BF16))
        rt = jnp.zeros((D, 128), F32).at[:, :N_EXPERTS].set(router_w[l])
        rt_hi = rt.astype(BF16)
        out_prm = dict(gn_w=rw_gn_w[l].reshape(1, -1), gn_b=rw_gn_b[l].reshape(1, -1), ones=ones_blk,
                       w_out=w_out[l].astype(BF16), ln1_g=ln1_g[l].reshape(1, -1), ln1_b=ln1_b[l].reshape(1, -1),
                       router_hi=rt_hi, router_lo=(rt - rt_hi.astype(F32)).astype(BF16),
                       router_b=jnp.zeros((1, 128), F32).at[0, :N_EXPERTS].set(router_b[l]))

        p_rw, p_mla, p_na = _in_proj(xa, mod6[l], w_in_ext, nct)
        feat, g_gate, bonus = _rw_features(p_rw, rw_prm, nct)
        ft = jnp.swapaxes(feat.reshape(S, BH, 8 * RW_HEAD_DIM), 1, 2).reshape(S, 8, RW_HEAD_DIM, BH)
        ka_t = jnp.tile(rw_ka[l].reshape(RW_HEADS, RW_HEAD_DIM).T[:, None, :], (1, B, 1)).reshape(RW_HEAD_DIM, BH)
        q, k, v = _mla_prep(p_mla, cos128, sin128, mla_prm)
        mla_o = _mla_attn(q, k, v, t0, nct, n_ctx)
        na_o = _na_attn(p_na, na_bias[l], n_ctx, need_ctx)

        o_scan = _rw_scan(ft, ka_t, n_ctx, mla_o)
        o_rw = jnp.swapaxes(o_scan, 2, 3).reshape(2, S, B * RW_DIM)

        x1, h, logits = _out_proj(o_rw, bonus, g_gate, mla_o, na_o, xa, mod6[l], out_prm, t0, nct, dn_alpha)

        So = x1.shape[1]
        n_tok = B * So
        gates, dest, tok_pad, block_e, n_used = _route(logits.reshape(n_tok, 128)[:, :N_EXPERTS], n_tok)
        n_pad = tok_pad.shape[0]
        n_blocks = n_pad // MOE_BLK
        parts = MOE_PARTS if n_blocks % MOE_PARTS == 0 else 1
        pb = n_blocks // parts
        h2 = h.reshape(n_tok, D // 2)
        y_sorted = None
        for c in range(parts):
            x_part = _sc_gather(h2, tok_pad[c * pb * MOE_BLK:(c + 1) * pb * MOE_BLK])
            y_sorted = _moe_ffn(x_part, block_e[c * pb:(c + 1) * pb], jnp.clip(n_used - c * pb, 0, pb),
                                w_gu, b_gu[l], w_dn, b_dn[l], l, c * pb, n_pad, y_sorted)
        yk = _sc_gather(y_sorted, dest.T.reshape(-1)).reshape(TOP_K, B, So, D // 2)

        xa = _final(x1, yk, gates.reshape(B, So, TOP_K), mod6[l], ln2_g[l].reshape(1, -1), ln2_b[l].reshape(1, -1),
                    t0, nct, dn_alpha)
    return xa
```

```python
import functools

import numpy as np
import jax
import jax.numpy as jnp
from jax import lax
from jax.experimental import pallas as pl
from jax.experimental.pallas import tpu as pltpu
from jax.experimental.pallas import tpu_sc as plsc

F32 = jnp.float32
BF16 = jnp.bfloat16

GRID_W = 64
RW_HEAD_DIM = 64
RW_HEADS = 4
RW_DIM = RW_HEADS * RW_HEAD_DIM
RW_LORA = 64
RW_G_LORA = 128
RW_GN_EPS = 64e-5
RW_COLS = 3 * RW_DIM + 4 * RW_LORA + RW_G_LORA
MLA_HEADS = 4
MLA_NOPE = 128
MLA_ROPE = 64
MLA_V = 128
MLA_Q_LORA = 256
MLA_KV_LORA = 128
MLA_COLS = MLA_Q_LORA + MLA_KV_LORA + MLA_ROPE
MLA_COLS_EXT = MLA_Q_LORA + MLA_KV_LORA + 256
MLA_DIM = MLA_HEADS * MLA_V
NA_HEADS = 4
NA_HEAD_DIM = 64
NA_DIM = NA_HEADS * NA_HEAD_DIM
NA_WIN_R = 8
NA_WIN_C = 16
NA_COLS = 3 * NA_DIM
ROPE_BASE = 10000.0
N_EXPERTS = 32
TOP_K = 4
SWIGLU_ALPHA = 1.702
SWIGLU_LIMIT = 7.0
NEG_INF = -1e30

TM = 256
SCAN_TB = 16
SCAN_VC = 32
MOE_BLK = 512
MOE_PARTS = 4
VMEM_LIMIT = 56 * 1024 * 1024


def _cparams(sem):
    return pltpu.CompilerParams(dimension_semantics=sem, vmem_limit_bytes=VMEM_LIMIT)


def _ln(x, eps=1e-5):
    mu = jnp.mean(x, axis=-1, keepdims=True)
    d = x - mu
    var = jnp.mean(d * d, axis=-1, keepdims=True)
    return d * lax.rsqrt(var + eps)


def _dot(a, b):
    return jnp.dot(a, b, preferred_element_type=F32)


def _split(a):
    hi = a.astype(BF16)
    lo = (a - hi.astype(F32)).astype(BF16)
    return hi, lo


def _dot_hl(a, b_bf16):
    hi, lo = _split(a)
    return _dot(hi, b_bf16) + _dot(lo, b_bf16)


def _dot3(a, b_hi, b_lo):
    hi, lo = _split(a)
    return _dot(hi, b_hi) + _dot(lo, b_hi) + _dot(hi, b_lo)


def _sigmoid(x):
    return 1.0 / (1.0 + jnp.exp(-x))


def _pack_bf16_pair(x):
    w = x.shape[1] // 2
    u = pltpu.bitcast(x.astype(BF16).astype(F32), jnp.uint32)
    lo = lax.shift_right_logical(u[:, :w], jnp.uint32(16))
    hi = lax.bitwise_and(u[:, w:], jnp.uint32(0xFFFF0000))
    return pltpu.bitcast(lax.bitwise_or(lo, hi), jnp.int32)


def _unpack_bf16_pair(p):
    u = pltpu.bitcast(p, jnp.uint32)
    lo = pltpu.bitcast(lax.shift_left(u, jnp.uint32(16)), F32)
    hi = pltpu.bitcast(lax.bitwise_and(u, jnp.uint32(0xFFFF0000)), F32)
    return jnp.concatenate([lo, hi], axis=-1)


def _ada_kernel(cond_ref, w_ref, b_ref, o_ref):
    c = cond_ref[...]
    s = c * _sigmoid(c)
    w = w_ref[0]
    w_hi, w_lo = _split(w)
    o_ref[0] = _dot3(s, w_hi, w_lo) + b_ref[0]


def _ada_mod(cond, ada_w, ada_b):
    L, D, N = ada_w.shape
    R = cond.shape[0]
    tn = 512
    return pl.pallas_call(
        _ada_kernel,
        grid=(L, N // tn),
        in_specs=[
            pl.BlockSpec((R, D), lambda l, j: (0, 0)),
            pl.BlockSpec((1, D, tn), lambda l, j: (l, 0, j)),
            pl.BlockSpec((1, 1, tn), lambda l, j: (l, 0, j)),
        ],
        out_specs=pl.BlockSpec((1, R, tn), lambda l, j: (l, 0, j)),
        out_shape=jax.ShapeDtypeStruct((L, R, N), F32),
        compiler_params=_cparams(("arbitrary", "arbitrary")),
        name="ada_mod",
    )(cond, ada_w, ada_b.reshape(L, 1, N))


def _win_kernel(x_ref, mod_ref, w_ref, prw_ref, pmla_ref, pna_ref):
    x = x_ref[0]
    m = mod_ref[0, 0]
    xm = _ln(x) * (1.0 + m[1:2]) + m[0:1]
    p = _dot(xm.astype(BF16), w_ref[...])
    prw_ref[0] = p[:, :RW_COLS]
    pmla_ref[0] = p[:, RW_COLS:RW_COLS + MLA_COLS_EXT]
    pna_ref[0] = p[:, RW_COLS + MLA_COLS_EXT:].astype(BF16)


def _in_proj(xa, mod6, w_in_ext, nct):
    B, S, D = xa.shape
    NC = w_in_ext.shape[1]
    kind = lambda t: (t >= nct).astype(jnp.int32)
    return pl.pallas_call(
        _win_kernel,
        grid=(B, S // TM),
        in_specs=[
            pl.BlockSpec((1, TM, D), lambda b, t: (b, t, 0)),
            pl.BlockSpec((1, 1, 6, D), lambda b, t: (b, kind(t), 0, 0)),
            pl.BlockSpec((D, NC), lambda b, t: (0, 0)),
        ],
        out_specs=[
            pl.BlockSpec((1, TM, RW_COLS), lambda b, t: (b, t, 0)),
            pl.BlockSpec((1, TM, MLA_COLS_EXT), lambda b, t: (b, t, 0)),
            pl.BlockSpec((1, TM, NA_COLS), lambda b, t: (b, t, 0)),
        ],
        out_shape=[
            jax.ShapeDtypeStruct((B, S, RW_COLS), F32),
            jax.ShapeDtypeStruct((B, S, MLA_COLS_EXT), F32),
            jax.ShapeDtypeStruct((B, S, NA_COLS), BF16),
        ],
        compiler_params=_cparams(("arbitrary", "arbitrary")),
        name="in_proj",
    )(xa, mod6, w_in_ext)


def _group_sum(x, ones_ref):
    return _dot_hl(x, ones_ref[...])


def _rwfeat_kernel(p_ref, pp_ref, pn_ref, mu_ref, w0_ref, w2_ref, a0_ref, a2_ref, g2_ref,
                   kk_ref, rk_ref, ones_ref, f_ref, g_ref, bonus_ref, *, nct, nt):
    t = pl.program_id(1)
    p = p_ref[0]
    first = jnp.logical_or(t == 0, t == nct)
    last = jnp.logical_or(t == nct - 1, t == nt - 1)
    prev_row = jnp.where(first, 0.0, pp_ref[0, 7:8, :])
    next_row = jnp.where(last, 0.0, pn_ref[0, 0:1, :])
    rows = lax.broadcasted_iota(jnp.int32, p.shape, 0)
    prev = jnp.where(rows == 0, prev_row, pltpu.roll(p, 1, axis=0))
    nxt = jnp.where(rows == TM - 1, next_row, pltpu.roll(p, TM - 1, axis=0))
    mu = mu_ref[...]
    xs = p + mu[0:1] * (prev - p) + mu[1:2] * (nxt - p)

    D3 = 3 * RW_DIM
    r = xs[:, 0:RW_DIM]
    k = xs[:, RW_DIM:2 * RW_DIM]
    v = xs[:, 2 * RW_DIM:D3]
    w_lo = xs[:, D3:D3 + 2 * RW_LORA]
    a_lo = xs[:, D3 + 2 * RW_LORA:D3 + 4 * RW_LORA]
    g_pre = xs[:, D3 + 4 * RW_LORA:]

    lw = _dot(jnp.tanh(w_lo).astype(BF16), w2_ref[...]) + w0_ref[...]
    logw = jnp.minimum(lw, 0.0) - jnp.log(1.0 + jnp.exp(-jnp.abs(lw))) - 0.5
    decay = jnp.exp(-jnp.exp(logw))
    a = _sigmoid(_dot(a_lo.astype(BF16), a2_ref[...]) + a0_ref[...])

    kkr = k * kk_ref[...]
    kk = kkr * lax.rsqrt(_group_sum(kkr * kkr, ones_ref) + 1e-12)
    g_ref[0] = _dot(_sigmoid(g_pre).astype(BF16), g2_ref[...])
    bonus_ref[0] = _group_sum(r * k * rk_ref[...], ones_ref) * v

    comps = (r, k, v, kk, decay[:, :RW_DIM], a[:, :RW_DIM], decay[:, RW_DIM:], a[:, RW_DIM:])
    N = RW_HEAD_DIM
    for h in range(RW_HEADS):
        for ci, comp in enumerate(comps):
            col = (h * len(comps) + ci) * N
            f_ref[:, col:col + N] = comp[:, h * N:(h + 1) * N]


def _rw_features(p_rw, prm, nct):
    B, S, C = p_rw.shape
    nt = S // TM
    hb = TM // 8
    last_hb = S // 8 - 1
    full = lambda shape: pl.BlockSpec(shape, lambda b, t: (0,) * len(shape))
    return pl.pallas_call(
        functools.partial(_rwfeat_kernel, nct=nct, nt=nt),
        grid=(B, nt),
        in_specs=[
            pl.BlockSpec((1, TM, C), lambda b, t: (b, t, 0)),
            pl.BlockSpec((1, 8, C), lambda b, t: (b, jnp.maximum(t * hb - 1, 0), 0)),
            pl.BlockSpec((1, 8, C), lambda b, t: (b, jnp.minimum((t + 1) * hb, last_hb), 0)),
            full((2, C)),
            full((1, 2 * RW_DIM)), full((2 * RW_LORA, 2 * RW_DIM)),
            full((1, 2 * RW_DIM)), full((2 * RW_LORA, 2 * RW_DIM)),
            full((RW_G_LORA, RW_DIM)),
            full((1, RW_DIM)), full((1, RW_DIM)),
            full((RW_DIM, RW_DIM)),
        ],
        out_specs=[
            pl.BlockSpec((TM, 8 * RW_DIM), lambda b, t: (t, b)),
            pl.BlockSpec((1, TM, RW_DIM), lambda b, t: (b, t, 0)),
            pl.BlockSpec((1, TM, RW_DIM), lambda b, t: (b, t, 0)),
        ],
        out_shape=[
            jax.ShapeDtypeStruct((S, B * 8 * RW_DIM), F32),
            jax.ShapeDtypeStruct((B, S, RW_DIM), F32),
            jax.ShapeDtypeStruct((B, S, RW_DIM), F32),
        ],
        compiler_params=_cparams(("arbitrary", "arbitrary")),
        name="rw_features",
    )(p_rw, p_rw, p_rw, prm["mu"], prm["w0"], prm["w2"], prm["a0"], prm["a2"], prm["g2"],
      prm["kk"], prm["rk"], prm["ones"])


def _scan_kernel(fs_ref, fd_ref, ka_ref, after_ref, o_ref, s_ref, tmp_ref, *, tb):
    del after_ref
    d = pl.program_id(0)
    g = pl.program_id(1)
    N = RW_HEAD_DIM

    @pl.when(g == 0)
    def _():
        s_ref[...] = jnp.zeros_like(s_ref)

    ka = ka_ref[...]

    def step(i, carry):
        tt = jnp.where(d == 0, i, tb - 1 - i)
        r = fs_ref[tt, 0]
        k = fs_ref[tt, 1]
        kk = fs_ref[tt, 3]
        w = fd_ref[tt, 0]
        a = fd_ref[tt, 1]
        b = a * kk
        kd = k * (1.0 + (a - 1.0) * ka)
        wr = w * r
        br = jnp.sum(b * r, axis=0, keepdims=True)
        kr = jnp.sum(kd * r, axis=0, keepdims=True)
        tmp_ref[0] = wr
        tmp_ref[1] = b
        tmp_ref[2] = kd
        for c in range(N // SCAN_VC):
            vs = pl.ds(c * SCAN_VC, SCAN_VC)
            vc = fs_ref[tt, 2, vs, :]
            sa = [jnp.zeros_like(vc), jnp.zeros_like(vc)]
            op = [jnp.zeros_like(vc), jnp.zeros_like(vc)]
            for j in range(N):
                sk = s_ref[j, vs, :]
                sa[j % 2] = sa[j % 2] + sk * fs_ref[tt, 3, pl.ds(j, 1), :]
                op[j % 2] = op[j % 2] + sk * tmp_ref[0, pl.ds(j, 1), :]
            sa = sa[0] + sa[1]
            op = op[0] + op[1]
            for j in range(N):
                s_ref[j, vs, :] = (s_ref[j, vs, :] * fd_ref[tt, 0, pl.ds(j, 1), :]
                                   - sa * tmp_ref[1, pl.ds(j, 1), :]
                                   + vc * tmp_ref[2, pl.ds(j, 1), :])
            o_ref[0, tt, vs, :] = op - sa * br + vc * kr
        return carry

    lax.fori_loop(0, tb, step, 0)


def _rw_scan(ft, ka_t, n_ctx, after):
    S, _, N, BH = ft.shape
    tb = SCAN_TB
    nb = S // tb
    ncb = n_ctx // tb

    def tblk(d, g):
        bwd = jnp.where(g < ncb, ncb - 1 - g, nb - 1 - g + ncb)
        return jnp.where(d == 0, g, bwd)

    return pl.pallas_call(
        functools.partial(_scan_kernel, tb=tb),
        grid=(2, nb),
        in_specs=[
            pl.BlockSpec((tb, 4, N, BH), lambda d, g: (tblk(d, g), 0, 0, 0)),
            pl.BlockSpec((tb, 2, N, BH), lambda d, g: (tblk(d, g), 2 + d, 0, 0)),
            pl.BlockSpec((N, BH), lambda d, g: (0, 0)),
            pl.BlockSpec(memory_space=pl.ANY),
        ],
        out_specs=pl.BlockSpec((1, tb, N, BH), lambda d, g: (d, tblk(d, g), 0, 0)),
        out_shape=jax.ShapeDtypeStruct((2, S, N, BH), F32),
        scratch_shapes=[pltpu.VMEM((N, N, BH), F32), pltpu.VMEM((3, N, BH), F32)],
        compiler_params=_cparams(("arbitrary", "arbitrary")),
        name="rw_scan",
    )(ft, ft, ka_t, after)


def _mlaprep_kernel(p_ref, cos_ref, sin_ref, qn_ref, kvn_ref, wuq_ref, wukv_ref, q_ref, k_ref, v_ref):
    p = p_ref[0]
    cos = cos_ref[...]
    sin = sin_ref[...]
    scale = (MLA_NOPE + MLA_ROPE) ** -0.5

    def rms(x, g):
        return x * lax.rsqrt(jnp.mean(x * x, axis=-1, keepdims=True) + 1e-6) * g

    q = _dot(rms(p[:, :MLA_Q_LORA], qn_ref[...]).astype(BF16), wuq_ref[...])
    kv = _dot(rms(p[:, MLA_Q_LORA:MLA_Q_LORA + MLA_KV_LORA], kvn_ref[...]).astype(BF16), wukv_ref[...])
    c0 = MLA_Q_LORA + MLA_KV_LORA
    kr = (p[:, c0:c0 + 128] * cos + p[:, c0 + 128:c0 + 256] * sin).astype(BF16)
    for h in range(MLA_HEADS):
        qb = h * 384
        q_ref[0, :, h * 256:h * 256 + 128] = (q[:, qb:qb + 128] * scale).astype(BF16)
        q_ref[0, :, h * 256 + 128:h * 256 + 256] = (
            (q[:, qb + 128:qb + 256] * cos + q[:, qb + 256:qb + 384] * sin) * scale).astype(BF16)
        k_ref[0, :, h * 256:h * 256 + 128] = kv[:, h * 256:h * 256 + 128].astype(BF16)
        k_ref[0, :, h * 256 + 128:h * 256 + 256] = kr
        v_ref[0, :, h * 128:(h + 1) * 128] = kv[:, h * 256 + 128:h * 256 + 256].astype(BF16)


def _mla_prep(p_mla, cos128, sin128, prm):
    B, S, C = p_mla.shape
    full = lambda shape: pl.BlockSpec(shape, lambda b, t: (0,) * len(shape))
    H = MLA_HEADS
    return pl.pallas_call(
        _mlaprep_kernel,
        grid=(B, S // TM),
        in_specs=[
            pl.BlockSpec((1, TM, C), lambda b, t: (b, t, 0)),
            pl.BlockSpec((TM, 128), lambda b, t: (t, 0)),
            pl.BlockSpec((TM, 128), lambda b, t: (t, 0)),
            full((1, MLA_Q_LORA)), full((1, MLA_KV_LORA)),
            full((MLA_Q_LORA, H * 384)), full((MLA_KV_LORA, H * 256)),
        ],
        out_specs=[
            pl.BlockSpec((1, TM, H * 256), lambda b, t: (b, t, 0)),
            pl.BlockSpec((1, TM, H * 256), lambda b, t: (b, t, 0)),
            pl.BlockSpec((1, TM, H * 128), lambda b, t: (b, t, 0)),
        ],
        out_shape=[
            jax.ShapeDtypeStruct((B, S, H * 256), BF16),
            jax.ShapeDtypeStruct((B, S, H * 256), BF16),
            jax.ShapeDtypeStruct((B, S, H * 128), BF16),
        ],
        compiler_params=_cparams(("arbitrary", "arbitrary")),
        name="mla_prep",
    )(p_mla, cos128, sin128, prm["q_norm"], prm["kv_norm"], prm["w_uq"], prm["w_ukv"])


def _mla_attn_kernel(q_ref, k_ref, v_ref, o_ref, *, t0, nct, n_ctx):
    t = pl.program_id(1) + t0

    def attend(n_keys):
        for h in range(MLA_HEADS):
            q = q_ref[0, :, h * 256:(h + 1) * 256]
            k = k_ref[0, 0:n_keys, h * 256:(h + 1) * 256]
            s = lax.dot_general(q, k, (((1,), (1,)), ((), ())), preferred_element_type=F32)
            m = jnp.max(s, axis=-1, keepdims=True)
            e = jnp.exp(s - m)
            l = jnp.sum(e, axis=-1, keepdims=True)
            o = _dot(e.astype(BF16), v_ref[0, 0:n_keys, h * 128:(h + 1) * 128])
            o_ref[0, :, h * 128:(h + 1) * 128] = (o / l).astype(BF16)

    S = k_ref.shape[1]
    if t0 < nct:
        @pl.when(t < nct)
        def _():
            attend(n_ctx)

        @pl.when(t >= nct)
        def _():
            attend(S)
    else:
        attend(S)


def _mla_attn(q, k, v, t0, nct, n_ctx):
    B, S, _ = q.shape
    nq = S // TM - t0
    H = MLA_HEADS
    return pl.pallas_call(
        functools.partial(_mla_attn_kernel, t0=t0, nct=nct, n_ctx=n_ctx),
        grid=(B, nq),
        in_specs=[
            pl.BlockSpec((1, TM, H * 256), lambda b, t: (b, t + t0, 0)),
            pl.BlockSpec((1, S, H * 256), lambda b, t: (b, 0, 0)),
            pl.BlockSpec((1, S, H * 128), lambda b, t: (b, 0, 0)),
        ],
        out_specs=pl.BlockSpec((1, TM, H * 128), lambda b, t: (b, t, 0)),
        out_shape=jax.ShapeDtypeStruct((B, nq * TM, H * 128), BF16),
        compiler_params=_cparams(("arbitrary", "arbitrary")),
        name="mla_attn",
    )(q, k, v)


NA_QR = 4
NA_KR = NA_WIN_R + NA_QR
NA_QB = NA_QR * GRID_W


def _na_kernel(p_ref, bias_ref, o_ref, *, n_ctx, rows, with_ctx):
    s_id = pl.program_id(1)
    W = GRID_W
    n_loc = NA_KR * W
    scale = NA_HEAD_DIM ** -0.5
    nq_ctx = n_ctx // NA_QB
    nblk = rows // NA_QR

    def heads_out(q, parts):
        outs = []
        for h in range(NA_HEADS):
            hs = slice(h * NA_HEAD_DIM, (h + 1) * NA_HEAD_DIM)
            qh = q[:, hs] * scale
            ss = []
            for kx, vx, bias in parts:
                s = lax.dot_general(qh, kx[:, hs], (((1,), (1,)), ((), ())), preferred_element_type=F32)
                if bias is not None:
                    s = s + bias(h)
                ss.append(s)
            m = ss[0].max(axis=-1, keepdims=True)
            for s in ss[1:]:
                m = jnp.maximum(m, s.max(axis=-1, keepdims=True))
            acc = 0.0
            l = 0.0
            for s, (kx, vx, bias) in zip(ss, parts):
                e = jnp.exp(s - m)
                l = l + jnp.sum(e, axis=-1, keepdims=True)
                acc = acc + _dot(e.astype(BF16), vx[:, hs])
            outs.append(acc / l)
        return jnp.concatenate(outs, axis=-1).astype(BF16)

    k_c = p_ref[0, 0:n_ctx, NA_DIM:2 * NA_DIM]
    v_c = p_ref[0, 0:n_ctx, 2 * NA_DIM:3 * NA_DIM]

    def lat_block(j):
        i0 = j * NA_QR
        k_start = jnp.clip(i0 - NA_WIN_R // 2, 0, rows - NA_KR)
        pat = jnp.where(j == 0, 0, jnp.where(j == nblk - 1, 2, 1))
        q0 = pl.multiple_of(n_ctx + i0 * W, NA_QB)
        k0 = pl.multiple_of(n_ctx + k_start * W, W)
        q = p_ref[0, pl.ds(q0, NA_QB), 0:NA_DIM]
        k_l = p_ref[0, pl.ds(k0, n_loc), NA_DIM:2 * NA_DIM]
        v_l = p_ref[0, pl.ds(k0, n_loc), 2 * NA_DIM:3 * NA_DIM]
        o_ref[0] = heads_out(q, [(k_l, v_l, lambda h: bias_ref[pat, h]), (k_c, v_c, None)])

    if with_ctx:
        @pl.when(s_id < nq_ctx)
        def _():
            q0 = pl.multiple_of(s_id * NA_QB, NA_QB)
            q = p_ref[0, pl.ds(q0, NA_QB), 0:NA_DIM]
            o_ref[0] = heads_out(q, [(k_c, v_c, None)])

        @pl.when(s_id >= nq_ctx)
        def _():
            lat_block(s_id - nq_ctx)
    else:
        lat_block(s_id)


def _na_attn(p_na, bias_tab, n_ctx, with_ctx):
    B, S, C = p_na.shape
    T = S - n_ctx
    rows = T // GRID_W
    assert rows % NA_QR == 0 and rows >= NA_KR and n_ctx % NA_QB == 0
    nsteps = rows // NA_QR + (n_ctx // NA_QB if with_ctx else 0)
    return pl.pallas_call(
        functools.partial(_na_kernel, n_ctx=n_ctx, rows=rows, with_ctx=with_ctx),
        grid=(B, nsteps),
        in_specs=[
            pl.BlockSpec((1, S, C), lambda b, s: (b, 0, 0)),
            pl.BlockSpec(bias_tab.shape, lambda b, s: (0, 0, 0, 0)),
        ],
        out_specs=pl.BlockSpec((1, NA_QB, NA_DIM), lambda b, s: (b, s, 0)),
        out_shape=jax.ShapeDtypeStruct((B, nsteps * NA_QB, NA_DIM), BF16),
        compiler_params=_cparams(("arbitrary", "arbitrary")),
        name="na_attn",
    )(p_na, bias_tab)


def _na_bias_table(rpb):
    col = np.arange(GRID_W)
    c_start = np.clip(col - NA_WIN_C // 2, 0, GRID_W - NA_WIN_C)
    in_win = (col[None, :] >= c_start[:, None]) & (col[None, :] < c_start[:, None] + NA_WIN_C)
    dc_idx = np.clip(col[None, :] - col[:, None] + NA_WIN_C - 1, 0, 2 * NA_WIN_C - 2)
    qa = np.arange(NA_QR)[:, None]
    kc = np.arange(NA_KR)[None, :]
    row_ok, dr_idx = [], []
    for pat in range(3):
        off = (NA_WIN_R // 2) * pat
        first = (0 * qa, qa, 0 * qa + NA_WIN_R // 2)[pat]
        row_ok.append((kc >= first) & (kc < first + NA_WIN_R))
        dr_idx.append(np.clip(kc - qa - off + NA_WIN_R - 1, 0, 2 * NA_WIN_R - 2))
    sel_c = np.eye(2 * NA_WIN_C - 1, dtype=np.float32)[dc_idx]
    sel_r = np.eye(2 * NA_WIN_R - 1, dtype=np.float32)[np.stack(dr_idx)]
    hp = lax.Precision.HIGHEST
    t = jnp.einsum('hrs,qks->hrqk', rpb, sel_c, precision=hp)
    bias = jnp.einsum('pacr,hrqk->phaqck', sel_r, t, precision=hp)
    ok = jnp.asarray(np.stack(row_ok))[:, None, :, None, :, None] & jnp.asarray(in_win)[None, None, None, :, None, :]
    bias = jnp.where(ok, bias, NEG_INF)
    return bias.reshape(3, NA_HEADS, NA_QB, NA_KR * GRID_W).astype(F32)


def _outproj_kernel(orw_ref, bonus_ref, g_ref, mla_ref, na_ref, x_ref, mod_ref, gnw_ref, gnb_ref, ones_ref,
                    wout_ref, ln1g_ref, ln1b_ref, rwh_ref, rwl_ref, rb_ref,
                    x1_ref, h_ref, lg_ref, *, dn_alpha):
    o = orw_ref[0] + orw_ref[1]
    inv_n = 1.0 / RW_HEAD_DIM
    mu = _group_sum(o, ones_ref) * inv_n
    dlt = o - mu
    var = _group_sum(dlt * dlt, ones_ref) * inv_n
    on = dlt * lax.rsqrt(var + RW_GN_EPS) * gnw_ref[...] + gnb_ref[...]
    rw_y = ((on + bonus_ref[0]) * g_ref[0]).astype(BF16)
    y = (_dot(rw_y, wout_ref[0:RW_DIM, :])
         + _dot(mla_ref[0], wout_ref[RW_DIM:RW_DIM + MLA_DIM, :])
         + _dot(na_ref[0], wout_ref[RW_DIM + MLA_DIM:, :]))
    m = mod_ref[0, 0]
    x1 = _ln(dn_alpha * x_ref[0] + m[2:3] * y) * ln1g_ref[...] + ln1b_ref[...]
    x1_ref[0] = x1
    h = _ln(x1) * (1.0 + m[4:5]) + m[3:4]
    h_ref[0] = _pack_bf16_pair(h)
    lg_ref[0] = _dot3(h, rwh_ref[...], rwl_ref[...]) + rb_ref[...]


def _out_proj(o_rw, bonus, g, mla_o, na_o, xa, mod6, prm, t0, nct, dn_alpha):
    B, S, D = xa.shape
    nt = S // TM - t0
    So = nt * TM
    kind = lambda t: (t + t0 >= nct).astype(jnp.int32)
    full = lambda shape: pl.BlockSpec(shape, lambda b, t: (0,) * len(shape))
    off = lambda C: pl.BlockSpec((1, TM, C), lambda b, t: (b, t + t0, 0))
    own = lambda C: pl.BlockSpec((1, TM, C), lambda b, t: (b, t, 0))
    return pl.pallas_call(
        functools.partial(_outproj_kernel, dn_alpha=dn_alpha),
        grid=(B, nt),
        in_specs=[
            pl.BlockSpec((2, TM, RW_DIM), lambda b, t: (0, t + t0, b)),
            off(RW_DIM), off(RW_DIM), own(MLA_DIM), own(NA_DIM), off(D),
            pl.BlockSpec((1, 1, 6, D), lambda b, t: (b, kind(t), 0, 0)),
            full((1, RW_DIM)), full((1, RW_DIM)), full((RW_DIM, RW_DIM)),
            full((D, D)), full((1, D)), full((1, D)),
            full((D, 128)), full((D, 128)), full((1, 128)),
        ],
        out_specs=[own(D), own(D // 2), own(128)],
        out_shape=[
            jax.ShapeDtypeStruct((B, So, D), F32),
            jax.ShapeDtypeStruct((B, So, D // 2), jnp.int32),
            jax.ShapeDtypeStruct((B, So, 128), F32),
        ],
        compiler_params=_cparams(("arbitrary", "arbitrary")),
        name="out_proj",
    )(o_rw, bonus, g, mla_o, na_o, xa, mod6, prm["gn_w"], prm["gn_b"], prm["ones"],
      prm["w_out"], prm["ln1_g"], prm["ln1_b"], prm["router_hi"], prm["router_lo"], prm["router_b"])


GU_CHUNK = 256


def _moe_kernel(be_ref, nu_ref, x_ref, wgu_ref, bgu_ref, wdn_ref, bdn_ref, perm_ref, y_ref,
                wgu_s, wdn_s, act_s):
    i = pl.program_id(0)
    F = wdn_ref.shape[2]
    n_chunks = 2 * F // GU_CHUNK
    half = GU_CHUNK // 2
    valid = i < nu_ref[0]
    e = be_ref[i]
    new_expert = jnp.logical_or(i == 0, e != be_ref[jnp.maximum(i - 1, 0)])

    @pl.when(jnp.logical_and(valid, new_expert))
    def _():
        for c in range(n_chunks):
            cs = slice(c * GU_CHUNK, (c + 1) * GU_CHUNK)
            wgu_s[:, cs] = _dot(wgu_ref[0, 0, :, cs].astype(BF16), perm_ref[...]).astype(BF16)
        wdn_s[...] = wdn_ref[0, 0].astype(BF16)

    @pl.when(valid)
    def _():
        gu = _dot(_unpack_bf16_pair(x_ref[...]).astype(BF16), wgu_s[...]) + bgu_ref[0]
        for c in range(n_chunks):
            glu = jnp.minimum(gu[:, c * GU_CHUNK:c * GU_CHUNK + half], SWIGLU_LIMIT)
            lin = jnp.clip(gu[:, c * GU_CHUNK + half:(c + 1) * GU_CHUNK], -SWIGLU_LIMIT, SWIGLU_LIMIT)
            act_s[:, c * half:(c + 1) * half] = (glu * _sigmoid(SWIGLU_ALPHA * glu) * (lin + 1.0)).astype(BF16)
        y = _dot(act_s[...], wdn_s[...]) + bdn_ref[0]
        y_ref[...] = _pack_bf16_pair(y)

    @pl.when(jnp.logical_not(valid))
    def _():
        y_ref[...] = jnp.zeros_like(y_ref)


def _moe_kernel_into(be_ref, nu_ref, x_ref, wgu_ref, bgu_ref, wdn_ref, bdn_ref, perm_ref, yprev_ref, y_ref,
                     wgu_s, wdn_s, act_s):
    del yprev_ref
    _moe_kernel(be_ref, nu_ref, x_ref, wgu_ref, bgu_ref, wdn_ref, bdn_ref, perm_ref, y_ref, wgu_s, wdn_s, act_s)


def _moe_ffn(x_part, block_e, n_used, w_gu, b_gu, w_dn, b_dn, layer, blk0, n_pad, y_prev):
    D = w_gu.shape[2]
    Dp = D // 2
    _, E, _, F2 = w_gu.shape
    F = F2 // 2
    n_blocks = x_part.shape[0] // MOE_BLK
    half = GU_CHUNK // 2
    src = np.concatenate([2 * np.arange(half), 2 * np.arange(half) + 1])
    perm = jnp.asarray(np.eye(GU_CHUNK, dtype=np.float32)[:, src], BF16)
    bgu_p = jnp.swapaxes(b_gu.reshape(E, F2 // GU_CHUNK, half, 2), 2, 3).reshape(E, 1, F2)
    grid_spec = pltpu.PrefetchScalarGridSpec(
        num_scalar_prefetch=2,
        grid=(n_blocks,),
        in_specs=[
            pl.BlockSpec((MOE_BLK, Dp), lambda i, be, nu: (i, 0)),
            pl.BlockSpec((1, 1, D, F2), lambda i, be, nu: (layer, be[i], 0, 0)),
            pl.BlockSpec((1, 1, F2), lambda i, be, nu: (be[i], 0, 0)),
            pl.BlockSpec((1, 1, F, D), lambda i, be, nu: (layer, be[i], 0, 0)),
            pl.BlockSpec((1, 1, D), lambda i, be, nu: (be[i], 0, 0)),
            pl.BlockSpec((GU_CHUNK, GU_CHUNK), lambda i, be, nu: (0, 0)),
        ] + ([] if y_prev is None else [pl.BlockSpec(memory_space=pl.ANY)]),
        out_specs=pl.BlockSpec((MOE_BLK, Dp), lambda i, be, nu: (i + blk0, 0)),
        scratch_shapes=[pltpu.VMEM((D, F2), BF16), pltpu.VMEM((F, D), BF16), pltpu.VMEM((MOE_BLK, F), BF16)],
    )
    args = (block_e, n_used, x_part, w_gu, bgu_p, w_dn, b_dn.reshape(E, 1, D), perm)
    return pl.pallas_call(
        _moe_kernel if y_prev is None else _moe_kernel_into,
        grid_spec=grid_spec,
        out_shape=jax.ShapeDtypeStruct((n_pad, Dp), jnp.int32),
        input_output_aliases={} if y_prev is None else {len(args): 0},
        compiler_params=_cparams(("arbitrary",)),
        name="moe_ffn",
    )(*args, *(() if y_prev is None else (y_prev,)))


SC_CORES = 2
SC_SUBCORES = 16
SC_WORKERS = SC_CORES * SC_SUBCORES
SC_ROWS = 64
SC_NBUF = 2


def _sc_gather(table, idx):
    V, W = table.shape
    n = idx.shape[0]
    assert n % (SC_WORKERS * SC_ROWS * SC_NBUF) == 0
    steps = n // (SC_WORKERS * SC_ROWS)
    idx3 = idx.reshape(SC_WORKERS, steps, SC_ROWS)
    mesh = plsc.VectorSubcoreMesh(core_axis_name="c", subcore_axis_name="s")

    @functools.partial(
        pl.kernel, mesh=mesh,
        out_type=jax.ShapeDtypeStruct((n, W), table.dtype),
        scratch_types=[pltpu.VMEM((steps, SC_ROWS), jnp.int32),
                       pltpu.VMEM((SC_NBUF, SC_ROWS, W), table.dtype),
                       pltpu.SemaphoreType.DMA((SC_NBUF,)),
                       pltpu.SemaphoreType.DMA((SC_NBUF,))],
        name="sc_row_gather",
    )
    def gather_kernel(table_hbm, idx_hbm, out_hbm, idx_v, rows_v, sem_g, sem_o):
        wid = lax.axis_index("s") * SC_CORES + lax.axis_index("c")
        step0 = wid * steps
        pltpu.sync_copy(idx_hbm.at[wid], idx_v)

        def gather(j, b):
            return pltpu.make_async_copy(table_hbm.at[idx_v.at[j + b]], rows_v.at[b], sem_g.at[b])

        def put(j, b):
            row0 = pl.multiple_of((step0 + j + b) * SC_ROWS, SC_ROWS)
            return pltpu.make_async_copy(rows_v.at[b], out_hbm.at[pl.ds(row0, SC_ROWS)], sem_o.at[b])

        @pl.loop(0, steps, step=SC_NBUF)
        def _(j):
            for b in range(SC_NBUF):
                gather(j, b).start()
            for b in range(SC_NBUF):
                gather(j, b).wait()
                put(j, b).start()
            for b in range(SC_NBUF):
                put(j, b).wait()

    return gather_kernel(table, idx3)


SC_SROWS = 128
SC_SWORDS = 128


def _sc_scatter(vals, idx, n_out):
    n, W = vals.shape
    assert W == SC_SWORDS and n % (SC_WORKERS * SC_SROWS) == 0
    steps = n // (SC_WORKERS * SC_SROWS)
    idx3 = idx.reshape(SC_WORKERS, steps, SC_SROWS)
    vals4 = vals.reshape(SC_WORKERS, steps, SC_SROWS, W)
    mesh = plsc.VectorSubcoreMesh(core_axis_name="c", subcore_axis_name="s")

    @functools.partial(
        pl.kernel, mesh=mesh,
        out_type=jax.ShapeDtypeStruct((n_out, W), vals.dtype),
        scratch_types=[pltpu.VMEM((steps, SC_SROWS), jnp.int32),
                       pltpu.VMEM((SC_SROWS, W), vals.dtype)],
        name="sc_row_scatter",
    )
    def scatter_kernel(vals_hbm, idx_hbm, out_hbm, idx_v, rows_v):
        wid = lax.axis_index("s") * SC_CORES + lax.axis_index("c")
        pltpu.sync_copy(idx_hbm.at[wid], idx_v)

        @pl.loop(0, steps)
        def _(j):
            pltpu.sync_copy(vals_hbm.at[wid, j], rows_v)
            pltpu.sync_copy(rows_v, out_hbm.at[idx_v.at[j]])

    return scatter_kernel(vals4, idx3)


def _router_kernel(lg_ref, tri_ref, meta_ref, cnt_ref, base_s):
    i = pl.program_id(0)

    @pl.when(i == 0)
    def _():
        base_s[...] = jnp.zeros_like(base_s)

    lg = lg_ref[...]
    lane = lax.broadcasted_iota(jnp.int32, lg.shape, 1)
    lg = jnp.where(lane < N_EXPERTS, lg, -jnp.inf)
    vals, ids, hots = [], [], []
    for _ in range(TOP_K):
        m = jnp.max(lg, axis=-1, keepdims=True)
        idx = jnp.min(jnp.where(lg == m, lane, lg.shape[1]), axis=-1, keepdims=True)
        hot = lane == idx
        vals.append(m)
        ids.append(idx)
        hots.append(hot)
        lg = jnp.where(hot, -jnp.inf, lg)
    es = [jnp.exp(v - vals[0]) for v in vals]
    den = es[0] + es[1] + es[2] + es[3]
    onehot = hots[0].astype(F32) + hots[1].astype(F32) + hots[2].astype(F32) + hots[3].astype(F32)
    before = base_s[...] + _dot(tri_ref[...], onehot.astype(BF16))
    meta = jnp.zeros(lg.shape, F32)
    for k in range(TOP_K):
        rank = jnp.sum(jnp.where(hots[k], before, 0.0), axis=-1, keepdims=True)
        meta = jnp.where(lane == k, ids[k].astype(F32), meta)
        meta = jnp.where(lane == TOP_K + k, rank, meta)
        meta = jnp.where(lane == 2 * TOP_K + k, es[k] / den, meta)
    meta_ref[...] = meta
    base_s[...] = base_s[...] + jnp.sum(onehot, axis=0, keepdims=True)
    cnt_ref[...] = base_s[...]


def _router(logits):
    n_tok, W = logits.shape
    tri = jnp.asarray(np.tril(np.ones((TM, TM), np.float32), -1), BF16)
    return pl.pallas_call(
        _router_kernel,
        grid=(n_tok // TM,),
        in_specs=[pl.BlockSpec((TM, W), lambda i: (i, 0)), pl.BlockSpec((TM, TM), lambda i: (0, 0))],
        out_specs=[pl.BlockSpec((TM, W), lambda i: (i, 0)), pl.BlockSpec((1, W), lambda i: (0, 0))],
        out_shape=[jax.ShapeDtypeStruct((n_tok, W), F32), jax.ShapeDtypeStruct((1, W), F32)],
        scratch_shapes=[pltpu.VMEM((1, W), F32)],
        compiler_params=_cparams(("arbitrary",)),
        name="router",
    )(logits, tri)


def _route(logits, n_tok):
    meta, cnt = _router(logits)
    top_i = meta[:, 0:TOP_K].astype(jnp.int32)
    rank = meta[:, TOP_K:2 * TOP_K].astype(jnp.int32)
    gates = meta[:, 2 * TOP_K:3 * TOP_K]
    n_assign = n_tok * TOP_K
    counts = cnt[0, :N_EXPERTS].astype(jnp.int32)
    padded = (counts + MOE_BLK - 1) // MOE_BLK * MOE_BLK
    ends_p = jnp.cumsum(padded)
    g_start = ends_p - padded
    sel = top_i[:, :, None] == jnp.arange(N_EXPERTS, dtype=jnp.int32)[None, None, :]
    dest = jnp.sum(jnp.where(sel, g_start[None, None, :], 0), axis=-1) + rank
    n_blocks = -(-(n_assign + N_EXPERTS * (MOE_BLK - 1)) // MOE_BLK)
    n_pad = n_blocks * MOE_BLK
    blk_start = jnp.arange(n_blocks, dtype=jnp.int32) * MOE_BLK
    block_e = jnp.minimum(jnp.sum((ends_p[None, :] <= blk_start[:, None]).astype(jnp.int32), axis=1),
                          N_EXPERTS - 1).astype(jnp.int32)
    n_used = (ends_p[-1] // MOE_BLK).astype(jnp.int32).reshape(1)
    tok_of = jnp.broadcast_to(jnp.arange(n_tok, dtype=jnp.int32)[:, None, None], (n_tok, TOP_K, SC_SWORDS))
    slot_tok = _sc_scatter(tok_of.reshape(n_assign, SC_SWORDS), dest.reshape(-1), n_pad)[:, 0]
    tok_pad = jnp.clip(slot_tok, 0, n_tok - 1)
    return gates, dest, tok_pad, block_e, n_used


def _final_kernel(x1_ref, yk_ref, gate_ref, mod_ref, g_ref, b_ref, o_ref, *, dn_alpha):
    D = x1_ref.shape[2]
    m = mod_ref[0, 0]
    gates = gate_ref[0]
    f = _unpack_bf16_pair(yk_ref[0, 0]) * gates[:, 0:1]
    for k in range(1, TOP_K):
        f = f + _unpack_bf16_pair(yk_ref[k, 0]) * gates[:, k:k + 1]
    o_ref[0] = _ln(dn_alpha * x1_ref[0] + m[5:6] * f) * g_ref[...] + b_ref[...]


def _final(x1, yk, gates, mod6, ln_g, ln_b, t0, nct, dn_alpha):
    B, So, D = x1.shape
    kind = lambda t: (t + t0 >= nct).astype(jnp.int32)
    blk = pl.BlockSpec((1, TM, D), lambda b, t: (b, t, 0))
    vec = pl.BlockSpec((1, D), lambda b, t: (0, 0))
    return pl.pallas_call(
        functools.partial(_final_kernel, dn_alpha=dn_alpha),
        grid=(B, So // TM),
        in_specs=[blk,
                  pl.BlockSpec((TOP_K, 1, TM, D // 2), lambda b, t: (0, b, t, 0)),
                  pl.BlockSpec((1, TM, TOP_K), lambda b, t: (b, t, 0)),
                  pl.BlockSpec((1, 1, 6, D), lambda b, t: (b, kind(t), 0, 0)), vec, vec],
        out_specs=blk,
        out_shape=jax.ShapeDtypeStruct((B, So, D), F32),
        compiler_params=_cparams(("arbitrary", "arbitrary")),
        name="ffn_residual",
    )(x1, yk, gates, mod6, ln_g, ln_b)


def _rope_tables(n_ctx, T):
    t = jnp.arange(T)
    row = (t // GRID_W).astype(F32)
    col = (t % GRID_W).astype(F32)
    n_freq = MLA_ROPE // 4
    inv = ROPE_BASE ** (-jnp.arange(n_freq, dtype=F32) / n_freq)
    ang = jnp.concatenate([row[:, None] * inv, col[:, None] * inv], axis=-1)
    cos, sin = jnp.cos(ang), jnp.sin(ang)
    z = jnp.zeros((T, 128 - MLA_ROPE), F32)
    cos128 = jnp.concatenate([cos, cos, z], axis=-1)
    sin128 = jnp.concatenate([-sin, sin, z], axis=-1)
    cos_c = jnp.concatenate([jnp.ones((n_ctx, MLA_ROPE), F32), jnp.zeros((n_ctx, 128 - MLA_ROPE), F32)], -1)
    sin_c = jnp.zeros((n_ctx, 128), F32)
    return jnp.concatenate([cos_c, cos128], 0), jnp.concatenate([sin_c, sin128], 0)


def _rope_slabs(w):
    ev, od = w[:, 0::2], w[:, 1::2]
    z = jnp.zeros((w.shape[0], 128 - MLA_ROPE), w.dtype)
    return jnp.concatenate([ev, od, z, od, ev, z], axis=-1)


def _blockdiag2(m):
    z = jnp.zeros_like(m[0])
    return jnp.concatenate([jnp.concatenate([m[0], z], 1), jnp.concatenate([z, m[1]], 1)], 0)


def kernel(x, c, ctx, c_ctx, ada_w, ada_b, w_in, rw_mu, rw_w0, rw_w2, rw_a0, rw_a2, rw_g2, rw_kk, rw_ka, rw_rk, rw_gn_w, rw_gn_b, mla_q_norm, mla_kv_norm, mla_w_uq, mla_w_ukv, na_rpb, w_out, ln1_g, ln1_b, router_w, router_b, w_gu, b_gu, w_dn, b_dn, ln2_g, ln2_b):
    B, T, D = x.shape
    n_ctx = ctx.shape[1]
    depth = ada_w.shape[0]
    S = n_ctx + T
    assert n_ctx % TM == 0 and T % TM == 0 and T % GRID_W == 0 and T // GRID_W >= NA_WIN_R
    nct = n_ctx // TM
    BH = B * RW_HEADS
    dn_alpha = (2 * depth) ** 0.25
    F = w_dn.shape[2]

    R = (B + 1 + 7) // 8 * 8
    cond = jnp.zeros((R, D), F32).at[:B].set(c).at[B].set(c_ctx)
    mod = _ada_mod(cond, ada_w, ada_b)
    mod_l = mod[:, :B].reshape(depth, B, 1, 6, D)
    mod_c = jnp.broadcast_to(mod[:, B].reshape(depth, 1, 1, 6, D), (depth, B, 1, 6, D))
    mod6 = jnp.concatenate([mod_c, mod_l], axis=2)

    cos128, sin128 = _rope_tables(n_ctx, T)
    ones_blk = jnp.kron(jnp.eye(RW_HEADS, dtype=F32), jnp.ones((RW_HEAD_DIM, RW_HEAD_DIM), F32)).astype(BF16)

    na_bias = [_na_bias_table(na_rpb[l]) for l in range(depth)]

    xa = jnp.concatenate([ctx, x], axis=1)
    for l in range(depth):
        need_ctx = l < depth - 1
        t0 = 0 if need_ctx else nct

        wi = w_in[l]
        c_m = RW_COLS
        w_in_ext = jnp.concatenate(
            [wi[:, :c_m + MLA_Q_LORA + MLA_KV_LORA], _rope_slabs(wi[:, c_m + MLA_Q_LORA + MLA_KV_LORA:c_m + MLA_COLS]),
             wi[:, c_m + MLA_COLS:]], axis=-1).astype(BF16)
        wuq = mla_w_uq[l].reshape(MLA_Q_LORA, MLA_HEADS, MLA_NOPE + MLA_ROPE)
        wuq_ext = jnp.concatenate(
            [jnp.concatenate([wuq[:, h, :MLA_NOPE], _rope_slabs(wuq[:, h, MLA_NOPE:])], -1) for h in range(MLA_HEADS)],
            axis=-1).astype(BF16)
        rw_prm = dict(
            mu=rw_mu[l],
            w0=rw_w0[l].reshape(1, 2 * RW_DIM), w2=_blockdiag2(rw_w2[l]).astype(BF16),
            a0=rw_a0[l].reshape(1, 2 * RW_DIM), a2=_blockdiag2(rw_a2[l]).astype(BF16),
            g2=rw_g2[l].astype(BF16), kk=rw_kk[l].reshape(1, RW_DIM), rk=rw_rk[l].reshape(1, RW_DIM),
            ones=ones_blk)
        mla_prm = dict(q_norm=mla_q_norm[l].reshape(1, -1), kv_norm=mla_kv_norm[l].reshape(1, -1),
                       w_uq=wuq_ext, w_ukv=mla_w_ukv[l].astype(BF16))
        rt = jnp.zeros((D, 128), F32).at[:, :N_EXPERTS].set(router_w[l])
        rt_hi = rt.astype(BF16)
        out_prm = dict(gn_w=rw_gn_w[l].reshape(1, -1), gn_b=rw_gn_b[l].reshape(1, -1), ones=ones_blk,
                       w_out=w_out[l].astype(BF16), ln1_g=ln1_g[l].reshape(1, -1), ln1_b=ln1_b[l].reshape(1, -1),
                       router_hi=rt_hi, router_lo=(rt - rt_hi.astype(F32)).astype(BF16),
                       router_b=jnp.zeros((1, 128), F32).at[0, :N_EXPERTS].set(router_b[l]))

        p_rw, p_mla, p_na = _in_proj(xa, mod6[l], w_in_ext, nct)
        feat, g_gate, bonus = _rw_features(p_rw, rw_prm, nct)
        ft = jnp.swapaxes(feat.reshape(S, BH, 8 * RW_HEAD_DIM), 1, 2).reshape(S, 8, RW_HEAD_DIM, BH)
        ka_t = jnp.tile(rw_ka[l].reshape(RW_HEADS, RW_HEAD_DIM).T[:, None, :], (1, B, 1)).reshape(RW_HEAD_DIM, BH)
        q, k, v = _mla_prep(p_mla, cos128, sin128, mla_prm)
        mla_o = _mla_attn(q, k, v, t0, nct, n_ctx)
        na_o = _na_attn(p_na, na_bias[l], n_ctx, need_ctx)

        o_scan = _rw_scan(ft, ka_t, n_ctx, mla_o)
        o_rw = jnp.swapaxes(o_scan, 2, 3).reshape(2, S, B * RW_DIM)

        x1, h, logits = _out_proj(o_rw, bonus, g_gate, mla_o, na_o, xa, mod6[l], out_prm, t0, nct, dn_alpha)

        So = x1.shape[1]
        n_tok = B * So
        gates, dest, tok_pad, block_e, n_used = _route(logits.reshape(n_tok, 128), n_tok)
        n_pad = tok_pad.shape[0]
        n_blocks = n_pad // MOE_BLK
        parts = MOE_PARTS if n_blocks % MOE_PARTS == 0 else 1
        pb = n_blocks // parts
        h2 = h.reshape(n_tok, D // 2)
        y_sorted = None
        for c in range(parts):
            x_part = _sc_gather(h2, tok_pad[c * pb * MOE_BLK:(c + 1) * pb * MOE_BLK])
            y_sorted = _moe_ffn(x_part, block_e[c * pb:(c + 1) * pb], jnp.clip(n_used - c * pb, 0, pb),
                                w_gu, b_gu[l], w_dn, b_dn[l], l, c * pb, n_pad, y_sorted)
        yk = _sc_gather(y_sorted, dest.T.reshape(-1)).reshape(TOP_K, B, So, D // 2)

        xa = _final(x1, yk, gates.reshape(B, So, TOP_K), mod6[l], ln2_g[l].reshape(1, -1), ln2_b[l].reshape(1, -1),
                    t0, nct, dn_alpha)
    return xa
```

```python
import functools

import numpy as np
import jax
import jax.numpy as jnp
from jax import lax
from jax.experimental import pallas as pl
from jax.experimental.pallas import tpu as pltpu
from jax.experimental.pallas import tpu_sc as plsc

F32 = jnp.float32
BF16 = jnp.bfloat16

GRID_W = 64
RW_HEAD_DIM = 64
RW_HEADS = 4
RW_DIM = RW_HEADS * RW_HEAD_DIM
RW_LORA = 64
RW_G_LORA = 128
RW_GN_EPS = 64e-5
RW_COLS = 3 * RW_DIM + 4 * RW_LORA + RW_G_LORA
MLA_HEADS = 4
MLA_NOPE = 128
MLA_ROPE = 64
MLA_V = 128
MLA_Q_LORA = 256
MLA_KV_LORA = 128
MLA_COLS = MLA_Q_LORA + MLA_KV_LORA + MLA_ROPE
MLA_COLS_EXT = MLA_Q_LORA + MLA_KV_LORA + 256
MLA_DIM = MLA_HEADS * MLA_V
NA_HEADS = 4
NA_HEAD_DIM = 64
NA_DIM = NA_HEADS * NA_HEAD_DIM
NA_WIN_R = 8
NA_WIN_C = 16
NA_COLS = 3 * NA_DIM
ROPE_BASE = 10000.0
N_EXPERTS = 32
TOP_K = 4
SWIGLU_ALPHA = 1.702
SWIGLU_LIMIT = 7.0
NEG_INF = -1e30

TM = 256
SCAN_TB = 16
SCAN_VC = 64
MOE_BLK = 1024
MOE_PARTS = 4
VMEM_LIMIT = 56 * 1024 * 1024


def _cparams(sem):
    return pltpu.CompilerParams(dimension_semantics=sem, vmem_limit_bytes=VMEM_LIMIT)


def _ln(x, eps=1e-5):
    mu = jnp.mean(x, axis=-1, keepdims=True)
    d = x - mu
    var = jnp.mean(d * d, axis=-1, keepdims=True)
    return d * lax.rsqrt(var + eps)


def _dot(a, b):
    return jnp.dot(a, b, preferred_element_type=F32)


def _split(a):
    hi = a.astype(BF16)
    lo = (a - hi.astype(F32)).astype(BF16)
    return hi, lo


def _dot_hl(a, b_bf16):
    hi, lo = _split(a)
    return _dot(hi, b_bf16) + _dot(lo, b_bf16)


def _dot3(a, b_hi, b_lo):
    hi, lo = _split(a)
    return _dot(hi, b_hi) + _dot(lo, b_hi) + _dot(hi, b_lo)


def _sigmoid(x):
    return 1.0 / (1.0 + jnp.exp(-x))


def _pack_bf16_pair(x):
    w = x.shape[1] // 2
    u = pltpu.bitcast(x.astype(BF16).astype(F32), jnp.uint32)
    lo = lax.shift_right_logical(u[:, :w], jnp.uint32(16))
    hi = lax.bitwise_and(u[:, w:], jnp.uint32(0xFFFF0000))
    return pltpu.bitcast(lax.bitwise_or(lo, hi), jnp.int32)


def _unpack_bf16_pair(p):
    u = pltpu.bitcast(p, jnp.uint32)
    lo = pltpu.bitcast(lax.shift_left(u, jnp.uint32(16)), F32)
    hi = pltpu.bitcast(lax.bitwise_and(u, jnp.uint32(0xFFFF0000)), F32)
    return jnp.concatenate([lo, hi], axis=-1)


def _ada_kernel(cond_ref, w_ref, b_ref, o_ref):
    c = cond_ref[...]
    s = c * _sigmoid(c)
    w = w_ref[0]
    w_hi, w_lo = _split(w)
    o_ref[0] = _dot3(s, w_hi, w_lo) + b_ref[0]


def _ada_mod(cond, ada_w, ada_b):
    L, D, N = ada_w.shape
    R = cond.shape[0]
    tn = 512
    return pl.pallas_call(
        _ada_kernel,
        grid=(L, N // tn),
        in_specs=[
            pl.BlockSpec((R, D), lambda l, j: (0, 0)),
            pl.BlockSpec((1, D, tn), lambda l, j: (l, 0, j)),
            pl.BlockSpec((1, 1, tn), lambda l, j: (l, 0, j)),
        ],
        out_specs=pl.BlockSpec((1, R, tn), lambda l, j: (l, 0, j)),
        out_shape=jax.ShapeDtypeStruct((L, R, N), F32),
        compiler_params=_cparams(("arbitrary", "arbitrary")),
        name="ada_mod",
    )(cond, ada_w, ada_b.reshape(L, 1, N))


def _win_kernel(x_ref, mod_ref, w_ref, prw_ref, pmla_ref, pna_ref):
    x = x_ref[0]
    m = mod_ref[0, 0]
    xm = _ln(x) * (1.0 + m[1:2]) + m[0:1]
    p = _dot(xm.astype(BF16), w_ref[...])
    prw_ref[0] = p[:, :RW_COLS]
    pmla_ref[0] = p[:, RW_COLS:RW_COLS + MLA_COLS_EXT]
    pna_ref[0] = p[:, RW_COLS + MLA_COLS_EXT:].astype(BF16)


def _in_proj(xa, mod6, w_in_ext, nct):
    B, S, D = xa.shape
    NC = w_in_ext.shape[1]
    kind = lambda t: (t >= nct).astype(jnp.int32)
    return pl.pallas_call(
        _win_kernel,
        grid=(B, S // TM),
        in_specs=[
            pl.BlockSpec((1, TM, D), lambda b, t: (b, t, 0)),
            pl.BlockSpec((1, 1, 6, D), lambda b, t: (b, kind(t), 0, 0)),
            pl.BlockSpec((D, NC), lambda b, t: (0, 0)),
        ],
        out_specs=[
            pl.BlockSpec((1, TM, RW_COLS), lambda b, t: (b, t, 0)),
            pl.BlockSpec((1, TM, MLA_COLS_EXT), lambda b, t: (b, t, 0)),
            pl.BlockSpec((1, TM, NA_COLS), lambda b, t: (b, t, 0)),
        ],
        out_shape=[
            jax.ShapeDtypeStruct((B, S, RW_COLS), F32),
            jax.ShapeDtypeStruct((B, S, MLA_COLS_EXT), F32),
            jax.ShapeDtypeStruct((B, S, NA_COLS), BF16),
        ],
        compiler_params=_cparams(("arbitrary", "arbitrary")),
        name="in_proj",
    )(xa, mod6, w_in_ext)


def _group_sum(x, ones_ref):
    return _dot_hl(x, ones_ref[...])


def _rwfeat_kernel(p_ref, pp_ref, pn_ref, mu_ref, w0_ref, w2_ref, a0_ref, a2_ref, g2_ref,
                   kk_ref, rk_ref, ones_ref, f_ref, g_ref, bonus_ref, *, nct, nt):
    t = pl.program_id(1)
    p = p_ref[0]
    first = jnp.logical_or(t == 0, t == nct)
    last = jnp.logical_or(t == nct - 1, t == nt - 1)
    prev_row = jnp.where(first, 0.0, pp_ref[0, 7:8, :])
    next_row = jnp.where(last, 0.0, pn_ref[0, 0:1, :])
    rows = lax.broadcasted_iota(jnp.int32, p.shape, 0)
    prev = jnp.where(rows == 0, prev_row, pltpu.roll(p, 1, axis=0))
    nxt = jnp.where(rows == TM - 1, next_row, pltpu.roll(p, TM - 1, axis=0))
    mu = mu_ref[...]
    xs = p + mu[0:1] * (prev - p) + mu[1:2] * (nxt - p)

    D3 = 3 * RW_DIM
    r = xs[:, 0:RW_DIM]
    k = xs[:, RW_DIM:2 * RW_DIM]
    v = xs[:, 2 * RW_DIM:D3]
    w_lo = xs[:, D3:D3 + 2 * RW_LORA]
    a_lo = xs[:, D3 + 2 * RW_LORA:D3 + 4 * RW_LORA]
    g_pre = xs[:, D3 + 4 * RW_LORA:]

    lw = _dot(jnp.tanh(w_lo).astype(BF16), w2_ref[...]) + w0_ref[...]
    logw = jnp.minimum(lw, 0.0) - jnp.log(1.0 + jnp.exp(-jnp.abs(lw))) - 0.5
    decay = jnp.exp(-jnp.exp(logw))
    a = _sigmoid(_dot(a_lo.astype(BF16), a2_ref[...]) + a0_ref[...])

    kkr = k * kk_ref[...]
    kk = kkr * lax.rsqrt(_group_sum(kkr * kkr, ones_ref) + 1e-12)
    g_ref[0] = _dot(_sigmoid(g_pre).astype(BF16), g2_ref[...])
    bonus_ref[0] = _group_sum(r * k * rk_ref[...], ones_ref) * v

    comps = (r, k, v, kk, decay[:, :RW_DIM], a[:, :RW_DIM], decay[:, RW_DIM:], a[:, RW_DIM:])
    N = RW_HEAD_DIM
    for h in range(RW_HEADS):
        for ci, comp in enumerate(comps):
            col = (h * len(comps) + ci) * N
            f_ref[:, col:col + N] = comp[:, h * N:(h + 1) * N]


def _rw_features(p_rw, prm, nct):
    B, S, C = p_rw.shape
    nt = S // TM
    hb = TM // 8
    last_hb = S // 8 - 1
    full = lambda shape: pl.BlockSpec(shape, lambda b, t: (0,) * len(shape))
    return pl.pallas_call(
        functools.partial(_rwfeat_kernel, nct=nct, nt=nt),
        grid=(B, nt),
        in_specs=[
            pl.BlockSpec((1, TM, C), lambda b, t: (b, t, 0)),
            pl.BlockSpec((1, 8, C), lambda b, t: (b, jnp.maximum(t * hb - 1, 0), 0)),
            pl.BlockSpec((1, 8, C), lambda b, t: (b, jnp.minimum((t + 1) * hb, last_hb), 0)),
            full((2, C)),
            full((1, 2 * RW_DIM)), full((2 * RW_LORA, 2 * RW_DIM)),
            full((1, 2 * RW_DIM)), full((2 * RW_LORA, 2 * RW_DIM)),
            full((RW_G_LORA, RW_DIM)),
            full((1, RW_DIM)), full((1, RW_DIM)),
            full((RW_DIM, RW_DIM)),
        ],
        out_specs=[
            pl.BlockSpec((TM, 8 * RW_DIM), lambda b, t: (t, b)),
            pl.BlockSpec((1, TM, RW_DIM), lambda b, t: (b, t, 0)),
            pl.BlockSpec((1, TM, RW_DIM), lambda b, t: (b, t, 0)),
        ],
        out_shape=[
            jax.ShapeDtypeStruct((S, B * 8 * RW_DIM), F32),
            jax.ShapeDtypeStruct((B, S, RW_DIM), F32),
            jax.ShapeDtypeStruct((B, S, RW_DIM), F32),
        ],
        compiler_params=_cparams(("arbitrary", "arbitrary")),
        name="rw_features",
    )(p_rw, p_rw, p_rw, prm["mu"], prm["w0"], prm["w2"], prm["a0"], prm["a2"], prm["g2"],
      prm["kk"], prm["rk"], prm["ones"])


def _scan_kernel(fs_ref, fd_ref, ka_ref, after_ref, o_ref, s_ref, tmp_ref, *, tb):
    del after_ref
    d = pl.program_id(0)
    g = pl.program_id(1)
    N = RW_HEAD_DIM

    @pl.when(g == 0)
    def _():
        s_ref[...] = jnp.zeros_like(s_ref)

    ka = ka_ref[...]

    def step(i, carry):
        tt = jnp.where(d == 0, i, tb - 1 - i)
        r = fs_ref[tt, 0]
        k = fs_ref[tt, 1]
        kk = fs_ref[tt, 3]
        w = fd_ref[tt, 0]
        a = fd_ref[tt, 1]
        b = a * kk
        kd = k * (1.0 + (a - 1.0) * ka)
        wr = w * r
        br = jnp.sum(b * r, axis=0, keepdims=True)
        kr = jnp.sum(kd * r, axis=0, keepdims=True)
        tmp_ref[0] = wr
        tmp_ref[1] = b
        tmp_ref[2] = kd
        for c in range(N // SCAN_VC):
            vs = pl.ds(c * SCAN_VC, SCAN_VC)
            vc = fs_ref[tt, 2, vs, :]
            sa = jnp.zeros_like(vc)
            op = jnp.zeros_like(vc)
            for j in range(N):
                sk = s_ref[j, vs, :]
                sa = sa + sk * fs_ref[tt, 3, pl.ds(j, 1), :]
                op = op + sk * tmp_ref[0, pl.ds(j, 1), :]
            for j in range(N):
                s_ref[j, vs, :] = (s_ref[j, vs, :] * fd_ref[tt, 0, pl.ds(j, 1), :]
                                   - sa * tmp_ref[1, pl.ds(j, 1), :]
                                   + vc * tmp_ref[2, pl.ds(j, 1), :])
            o_ref[0, tt, vs, :] = op - sa * br + vc * kr
        return carry

    lax.fori_loop(0, tb, step, 0)


def _rw_scan(ft, ka_t, n_ctx, after):
    S, _, N, BH = ft.shape
    tb = SCAN_TB
    nb = S // tb
    ncb = n_ctx // tb

    def tblk(d, g):
        bwd = jnp.where(g < ncb, ncb - 1 - g, nb - 1 - g + ncb)
        return jnp.where(d == 0, g, bwd)

    return pl.pallas_call(
        functools.partial(_scan_kernel, tb=tb),
        grid=(2, nb),
        in_specs=[
            pl.BlockSpec((tb, 4, N, BH), lambda d, g: (tblk(d, g), 0, 0, 0)),
            pl.BlockSpec((tb, 2, N, BH), lambda d, g: (tblk(d, g), 2 + d, 0, 0)),
            pl.BlockSpec((N, BH), lambda d, g: (0, 0)),
            pl.BlockSpec(memory_space=pl.ANY),
        ],
        out_specs=pl.BlockSpec((1, tb, N, BH), lambda d, g: (d, tblk(d, g), 0, 0)),
        out_shape=jax.ShapeDtypeStruct((2, S, N, BH), F32),
        scratch_shapes=[pltpu.VMEM((N, N, BH), F32), pltpu.VMEM((3, N, BH), F32)],
        compiler_params=_cparams(("arbitrary", "arbitrary")),
        name="rw_scan",
    )(ft, ft, ka_t, after)


def _mlaprep_kernel(p_ref, cos_ref, sin_ref, qn_ref, kvn_ref, wuq_ref, wukv_ref, q_ref, k_ref, v_ref):
    p = p_ref[0]
    cos = cos_ref[...]
    sin = sin_ref[...]
    scale = (MLA_NOPE + MLA_ROPE) ** -0.5

    def rms(x, g):
        return x * lax.rsqrt(jnp.mean(x * x, axis=-1, keepdims=True) + 1e-6) * g

    q = _dot(rms(p[:, :MLA_Q_LORA], qn_ref[...]).astype(BF16), wuq_ref[...])
    kv = _dot(rms(p[:, MLA_Q_LORA:MLA_Q_LORA + MLA_KV_LORA], kvn_ref[...]).astype(BF16), wukv_ref[...])
    c0 = MLA_Q_LORA + MLA_KV_LORA
    kr = (p[:, c0:c0 + 128] * cos + p[:, c0 + 128:c0 + 256] * sin).astype(BF16)
    for h in range(MLA_HEADS):
        qb = h * 384
        q_ref[0, :, h * 256:h * 256 + 128] = (q[:, qb:qb + 128] * scale).astype(BF16)
        q_ref[0, :, h * 256 + 128:h * 256 + 256] = (
            (q[:, qb + 128:qb + 256] * cos + q[:, qb + 256:qb + 384] * sin) * scale).astype(BF16)
        k_ref[0, :, h * 256:h * 256 + 128] = kv[:, h * 256:h * 256 + 128].astype(BF16)
        k_ref[0, :, h * 256 + 128:h * 256 + 256] = kr
        v_ref[0, :, h * 128:(h + 1) * 128] = kv[:, h * 256 + 128:h * 256 + 256].astype(BF16)


def _mla_prep(p_mla, cos128, sin128, prm):
    B, S, C = p_mla.shape
    full = lambda shape: pl.BlockSpec(shape, lambda b, t: (0,) * len(shape))
    H = MLA_HEADS
    return pl.pallas_call(
        _mlaprep_kernel,
        grid=(B, S // TM),
        in_specs=[
            pl.BlockSpec((1, TM, C), lambda b, t: (b, t, 0)),
            pl.BlockSpec((TM, 128), lambda b, t: (t, 0)),
            pl.BlockSpec((TM, 128), lambda b, t: (t, 0)),
            full((1, MLA_Q_LORA)), full((1, MLA_KV_LORA)),
            full((MLA_Q_LORA, H * 384)), full((MLA_KV_LORA, H * 256)),
        ],
        out_specs=[
            pl.BlockSpec((1, TM, H * 256), lambda b, t: (b, t, 0)),
            pl.BlockSpec((1, TM, H * 256), lambda b, t: (b, t, 0)),
            pl.BlockSpec((1, TM, H * 128), lambda b, t: (b, t, 0)),
        ],
        out_shape=[
            jax.ShapeDtypeStruct((B, S, H * 256), BF16),
            jax.ShapeDtypeStruct((B, S, H * 256), BF16),
            jax.ShapeDtypeStruct((B, S, H * 128), BF16),
        ],
        compiler_params=_cparams(("arbitrary", "arbitrary")),
        name="mla_prep",
    )(p_mla, cos128, sin128, prm["q_norm"], prm["kv_norm"], prm["w_uq"], prm["w_ukv"])


def _mla_attn_kernel(q_ref, k_ref, v_ref, o_ref, *, t0, nct, n_ctx):
    t = pl.program_id(1) + t0

    def attend(n_keys):
        for h in range(MLA_HEADS):
            q = q_ref[0, :, h * 256:(h + 1) * 256]
            k = k_ref[0, 0:n_keys, h * 256:(h + 1) * 256]
            s = lax.dot_general(q, k, (((1,), (1,)), ((), ())), preferred_element_type=F32)
            m = jnp.max(s, axis=-1, keepdims=True)
            e = jnp.exp(s - m)
            l = jnp.sum(e, axis=-1, keepdims=True)
            o = _dot(e.astype(BF16), v_ref[0, 0:n_keys, h * 128:(h + 1) * 128])
            o_ref[0, :, h * 128:(h + 1) * 128] = (o / l).astype(BF16)

    S = k_ref.shape[1]
    if t0 < nct:
        @pl.when(t < nct)
        def _():
            attend(n_ctx)

        @pl.when(t >= nct)
        def _():
            attend(S)
    else:
        attend(S)


def _mla_attn(q, k, v, t0, nct, n_ctx):
    B, S, _ = q.shape
    nq = S // TM - t0
    H = MLA_HEADS
    return pl.pallas_call(
        functools.partial(_mla_attn_kernel, t0=t0, nct=nct, n_ctx=n_ctx),
        grid=(B, nq),
        in_specs=[
            pl.BlockSpec((1, TM, H * 256), lambda b, t: (b, t + t0, 0)),
            pl.BlockSpec((1, S, H * 256), lambda b, t: (b, 0, 0)),
            pl.BlockSpec((1, S, H * 128), lambda b, t: (b, 0, 0)),
        ],
        out_specs=pl.BlockSpec((1, TM, H * 128), lambda b, t: (b, t, 0)),
        out_shape=jax.ShapeDtypeStruct((B, nq * TM, H * 128), BF16),
        compiler_params=_cparams(("arbitrary", "arbitrary")),
        name="mla_attn",
    )(q, k, v)


NA_QR = 4
NA_KR = NA_WIN_R + NA_QR
NA_QB = NA_QR * GRID_W


def _na_kernel(p_ref, bias_ref, o_ref, *, n_ctx, rows, with_ctx):
    s_id = pl.program_id(1)
    W = GRID_W
    n_loc = NA_KR * W
    scale = NA_HEAD_DIM ** -0.5
    nq_ctx = n_ctx // NA_QB
    nblk = rows // NA_QR

    def heads_out(q, parts):
        outs = []
        for h in range(NA_HEADS):
            hs = slice(h * NA_HEAD_DIM, (h + 1) * NA_HEAD_DIM)
            qh = q[:, hs] * scale
            ss = []
            for kx, vx, bias in parts:
                s = lax.dot_general(qh, kx[:, hs], (((1,), (1,)), ((), ())), preferred_element_type=F32)
                if bias is not None:
                    s = s + bias(h)
                ss.append(s)
            m = ss[0].max(axis=-1, keepdims=True)
            for s in ss[1:]:
                m = jnp.maximum(m, s.max(axis=-1, keepdims=True))
            acc = 0.0
            l = 0.0
            for s, (kx, vx, bias) in zip(ss, parts):
                e = jnp.exp(s - m)
                l = l + jnp.sum(e, axis=-1, keepdims=True)
                acc = acc + _dot(e.astype(BF16), vx[:, hs])
            outs.append(acc / l)
        return jnp.concatenate(outs, axis=-1).astype(BF16)

    k_c = p_ref[0, 0:n_ctx, NA_DIM:2 * NA_DIM]
    v_c = p_ref[0, 0:n_ctx, 2 * NA_DIM:3 * NA_DIM]

    def lat_block(j):
        i0 = j * NA_QR
        k_start = jnp.clip(i0 - NA_WIN_R // 2, 0, rows - NA_KR)
        pat = jnp.where(j == 0, 0, jnp.where(j == nblk - 1, 2, 1))
        q0 = pl.multiple_of(n_ctx + i0 * W, NA_QB)
        k0 = pl.multiple_of(n_ctx + k_start * W, W)
        q = p_ref[0, pl.ds(q0, NA_QB), 0:NA_DIM]
        k_l = p_ref[0, pl.ds(k0, n_loc), NA_DIM:2 * NA_DIM]
        v_l = p_ref[0, pl.ds(k0, n_loc), 2 * NA_DIM:3 * NA_DIM]
        o_ref[0] = heads_out(q, [(k_l, v_l, lambda h: bias_ref[pat, h]), (k_c, v_c, None)])

    if with_ctx:
        @pl.when(s_id < nq_ctx)
        def _():
            q0 = pl.multiple_of(s_id * NA_QB, NA_QB)
            q = p_ref[0, pl.ds(q0, NA_QB), 0:NA_DIM]
            o_ref[0] = heads_out(q, [(k_c, v_c, None)])

        @pl.when(s_id >= nq_ctx)
        def _():
            lat_block(s_id - nq_ctx)
    else:
        lat_block(s_id)


def _na_attn(p_na, bias_tab, n_ctx, with_ctx):
    B, S, C = p_na.shape
    T = S - n_ctx
    rows = T // GRID_W
    assert rows % NA_QR == 0 and rows >= NA_KR and n_ctx % NA_QB == 0
    nsteps = rows // NA_QR + (n_ctx // NA_QB if with_ctx else 0)
    return pl.pallas_call(
        functools.partial(_na_kernel, n_ctx=n_ctx, rows=rows, with_ctx=with_ctx),
        grid=(B, nsteps),
        in_specs=[
            pl.BlockSpec((1, S, C), lambda b, s: (b, 0, 0)),
            pl.BlockSpec(bias_tab.shape, lambda b, s: (0, 0, 0, 0)),
        ],
        out_specs=pl.BlockSpec((1, NA_QB, NA_DIM), lambda b, s: (b, s, 0)),
        out_shape=jax.ShapeDtypeStruct((B, nsteps * NA_QB, NA_DIM), BF16),
        compiler_params=_cparams(("arbitrary", "arbitrary")),
        name="na_attn",
    )(p_na, bias_tab)


def _na_bias_table(rpb):
    col = np.arange(GRID_W)
    c_start = np.clip(col - NA_WIN_C // 2, 0, GRID_W - NA_WIN_C)
    in_win = (col[None, :] >= c_start[:, None]) & (col[None, :] < c_start[:, None] + NA_WIN_C)
    dc_idx = np.clip(col[None, :] - col[:, None] + NA_WIN_C - 1, 0, 2 * NA_WIN_C - 2)
    qa = np.arange(NA_QR)[:, None]
    kc = np.arange(NA_KR)[None, :]
    row_ok, dr_idx = [], []
    for pat in range(3):
        off = (NA_WIN_R // 2) * pat
        first = (0 * qa, qa, 0 * qa + NA_WIN_R // 2)[pat]
        row_ok.append((kc >= first) & (kc < first + NA_WIN_R))
        dr_idx.append(np.clip(kc - qa - off + NA_WIN_R - 1, 0, 2 * NA_WIN_R - 2))
    sel_c = np.eye(2 * NA_WIN_C - 1, dtype=np.float32)[dc_idx]
    sel_r = np.eye(2 * NA_WIN_R - 1, dtype=np.float32)[np.stack(dr_idx)]
    hp = lax.Precision.HIGHEST
    t = jnp.einsum('hrs,qks->hrqk', rpb, sel_c, precision=hp)
    bias = jnp.einsum('pacr,hrqk->phaqck', sel_r, t, precision=hp)
    ok = jnp.asarray(np.stack(row_ok))[:, None, :, None, :, None] & jnp.asarray(in_win)[None, None, None, :, None, :]
    bias = jnp.where(ok, bias, NEG_INF)
    return bias.reshape(3, NA_HEADS, NA_QB, NA_KR * GRID_W).astype(F32)


def _outproj_kernel(orw_ref, bonus_ref, g_ref, mla_ref, na_ref, x_ref, mod_ref, gnw_ref, gnb_ref, ones_ref,
                    wout_ref, ln1g_ref, ln1b_ref, rwh_ref, rwl_ref, rb_ref,
                    x1_ref, h_ref, lg_ref, *, dn_alpha):
    o = orw_ref[0] + orw_ref[1]
    inv_n = 1.0 / RW_HEAD_DIM
    mu = _group_sum(o, ones_ref) * inv_n
    dlt = o - mu
    var = _group_sum(dlt * dlt, ones_ref) * inv_n
    on = dlt * lax.rsqrt(var + RW_GN_EPS) * gnw_ref[...] + gnb_ref[...]
    rw_y = ((on + bonus_ref[0]) * g_ref[0]).astype(BF16)
    y = (_dot(rw_y, wout_ref[0:RW_DIM, :])
         + _dot(mla_ref[0], wout_ref[RW_DIM:RW_DIM + MLA_DIM, :])
         + _dot(na_ref[0], wout_ref[RW_DIM + MLA_DIM:, :]))
    m = mod_ref[0, 0]
    x1 = _ln(dn_alpha * x_ref[0] + m[2:3] * y) * ln1g_ref[...] + ln1b_ref[...]
    x1_ref[0] = x1
    h = _ln(x1) * (1.0 + m[4:5]) + m[3:4]
    h_ref[0] = _pack_bf16_pair(h)
    lg_ref[0] = _dot3(h, rwh_ref[...], rwl_ref[...]) + rb_ref[...]


def _out_proj(o_rw, bonus, g, mla_o, na_o, xa, mod6, prm, t0, nct, dn_alpha):
    B, S, D = xa.shape
    nt = S // TM - t0
    So = nt * TM
    kind = lambda t: (t + t0 >= nct).astype(jnp.int32)
    full = lambda shape: pl.BlockSpec(shape, lambda b, t: (0,) * len(shape))
    off = lambda C: pl.BlockSpec((1, TM, C), lambda b, t: (b, t + t0, 0))
    own = lambda C: pl.BlockSpec((1, TM, C), lambda b, t: (b, t, 0))
    return pl.pallas_call(
        functools.partial(_outproj_kernel, dn_alpha=dn_alpha),
        grid=(B, nt),
        in_specs=[
            pl.BlockSpec((2, TM, RW_DIM), lambda b, t: (0, t + t0, b)),
            off(RW_DIM), off(RW_DIM), own(MLA_DIM), own(NA_DIM), off(D),
            pl.BlockSpec((1, 1, 6, D), lambda b, t: (b, kind(t), 0, 0)),
            full((1, RW_DIM)), full((1, RW_DIM)), full((RW_DIM, RW_DIM)),
            full((D, D)), full((1, D)), full((1, D)),
            full((D, 128)), full((D, 128)), full((1, 128)),
        ],
        out_specs=[own(D), own(D // 2), own(128)],
        out_shape=[
            jax.ShapeDtypeStruct((B, So, D), F32),
            jax.ShapeDtypeStruct((B, So, D // 2), jnp.int32),
            jax.ShapeDtypeStruct((B, So, 128), F32),
        ],
        compiler_params=_cparams(("arbitrary", "arbitrary")),
        name="out_proj",
    )(o_rw, bonus, g, mla_o, na_o, xa, mod6, prm["gn_w"], prm["gn_b"], prm["ones"],
      prm["w_out"], prm["ln1_g"], prm["ln1_b"], prm["router_hi"], prm["router_lo"], prm["router_b"])


GU_CHUNK = 256


def _moe_kernel(be_ref, nu_ref, x_ref, wgu_ref, bgu_ref, wdn_ref, bdn_ref, perm_ref, y_ref,
                wgu_s, wdn_s, act_s):
    i = pl.program_id(0)
    F = wdn_ref.shape[2]
    n_chunks = 2 * F // GU_CHUNK
    half = GU_CHUNK // 2
    valid = i < nu_ref[0]
    e = be_ref[i]
    new_expert = jnp.logical_or(i == 0, e != be_ref[jnp.maximum(i - 1, 0)])

    @pl.when(jnp.logical_and(valid, new_expert))
    def _():
        for c in range(n_chunks):
            cs = slice(c * GU_CHUNK, (c + 1) * GU_CHUNK)
            wgu_s[:, cs] = _dot(wgu_ref[0, 0, :, cs].astype(BF16), perm_ref[...]).astype(BF16)
        wdn_s[...] = wdn_ref[0, 0].astype(BF16)

    @pl.when(valid)
    def _():
        x = _unpack_bf16_pair(x_ref[...]).astype(BF16)
        for c in range(n_chunks):
            cs = slice(c * GU_CHUNK, (c + 1) * GU_CHUNK)
            gu = _dot(x, wgu_s[:, cs]) + bgu_ref[0, :, cs]
            glu = jnp.minimum(gu[:, :half], SWIGLU_LIMIT)
            lin = jnp.clip(gu[:, half:], -SWIGLU_LIMIT, SWIGLU_LIMIT)
            act_s[:, c * half:(c + 1) * half] = (glu * _sigmoid(SWIGLU_ALPHA * glu) * (lin + 1.0)).astype(BF16)
        y = _dot(act_s[...], wdn_s[...]) + bdn_ref[0]
        y_ref[...] = _pack_bf16_pair(y)

    @pl.when(jnp.logical_not(valid))
    def _():
        y_ref[...] = jnp.zeros_like(y_ref)


def _moe_kernel_into(be_ref, nu_ref, x_ref, wgu_ref, bgu_ref, wdn_ref, bdn_ref, perm_ref, yprev_ref, y_ref,
                     wgu_s, wdn_s, act_s):
    del yprev_ref
    _moe_kernel(be_ref, nu_ref, x_ref, wgu_ref, bgu_ref, wdn_ref, bdn_ref, perm_ref, y_ref, wgu_s, wdn_s, act_s)


def _moe_ffn(x_part, block_e, n_used, w_gu, b_gu, w_dn, b_dn, layer, blk0, n_pad, y_prev):
    D = w_gu.shape[2]
    Dp = D // 2
    _, E, _, F2 = w_gu.shape
    F = F2 // 2
    n_blocks = x_part.shape[0] // MOE_BLK
    half = GU_CHUNK // 2
    src = np.concatenate([2 * np.arange(half), 2 * np.arange(half) + 1])
    perm = jnp.asarray(np.eye(GU_CHUNK, dtype=np.float32)[:, src], BF16)
    bgu_p = jnp.swapaxes(b_gu.reshape(E, F2 // GU_CHUNK, half, 2), 2, 3).reshape(E, 1, F2)
    grid_spec = pltpu.PrefetchScalarGridSpec(
        num_scalar_prefetch=2,
        grid=(n_blocks,),
        in_specs=[
            pl.BlockSpec((MOE_BLK, Dp), lambda i, be, nu: (i, 0)),
            pl.BlockSpec((1, 1, D, F2), lambda i, be, nu: (layer, be[i], 0, 0)),
            pl.BlockSpec((1, 1, F2), lambda i, be, nu: (be[i], 0, 0)),
            pl.BlockSpec((1, 1, F, D), lambda i, be, nu: (layer, be[i], 0, 0)),
            pl.BlockSpec((1, 1, D), lambda i, be, nu: (be[i], 0, 0)),
            pl.BlockSpec((GU_CHUNK, GU_CHUNK), lambda i, be, nu: (0, 0)),
        ] + ([] if y_prev is None else [pl.BlockSpec(memory_space=pl.ANY)]),
        out_specs=pl.BlockSpec((MOE_BLK, Dp), lambda i, be, nu: (i + blk0, 0)),
        scratch_shapes=[pltpu.VMEM((D, F2), BF16), pltpu.VMEM((F, D), BF16), pltpu.VMEM((MOE_BLK, F), BF16)],
    )
    args = (block_e, n_used, x_part, w_gu, bgu_p, w_dn, b_dn.reshape(E, 1, D), perm)
    return pl.pallas_call(
        _moe_kernel if y_prev is None else _moe_kernel_into,
        grid_spec=grid_spec,
        out_shape=jax.ShapeDtypeStruct((n_pad, Dp), jnp.int32),
        input_output_aliases={} if y_prev is None else {len(args): 0},
        compiler_params=_cparams(("arbitrary",)),
        name="moe_ffn",
    )(*args, *(() if y_prev is None else (y_prev,)))


SC_CORES = 2
SC_SUBCORES = 16
SC_WORKERS = SC_CORES * SC_SUBCORES
SC_ROWS = 64
SC_NBUF = 2


def _sc_gather(table, idx):
    V, W = table.shape
    n = idx.shape[0]
    assert n % (SC_WORKERS * SC_ROWS * SC_NBUF) == 0
    steps = n // (SC_WORKERS * SC_ROWS)
    idx3 = idx.reshape(SC_WORKERS, steps, SC_ROWS)
    mesh = plsc.VectorSubcoreMesh(core_axis_name="c", subcore_axis_name="s")

    @functools.partial(
        pl.kernel, mesh=mesh,
        out_type=jax.ShapeDtypeStruct((n, W), table.dtype),
        scratch_types=[pltpu.VMEM((steps, SC_ROWS), jnp.int32),
                       pltpu.VMEM((SC_NBUF, SC_ROWS, W), table.dtype),
                       pltpu.SemaphoreType.DMA((SC_NBUF,)),
                       pltpu.SemaphoreType.DMA((SC_NBUF,))],
        name="sc_row_gather",
    )
    def gather_kernel(table_hbm, idx_hbm, out_hbm, idx_v, rows_v, sem_g, sem_o):
        wid = lax.axis_index("s") * SC_CORES + lax.axis_index("c")
        step0 = wid * steps
        pltpu.sync_copy(idx_hbm.at[wid], idx_v)

        def gather(j, b):
            return pltpu.make_async_copy(table_hbm.at[idx_v.at[j + b]], rows_v.at[b], sem_g.at[b])

        def put(j, b):
            row0 = pl.multiple_of((step0 + j + b) * SC_ROWS, SC_ROWS)
            return pltpu.make_async_copy(rows_v.at[b], out_hbm.at[pl.ds(row0, SC_ROWS)], sem_o.at[b])

        @pl.loop(0, steps, step=SC_NBUF)
        def _(j):
            for b in range(SC_NBUF):
                gather(j, b).start()
            for b in range(SC_NBUF):
                gather(j, b).wait()
                put(j, b).start()
            for b in range(SC_NBUF):
                put(j, b).wait()

    return gather_kernel(table, idx3)


SC_SROWS = 128
SC_SWORDS = 128


def _sc_scatter(vals, idx, n_out):
    n, W = vals.shape
    assert W == SC_SWORDS and n % (SC_WORKERS * SC_SROWS) == 0
    steps = n // (SC_WORKERS * SC_SROWS)
    idx3 = idx.reshape(SC_WORKERS, steps, SC_SROWS)
    vals4 = vals.reshape(SC_WORKERS, steps, SC_SROWS, W)
    mesh = plsc.VectorSubcoreMesh(core_axis_name="c", subcore_axis_name="s")

    @functools.partial(
        pl.kernel, mesh=mesh,
        out_type=jax.ShapeDtypeStruct((n_out, W), vals.dtype),
        scratch_types=[pltpu.VMEM((steps, SC_SROWS), jnp.int32),
                       pltpu.VMEM((SC_SROWS, W), vals.dtype)],
        name="sc_row_scatter",
    )
    def scatter_kernel(vals_hbm, idx_hbm, out_hbm, idx_v, rows_v):
        wid = lax.axis_index("s") * SC_CORES + lax.axis_index("c")
        pltpu.sync_copy(idx_hbm.at[wid], idx_v)

        @pl.loop(0, steps)
        def _(j):
            pltpu.sync_copy(vals_hbm.at[wid, j], rows_v)
            pltpu.sync_copy(rows_v, out_hbm.at[idx_v.at[j]])

    return scatter_kernel(vals4, idx3)


def _router_kernel(lg_ref, tri_ref, meta_ref, cnt_ref, base_s):
    i = pl.program_id(0)

    @pl.when(i == 0)
    def _():
        base_s[...] = jnp.zeros_like(base_s)

    lg = lg_ref[...]
    lane = lax.broadcasted_iota(jnp.int32, lg.shape, 1)
    lg = jnp.where(lane < N_EXPERTS, lg, -jnp.inf)
    vals, ids, hots = [], [], []
    for _ in range(TOP_K):
        m = jnp.max(lg, axis=-1, keepdims=True)
        idx = jnp.min(jnp.where(lg == m, lane, lg.shape[1]), axis=-1, keepdims=True)
        hot = lane == idx
        vals.append(m)
        ids.append(idx)
        hots.append(hot)
        lg = jnp.where(hot, -jnp.inf, lg)
    es = [jnp.exp(v - vals[0]) for v in vals]
    den = es[0] + es[1] + es[2] + es[3]
    onehot = hots[0].astype(F32) + hots[1].astype(F32) + hots[2].astype(F32) + hots[3].astype(F32)
    before = base_s[...] + _dot(tri_ref[...], onehot.astype(BF16))
    meta = jnp.zeros(lg.shape, F32)
    for k in range(TOP_K):
        rank = jnp.sum(jnp.where(hots[k], before, 0.0), axis=-1, keepdims=True)
        meta = jnp.where(lane == k, ids[k].astype(F32), meta)
        meta = jnp.where(lane == TOP_K + k, rank, meta)
        meta = jnp.where(lane == 2 * TOP_K + k, es[k] / den, meta)
    meta_ref[...] = meta
    base_s[...] = base_s[...] + jnp.sum(onehot, axis=0, keepdims=True)
    cnt_ref[...] = base_s[...]


def _router(logits):
    n_tok, W = logits.shape
    tri = jnp.asarray(np.tril(np.ones((TM, TM), np.float32), -1), BF16)
    return pl.pallas_call(
        _router_kernel,
        grid=(n_tok // TM,),
        in_specs=[pl.BlockSpec((TM, W), lambda i: (i, 0)), pl.BlockSpec((TM, TM), lambda i: (0, 0))],
        out_specs=[pl.BlockSpec((TM, W), lambda i: (i, 0)), pl.BlockSpec((1, W), lambda i: (0, 0))],
        out_shape=[jax.ShapeDtypeStruct((n_tok, W), F32), jax.ShapeDtypeStruct((1, W), F32)],
        scratch_shapes=[pltpu.VMEM((1, W), F32)],
        compiler_params=_cparams(("arbitrary",)),
        name="router",
    )(logits, tri)


def _route(logits, n_tok):
    meta, cnt = _router(logits)
    top_i = meta[:, 0:TOP_K].astype(jnp.int32)
    rank = meta[:, TOP_K:2 * TOP_K].astype(jnp.int32)
    gates = meta[:, 2 * TOP_K:3 * TOP_K]
    n_assign = n_tok * TOP_K
    counts = cnt[0, :N_EXPERTS].astype(jnp.int32)
    padded = (counts + MOE_BLK - 1) // MOE_BLK * MOE_BLK
    ends_p = jnp.cumsum(padded)
    g_start = ends_p - padded
    sel = top_i[:, :, None] == jnp.arange(N_EXPERTS, dtype=jnp.int32)[None, None, :]
    dest = jnp.sum(jnp.where(sel, g_start[None, None, :], 0), axis=-1) + rank
    n_blocks = -(-(n_assign + N_EXPERTS * (MOE_BLK - 1)) // MOE_BLK)
    n_pad = n_blocks * MOE_BLK
    blk_start = jnp.arange(n_blocks, dtype=jnp.int32) * MOE_BLK
    block_e = jnp.minimum(jnp.sum((ends_p[None, :] <= blk_start[:, None]).astype(jnp.int32), axis=1),
                          N_EXPERTS - 1).astype(jnp.int32)
    n_used = (ends_p[-1] // MOE_BLK).astype(jnp.int32).reshape(1)
    tok_of = jnp.broadcast_to(jnp.arange(n_tok, dtype=jnp.int32)[:, None, None], (n_tok, TOP_K, SC_SWORDS))
    slot_tok = _sc_scatter(tok_of.reshape(n_assign, SC_SWORDS), dest.reshape(-1), n_pad)[:, 0]
    tok_pad = jnp.clip(slot_tok, 0, n_tok - 1)
    return gates, dest, tok_pad, block_e, n_used


def _final_kernel(x1_ref, yk_ref, gate_ref, mod_ref, g_ref, b_ref, o_ref, *, dn_alpha):
    D = x1_ref.shape[2]
    m = mod_ref[0, 0]
    gates = gate_ref[0]
    f = _unpack_bf16_pair(yk_ref[0, 0]) * gates[:, 0:1]
    for k in range(1, TOP_K):
        f = f + _unpack_bf16_pair(yk_ref[k, 0]) * gates[:, k:k + 1]
    o_ref[0] = _ln(dn_alpha * x1_ref[0] + m[5:6] * f) * g_ref[...] + b_ref[...]


def _final(x1, yk, gates, mod6, ln_g, ln_b, t0, nct, dn_alpha):
    B, So, D = x1.shape
    kind = lambda t: (t + t0 >= nct).astype(jnp.int32)
    blk = pl.BlockSpec((1, TM, D), lambda b, t: (b, t, 0))
    vec = pl.BlockSpec((1, D), lambda b, t: (0, 0))
    return pl.pallas_call(
        functools.partial(_final_kernel, dn_alpha=dn_alpha),
        grid=(B, So // TM),
        in_specs=[blk,
                  pl.BlockSpec((TOP_K, 1, TM, D // 2), lambda b, t: (0, b, t, 0)),
                  pl.BlockSpec((1, TM, TOP_K), lambda b, t: (b, t, 0)),
                  pl.BlockSpec((1, 1, 6, D), lambda b, t: (b, kind(t), 0, 0)), vec, vec],
        out_specs=blk,
        out_shape=jax.ShapeDtypeStruct((B, So, D), F32),
        compiler_params=_cparams(("arbitrary", "arbitrary")),
        name="ffn_residual",
    )(x1, yk, gates, mod6, ln_g, ln_b)


def _rope_tables(n_ctx, T):
    t = jnp.arange(T)
    row = (t // GRID_W).astype(F32)
    col = (t % GRID_W).astype(F32)
    n_freq = MLA_ROPE // 4
    inv = ROPE_BASE ** (-jnp.arange(n_freq, dtype=F32) / n_freq)
    ang = jnp.concatenate([row[:, None] * inv, col[:, None] * inv], axis=-1)
    cos, sin = jnp.cos(ang), jnp.sin(ang)
    z = jnp.zeros((T, 128 - MLA_ROPE), F32)
    cos128 = jnp.concatenate([cos, cos, z], axis=-1)
    sin128 = jnp.concatenate([-sin, sin, z], axis=-1)
    cos_c = jnp.concatenate([jnp.ones((n_ctx, MLA_ROPE), F32), jnp.zeros((n_ctx, 128 - MLA_ROPE), F32)], -1)
    sin_c = jnp.zeros((n_ctx, 128), F32)
    return jnp.concatenate([cos_c, cos128], 0), jnp.concatenate([sin_c, sin128], 0)


def _rope_slabs(w):
    ev, od = w[:, 0::2], w[:, 1::2]
    z = jnp.zeros((w.shape[0], 128 - MLA_ROPE), w.dtype)
    return jnp.concatenate([ev, od, z, od, ev, z], axis=-1)


def _blockdiag2(m):
    z = jnp.zeros_like(m[0])
    return jnp.concatenate([jnp.concatenate([m[0], z], 1), jnp.concatenate([z, m[1]], 1)], 0)


def kernel(x, c, ctx, c_ctx, ada_w, ada_b, w_in, rw_mu, rw_w0, rw_w2, rw_a0, rw_a2, rw_g2, rw_kk, rw_ka, rw_rk, rw_gn_w, rw_gn_b, mla_q_norm, mla_kv_norm, mla_w_uq, mla_w_ukv, na_rpb, w_out, ln1_g, ln1_b, router_w, router_b, w_gu, b_gu, w_dn, b_dn, ln2_g, ln2_b):
    B, T, D = x.shape
    n_ctx = ctx.shape[1]
    depth = ada_w.shape[0]
    S = n_ctx + T
    assert n_ctx % TM == 0 and T % TM == 0 and T % GRID_W == 0 and T // GRID_W >= NA_WIN_R
    nct = n_ctx // TM
    BH = B * RW_HEADS
    dn_alpha = (2 * depth) ** 0.25
    F = w_dn.shape[2]

    R = (B + 1 + 7) // 8 * 8
    cond = jnp.zeros((R, D), F32).at[:B].set(c).at[B].set(c_ctx)
    mod = _ada_mod(cond, ada_w, ada_b)
    mod_l = mod[:, :B].reshape(depth, B, 1, 6, D)
    mod_c = jnp.broadcast_to(mod[:, B].reshape(depth, 1, 1, 6, D), (depth, B, 1, 6, D))
    mod6 = jnp.concatenate([mod_c, mod_l], axis=2)

    cos128, sin128 = _rope_tables(n_ctx, T)
    ones_blk = jnp.kron(jnp.eye(RW_HEADS, dtype=F32), jnp.ones((RW_HEAD_DIM, RW_HEAD_DIM), F32)).astype(BF16)

    na_bias = [_na_bias_table(na_rpb[l]) for l in range(depth)]

    xa = jnp.concatenate([ctx, x], axis=1)
    for l in range(depth):
        need_ctx = l < depth - 1
        t0 = 0 if need_ctx else nct

        wi = w_in[l]
        c_m = RW_COLS
        w_in_ext = jnp.concatenate(
            [wi[:, :c_m + MLA_Q_LORA + MLA_KV_LORA], _rope_slabs(wi[:, c_m + MLA_Q_LORA + MLA_KV_LORA:c_m + MLA_COLS]),
             wi[:, c_m + MLA_COLS:]], axis=-1).astype(BF16)
        wuq = mla_w_uq[l].reshape(MLA_Q_LORA, MLA_HEADS, MLA_NOPE + MLA_ROPE)
        wuq_ext = jnp.concatenate(
            [jnp.concatenate([wuq[:, h, :MLA_NOPE], _rope_slabs(wuq[:, h, MLA_NOPE:])], -1) for h in range(MLA_HEADS)],
            axis=-1).astype(BF16)
        rw_prm = dict(
            mu=rw_mu[l],
            w0=rw_w0[l].reshape(1, 2 * RW_DIM), w2=_blockdiag2(rw_w2[l]).astype(BF16),
            a0=rw_a0[l].reshape(1, 2 * RW_DIM), a2=_blockdiag2(rw_a2[l]).astype(BF16),
            g2=rw_g2[l].astype(BF16), kk=rw_kk[l].reshape(1, RW_DIM), rk=rw_rk[l].reshape(1, RW_DIM),
            ones=ones_blk)
        mla_prm = dict(q_norm=mla_q_norm[l].reshape(1, -1), kv_norm=mla_kv_norm[l].reshape(1, -1),
                       w_uq=wuq_ext, w_ukv=mla_w_ukv[l].astype(BF16))
        rt = jnp.zeros((D, 128), F32).at[:, :N_EXPERTS].set(router_w[l])
        rt_hi = rt.astype(BF16)
        out_prm = dict(gn_w=rw_gn_w[l].reshape(1, -1), gn_b=rw_gn_b[l].reshape(1, -1), ones=ones_blk,
                       w_out=w_out[l].astype(BF16), ln1_g=ln1_g[l].reshape(1, -1), ln1_b=ln1_b[l].reshape(1, -1),
                       router_hi=rt_hi, router_lo=(rt - rt_hi.astype(F32)).astype(BF16),
                       router_b=jnp.zeros((1, 128), F32).at[0, :N_EXPERTS].set(router_b[l]))

        p_rw, p_mla, p_na = _in_proj(xa, mod6[l], w_in_ext, nct)
        feat, g_gate, bonus = _rw_features(p_rw, rw_prm, nct)
        ft = jnp.swapaxes(feat.reshape(S, BH, 8 * RW_HEAD_DIM), 1, 2).reshape(S, 8, RW_HEAD_DIM, BH)
        ka_t = jnp.tile(rw_ka[l].reshape(RW_HEADS, RW_HEAD_DIM).T[:, None, :], (1, B, 1)).reshape(RW_HEAD_DIM, BH)
        q, k, v = _mla_prep(p_mla, cos128, sin128, mla_prm)
        mla_o = _mla_attn(q, k, v, t0, nct, n_ctx)
        na_o = _na_attn(p_na, na_bias[l], n_ctx, need_ctx)

        o_scan = _rw_scan(ft, ka_t, n_ctx, mla_o)
        o_rw = jnp.swapaxes(o_scan, 2, 3).reshape(2, S, B * RW_DIM)

        x1, h, logits = _out_proj(o_rw, bonus, g_gate, mla_o, na_o, xa, mod6[l], out_prm, t0, nct, dn_alpha)

        So = x1.shape[1]
        n_tok = B * So
        gates, dest, tok_pad, block_e, n_used = _route(logits.reshape(n_tok, 128), n_tok)
        n_pad = tok_pad.shape[0]
        n_blocks = n_pad // MOE_BLK
        parts = MOE_PARTS if n_blocks % MOE_PARTS == 0 else 1
        pb = n_blocks // parts
        h2 = h.reshape(n_tok, D // 2)
        y_sorted = None
        for c in range(parts):
            x_part = _sc_gather(h2, tok_pad[c * pb * MOE_BLK:(c + 1) * pb * MOE_BLK])
            y_sorted = _moe_ffn(x_part, block_e[c * pb:(c + 1) * pb], jnp.clip(n_used - c * pb, 0, pb),
                                w_gu, b_gu[l], w_dn, b_dn[l], l, c * pb, n_pad, y_sorted)
        yk = _sc_gather(y_sorted, dest.T.reshape(-1)).reshape(TOP_K, B, So, D // 2)

        xa = _final(x1, yk, gates.reshape(B, So, TOP_K), mod6[l], ln2_g[l].reshape(1, -1), ln2_b[l].reshape(1, -1),
                    t0, nct, dn_alpha)
    return xa
```

```python
import functools

import numpy as np
import jax
import jax.numpy as jnp
from jax import lax
from jax.experimental import pallas as pl
from jax.experimental.pallas import tpu as pltpu
from jax.experimental.pallas import tpu_sc as plsc

F32 = jnp.float32
BF16 = jnp.bfloat16

GRID_W = 64
RW_HEAD_DIM = 64
RW_HEADS = 4
RW_DIM = RW_HEADS * RW_HEAD_DIM
RW_LORA = 64
RW_G_LORA = 128
RW_GN_EPS = 64e-5
RW_COLS = 3 * RW_DIM + 4 * RW_LORA + RW_G_LORA
MLA_HEADS = 4
MLA_NOPE = 128
MLA_ROPE = 64
MLA_V = 128
MLA_Q_LORA = 256
MLA_KV_LORA = 128
MLA_COLS = MLA_Q_LORA + MLA_KV_LORA + MLA_ROPE
MLA_COLS_EXT = MLA_Q_LORA + MLA_KV_LORA + 256
MLA_DIM = MLA_HEADS * MLA_V
NA_HEADS = 4
NA_HEAD_DIM = 64
NA_DIM = NA_HEADS * NA_HEAD_DIM
NA_WIN_R = 8
NA_WIN_C = 16
NA_COLS = 3 * NA_DIM
ROPE_BASE = 10000.0
N_EXPERTS = 32
TOP_K = 4
SWIGLU_ALPHA = 1.702
SWIGLU_LIMIT = 7.0
NEG_INF = -1e30

TM = 256
SCAN_TB = 16
SCAN_VC = 64
MOE_BLK = 512
MOE_PARTS = 4
VMEM_LIMIT = 56 * 1024 * 1024


def _cparams(sem):
    return pltpu.CompilerParams(dimension_semantics=sem, vmem_limit_bytes=VMEM_LIMIT)


def _ln(x, eps=1e-5):
    mu = jnp.mean(x, axis=-1, keepdims=True)
    d = x - mu
    var = jnp.mean(d * d, axis=-1, keepdims=True)
    return d * lax.rsqrt(var + eps)


def _dot(a, b):
    return jnp.dot(a, b, preferred_element_type=F32)


def _split(a):
    hi = a.astype(BF16)
    lo = (a - hi.astype(F32)).astype(BF16)
    return hi, lo


def _dot_hl(a, b_bf16):
    hi, lo = _split(a)
    return _dot(hi, b_bf16) + _dot(lo, b_bf16)


def _dot3(a, b_hi, b_lo):
    hi, lo = _split(a)
    return _dot(hi, b_hi) + _dot(lo, b_hi) + _dot(hi, b_lo)


def _sigmoid(x):
    return 1.0 / (1.0 + jnp.exp(-x))


def _pack_bf16_pair(x):
    w = x.shape[1] // 2
    u = pltpu.bitcast(x.astype(BF16).astype(F32), jnp.uint32)
    lo = lax.shift_right_logical(u[:, :w], jnp.uint32(16))
    hi = lax.bitwise_and(u[:, w:], jnp.uint32(0xFFFF0000))
    return pltpu.bitcast(lax.bitwise_or(lo, hi), jnp.int32)


def _unpack_bf16_pair(p):
    u = pltpu.bitcast(p, jnp.uint32)
    lo = pltpu.bitcast(lax.shift_left(u, jnp.uint32(16)), F32)
    hi = pltpu.bitcast(lax.bitwise_and(u, jnp.uint32(0xFFFF0000)), F32)
    return jnp.concatenate([lo, hi], axis=-1)


def _ada_kernel(cond_ref, w_ref, b_ref, o_ref):
    c = cond_ref[...]
    s = c * _sigmoid(c)
    w = w_ref[0]
    w_hi, w_lo = _split(w)
    o_ref[0] = _dot3(s, w_hi, w_lo) + b_ref[0]


def _ada_mod(cond, ada_w, ada_b):
    L, D, N = ada_w.shape
    R = cond.shape[0]
    tn = 512
    return pl.pallas_call(
        _ada_kernel,
        grid=(L, N // tn),
        in_specs=[
            pl.BlockSpec((R, D), lambda l, j: (0, 0)),
            pl.BlockSpec((1, D, tn), lambda l, j: (l, 0, j)),
            pl.BlockSpec((1, 1, tn), lambda l, j: (l, 0, j)),
        ],
        out_specs=pl.BlockSpec((1, R, tn), lambda l, j: (l, 0, j)),
        out_shape=jax.ShapeDtypeStruct((L, R, N), F32),
        compiler_params=_cparams(("arbitrary", "arbitrary")),
        name="ada_mod",
    )(cond, ada_w, ada_b.reshape(L, 1, N))


def _win_kernel(x_ref, mod_ref, w_ref, prw_ref, pmla_ref, pna_ref):
    x = x_ref[0]
    m = mod_ref[0, 0]
    xm = _ln(x) * (1.0 + m[1:2]) + m[0:1]
    p = _dot(xm.astype(BF16), w_ref[...])
    prw_ref[0] = p[:, :RW_COLS]
    pmla_ref[0] = p[:, RW_COLS:RW_COLS + MLA_COLS_EXT]
    pna_ref[0] = p[:, RW_COLS + MLA_COLS_EXT:].astype(BF16)


def _in_proj(xa, mod6, w_in_ext, nct):
    B, S, D = xa.shape
    NC = w_in_ext.shape[1]
    kind = lambda t: (t >= nct).astype(jnp.int32)
    return pl.pallas_call(
        _win_kernel,
        grid=(B, S // TM),
        in_specs=[
            pl.BlockSpec((1, TM, D), lambda b, t: (b, t, 0)),
            pl.BlockSpec((1, 1, 6, D), lambda b, t: (b, kind(t), 0, 0)),
            pl.BlockSpec((D, NC), lambda b, t: (0, 0)),
        ],
        out_specs=[
            pl.BlockSpec((1, TM, RW_COLS), lambda b, t: (b, t, 0)),
            pl.BlockSpec((1, TM, MLA_COLS_EXT), lambda b, t: (b, t, 0)),
            pl.BlockSpec((1, TM, NA_COLS), lambda b, t: (b, t, 0)),
        ],
        out_shape=[
            jax.ShapeDtypeStruct((B, S, RW_COLS), F32),
            jax.ShapeDtypeStruct((B, S, MLA_COLS_EXT), F32),
            jax.ShapeDtypeStruct((B, S, NA_COLS), BF16),
        ],
        compiler_params=_cparams(("arbitrary", "arbitrary")),
        name="in_proj",
    )(xa, mod6, w_in_ext)


def _group_sum(x, ones_ref):
    return _dot_hl(x, ones_ref[...])


def _rwfeat_kernel(p_ref, pp_ref, pn_ref, mu_ref, w0_ref, w2_ref, a0_ref, a2_ref, g2_ref,
                   kk_ref, rk_ref, ones_ref, f_ref, g_ref, bonus_ref, *, nct, nt):
    t = pl.program_id(1)
    p = p_ref[0]
    first = jnp.logical_or(t == 0, t == nct)
    last = jnp.logical_or(t == nct - 1, t == nt - 1)
    prev_row = jnp.where(first, 0.0, pp_ref[0, 7:8, :])
    next_row = jnp.where(last, 0.0, pn_ref[0, 0:1, :])
    rows = lax.broadcasted_iota(jnp.int32, p.shape, 0)
    prev = jnp.where(rows == 0, prev_row, pltpu.roll(p, 1, axis=0))
    nxt = jnp.where(rows == TM - 1, next_row, pltpu.roll(p, TM - 1, axis=0))
    mu = mu_ref[...]
    xs = p + mu[0:1] * (prev - p) + mu[1:2] * (nxt - p)

    D3 = 3 * RW_DIM
    r = xs[:, 0:RW_DIM]
    k = xs[:, RW_DIM:2 * RW_DIM]
    v = xs[:, 2 * RW_DIM:D3]
    w_lo = xs[:, D3:D3 + 2 * RW_LORA]
    a_lo = xs[:, D3 + 2 * RW_LORA:D3 + 4 * RW_LORA]
    g_pre = xs[:, D3 + 4 * RW_LORA:]

    lw = _dot(jnp.tanh(w_lo).astype(BF16), w2_ref[...]) + w0_ref[...]
    logw = jnp.minimum(lw, 0.0) - jnp.log(1.0 + jnp.exp(-jnp.abs(lw))) - 0.5
    decay = jnp.exp(-jnp.exp(logw))
    a = _sigmoid(_dot(a_lo.astype(BF16), a2_ref[...]) + a0_ref[...])

    kkr = k * kk_ref[...]
    kk = kkr * lax.rsqrt(_group_sum(kkr * kkr, ones_ref) + 1e-12)
    g_ref[0] = _dot(_sigmoid(g_pre).astype(BF16), g2_ref[...])
    bonus_ref[0] = _group_sum(r * k * rk_ref[...], ones_ref) * v

    comps = (r, k, v, kk, decay[:, :RW_DIM], a[:, :RW_DIM], decay[:, RW_DIM:], a[:, RW_DIM:])
    N = RW_HEAD_DIM
    for h in range(RW_HEADS):
        for ci, comp in enumerate(comps):
            col = (h * len(comps) + ci) * N
            f_ref[:, col:col + N] = comp[:, h * N:(h + 1) * N]


def _rw_features(p_rw, prm, nct):
    B, S, C = p_rw.shape
    nt = S // TM
    hb = TM // 8
    last_hb = S // 8 - 1
    full = lambda shape: pl.BlockSpec(shape, lambda b, t: (0,) * len(shape))
    return pl.pallas_call(
        functools.partial(_rwfeat_kernel, nct=nct, nt=nt),
        grid=(B, nt),
        in_specs=[
            pl.BlockSpec((1, TM, C), lambda b, t: (b, t, 0)),
            pl.BlockSpec((1, 8, C), lambda b, t: (b, jnp.maximum(t * hb - 1, 0), 0)),
            pl.BlockSpec((1, 8, C), lambda b, t: (b, jnp.minimum((t + 1) * hb, last_hb), 0)),
            full((2, C)),
            full((1, 2 * RW_DIM)), full((2 * RW_LORA, 2 * RW_DIM)),
            full((1, 2 * RW_DIM)), full((2 * RW_LORA, 2 * RW_DIM)),
            full((RW_G_LORA, RW_DIM)),
            full((1, RW_DIM)), full((1, RW_DIM)),
            full((RW_DIM, RW_DIM)),
        ],
        out_specs=[
            pl.BlockSpec((TM, 8 * RW_DIM), lambda b, t: (t, b)),
            pl.BlockSpec((1, TM, RW_DIM), lambda b, t: (b, t, 0)),
            pl.BlockSpec((1, TM, RW_DIM), lambda b, t: (b, t, 0)),
        ],
        out_shape=[
            jax.ShapeDtypeStruct((S, B * 8 * RW_DIM), F32),
            jax.ShapeDtypeStruct((B, S, RW_DIM), F32),
            jax.ShapeDtypeStruct((B, S, RW_DIM), F32),
        ],
        compiler_params=_cparams(("arbitrary", "arbitrary")),
        name="rw_features",
    )(p_rw, p_rw, p_rw, prm["mu"], prm["w0"], prm["w2"], prm["a0"], prm["a2"], prm["g2"],
      prm["kk"], prm["rk"], prm["ones"])


def _scan_kernel(fs_ref, fd_ref, ka_ref, after_ref, o_ref, s_ref, tmp_ref, *, tb):
    del after_ref
    d = pl.program_id(0)
    g = pl.program_id(1)
    N = RW_HEAD_DIM

    @pl.when(g == 0)
    def _():
        s_ref[...] = jnp.zeros_like(s_ref)

    ka = ka_ref[...]

    def step(i, carry):
        tt = jnp.where(d == 0, i, tb - 1 - i)
        r = fs_ref[tt, 0]
        k = fs_ref[tt, 1]
        kk = fs_ref[tt, 3]
        w = fd_ref[tt, 0]
        a = fd_ref[tt, 1]
        b = a * kk
        kd = k * (1.0 + (a - 1.0) * ka)
        wr = w * r
        br = jnp.sum(b * r, axis=0, keepdims=True)
        kr = jnp.sum(kd * r, axis=0, keepdims=True)
        tmp_ref[0] = wr
        tmp_ref[1] = b
        tmp_ref[2] = kd
        for c in range(N // SCAN_VC):
            vs = pl.ds(c * SCAN_VC, SCAN_VC)
            vc = fs_ref[tt, 2, vs, :]
            sa = [jnp.zeros_like(vc), jnp.zeros_like(vc)]
            for j in range(N):
                sa[j % 2] = sa[j % 2] + s_ref[j, vs, :] * fs_ref[tt, 3, pl.ds(j, 1), :]
            sa = sa[0] + sa[1]
            op = [jnp.zeros_like(vc), jnp.zeros_like(vc)]
            for j in range(N):
                sk = s_ref[j, vs, :]
                op[j % 2] = op[j % 2] + sk * tmp_ref[0, pl.ds(j, 1), :]
                s_ref[j, vs, :] = (sk * fd_ref[tt, 0, pl.ds(j, 1), :]
                                   - sa * tmp_ref[1, pl.ds(j, 1), :]
                                   + vc * tmp_ref[2, pl.ds(j, 1), :])
            o_ref[0, tt, vs, :] = (op[0] + op[1]) - sa * br + vc * kr
        return carry

    lax.fori_loop(0, tb, step, 0)


def _rw_scan(ft, ka_t, n_ctx, after):
    S, _, N, BH = ft.shape
    tb = SCAN_TB
    nb = S // tb
    ncb = n_ctx // tb

    def tblk(d, g):
        bwd = jnp.where(g < ncb, ncb - 1 - g, nb - 1 - g + ncb)
        return jnp.where(d == 0, g, bwd)

    return pl.pallas_call(
        functools.partial(_scan_kernel, tb=tb),
        grid=(2, nb),
        in_specs=[
            pl.BlockSpec((tb, 4, N, BH), lambda d, g: (tblk(d, g), 0, 0, 0)),
            pl.BlockSpec((tb, 2, N, BH), lambda d, g: (tblk(d, g), 2 + d, 0, 0)),
            pl.BlockSpec((N, BH), lambda d, g: (0, 0)),
            pl.BlockSpec(memory_space=pl.ANY),
        ],
        out_specs=pl.BlockSpec((1, tb, N, BH), lambda d, g: (d, tblk(d, g), 0, 0)),
        out_shape=jax.ShapeDtypeStruct((2, S, N, BH), F32),
        scratch_shapes=[pltpu.VMEM((N, N, BH), F32), pltpu.VMEM((3, N, BH), F32)],
        compiler_params=_cparams(("arbitrary", "arbitrary")),
        name="rw_scan",
    )(ft, ft, ka_t, after)


def _mlaprep_kernel(p_ref, cos_ref, sin_ref, qn_ref, kvn_ref, wuq_ref, wukv_ref, q_ref, k_ref, v_ref):
    p = p_ref[0]
    cos = cos_ref[...]
    sin = sin_ref[...]
    scale = (MLA_NOPE + MLA_ROPE) ** -0.5

    def rms(x, g):
        return x * lax.rsqrt(jnp.mean(x * x, axis=-1, keepdims=True) + 1e-6) * g

    q = _dot(rms(p[:, :MLA_Q_LORA], qn_ref[...]).astype(BF16), wuq_ref[...])
    kv = _dot(rms(p[:, MLA_Q_LORA:MLA_Q_LORA + MLA_KV_LORA], kvn_ref[...]).astype(BF16), wukv_ref[...])
    c0 = MLA_Q_LORA + MLA_KV_LORA
    kr = (p[:, c0:c0 + 128] * cos + p[:, c0 + 128:c0 + 256] * sin).astype(BF16)
    for h in range(MLA_HEADS):
        qb = h * 384
        q_ref[0, :, h * 256:h * 256 + 128] = (q[:, qb:qb + 128] * scale).astype(BF16)
        q_ref[0, :, h * 256 + 128:h * 256 + 256] = (
            (q[:, qb + 128:qb + 256] * cos + q[:, qb + 256:qb + 384] * sin) * scale).astype(BF16)
        k_ref[0, :, h * 256:h * 256 + 128] = kv[:, h * 256:h * 256 + 128].astype(BF16)
        k_ref[0, :, h * 256 + 128:h * 256 + 256] = kr
        v_ref[0, :, h * 128:(h + 1) * 128] = kv[:, h * 256 + 128:h * 256 + 256].astype(BF16)


def _mla_prep(p_mla, cos128, sin128, prm):
    B, S, C = p_mla.shape
    full = lambda shape: pl.BlockSpec(shape, lambda b, t: (0,) * len(shape))
    H = MLA_HEADS
    return pl.pallas_call(
        _mlaprep_kernel,
        grid=(B, S // TM),
        in_specs=[
            pl.BlockSpec((1, TM, C), lambda b, t: (b, t, 0)),
            pl.BlockSpec((TM, 128), lambda b, t: (t, 0)),
            pl.BlockSpec((TM, 128), lambda b, t: (t, 0)),
            full((1, MLA_Q_LORA)), full((1, MLA_KV_LORA)),
            full((MLA_Q_LORA, H * 384)), full((MLA_KV_LORA, H * 256)),
        ],
        out_specs=[
            pl.BlockSpec((1, TM, H * 256), lambda b, t: (b, t, 0)),
            pl.BlockSpec((1, TM, H * 256), lambda b, t: (b, t, 0)),
            pl.BlockSpec((1, TM, H * 128), lambda b, t: (b, t, 0)),
        ],
        out_shape=[
            jax.ShapeDtypeStruct((B, S, H * 256), BF16),
            jax.ShapeDtypeStruct((B, S, H * 256), BF16),
            jax.ShapeDtypeStruct((B, S, H * 128), BF16),
        ],
        compiler_params=_cparams(("arbitrary", "arbitrary")),
        name="mla_prep",
    )(p_mla, cos128, sin128, prm["q_norm"], prm["kv_norm"], prm["w_uq"], prm["w_ukv"])


def _mla_attn_kernel(q_ref, k_ref, v_ref, o_ref, *, t0, nct, n_ctx):
    t = pl.program_id(1) + t0

    def attend(n_keys):
        for h in range(MLA_HEADS):
            q = q_ref[0, :, h * 256:(h + 1) * 256]
            k = k_ref[0, 0:n_keys, h * 256:(h + 1) * 256]
            s = lax.dot_general(q, k, (((1,), (1,)), ((), ())), preferred_element_type=F32)
            m = jnp.max(s, axis=-1, keepdims=True)
            e = jnp.exp(s - m)
            l = jnp.sum(e, axis=-1, keepdims=True)
            o = _dot(e.astype(BF16), v_ref[0, 0:n_keys, h * 128:(h + 1) * 128])
            o_ref[0, :, h * 128:(h + 1) * 128] = (o / l).astype(BF16)

    S = k_ref.shape[1]
    if t0 < nct:
        @pl.when(t < nct)
        def _():
            attend(n_ctx)

        @pl.when(t >= nct)
        def _():
            attend(S)
    else:
        attend(S)


def _mla_attn(q, k, v, t0, nct, n_ctx):
    B, S, _ = q.shape
    nq = S // TM - t0
    H = MLA_HEADS
    return pl.pallas_call(
        functools.partial(_mla_attn_kernel, t0=t0, nct=nct, n_ctx=n_ctx),
        grid=(B, nq),
        in_specs=[
            pl.BlockSpec((1, TM, H * 256), lambda b, t: (b, t + t0, 0)),
            pl.BlockSpec((1, S, H * 256), lambda b, t: (b, 0, 0)),
            pl.BlockSpec((1, S, H * 128), lambda b, t: (b, 0, 0)),
        ],
        out_specs=pl.BlockSpec((1, TM, H * 128), lambda b, t: (b, t, 0)),
        out_shape=jax.ShapeDtypeStruct((B, nq * TM, H * 128), BF16),
        compiler_params=_cparams(("arbitrary", "arbitrary")),
        name="mla_attn",
    )(q, k, v)


NA_QR = 4
NA_KR = NA_WIN_R + NA_QR
NA_QB = NA_QR * GRID_W


def _na_kernel(p_ref, bias_ref, o_ref, *, n_ctx, rows, with_ctx):
    s_id = pl.program_id(1)
    W = GRID_W
    n_loc = NA_KR * W
    scale = NA_HEAD_DIM ** -0.5
    nq_ctx = n_ctx // NA_QB
    nblk = rows // NA_QR

    def heads_out(q, parts):
        outs = []
        for h in range(NA_HEADS):
            hs = slice(h * NA_HEAD_DIM, (h + 1) * NA_HEAD_DIM)
            qh = q[:, hs] * scale
            ss = []
            for kx, vx, bias in parts:
                s = lax.dot_general(qh, kx[:, hs], (((1,), (1,)), ((), ())), preferred_element_type=F32)
                if bias is not None:
                    s = s + bias(h)
                ss.append(s)
            m = ss[0].max(axis=-1, keepdims=True)
            for s in ss[1:]:
                m = jnp.maximum(m, s.max(axis=-1, keepdims=True))
            acc = 0.0
            l = 0.0
            for s, (kx, vx, bias) in zip(ss, parts):
                e = jnp.exp(s - m)
                l = l + jnp.sum(e, axis=-1, keepdims=True)
                acc = acc + _dot(e.astype(BF16), vx[:, hs])
            outs.append(acc / l)
        return jnp.concatenate(outs, axis=-1).astype(BF16)

    k_c = p_ref[0, 0:n_ctx, NA_DIM:2 * NA_DIM]
    v_c = p_ref[0, 0:n_ctx, 2 * NA_DIM:3 * NA_DIM]

    def lat_block(j):
        i0 = j * NA_QR
        k_start = jnp.clip(i0 - NA_WIN_R // 2, 0, rows - NA_KR)
        pat = jnp.where(j == 0, 0, jnp.where(j == nblk - 1, 2, 1))
        q0 = pl.multiple_of(n_ctx + i0 * W, NA_QB)
        k0 = pl.multiple_of(n_ctx + k_start * W, W)
        q = p_ref[0, pl.ds(q0, NA_QB), 0:NA_DIM]
        k_l = p_ref[0, pl.ds(k0, n_loc), NA_DIM:2 * NA_DIM]
        v_l = p_ref[0, pl.ds(k0, n_loc), 2 * NA_DIM:3 * NA_DIM]
        o_ref[0] = heads_out(q, [(k_l, v_l, lambda h: bias_ref[pat, h]), (k_c, v_c, None)])

    if with_ctx:
        @pl.when(s_id < nq_ctx)
        def _():
            q0 = pl.multiple_of(s_id * NA_QB, NA_QB)
            q = p_ref[0, pl.ds(q0, NA_QB), 0:NA_DIM]
            o_ref[0] = heads_out(q, [(k_c, v_c, None)])

        @pl.when(s_id >= nq_ctx)
        def _():
            lat_block(s_id - nq_ctx)
    else:
        lat_block(s_id)


def _na_attn(p_na, bias_tab, n_ctx, with_ctx):
    B, S, C = p_na.shape
    T = S - n_ctx
    rows = T // GRID_W
    assert rows % NA_QR == 0 and rows >= NA_KR and n_ctx % NA_QB == 0
    nsteps = rows // NA_QR + (n_ctx // NA_QB if with_ctx else 0)
    return pl.pallas_call(
        functools.partial(_na_kernel, n_ctx=n_ctx, rows=rows, with_ctx=with_ctx),
        grid=(B, nsteps),
        in_specs=[
            pl.BlockSpec((1, S, C), lambda b, s: (b, 0, 0)),
            pl.BlockSpec(bias_tab.shape, lambda b, s: (0, 0, 0, 0)),
        ],
        out_specs=pl.BlockSpec((1, NA_QB, NA_DIM), lambda b, s: (b, s, 0)),
        out_shape=jax.ShapeDtypeStruct((B, nsteps * NA_QB, NA_DIM), BF16),
        compiler_params=_cparams(("arbitrary", "arbitrary")),
        name="na_attn",
    )(p_na, bias_tab)


def _na_bias_table(rpb):
    col = np.arange(GRID_W)
    c_start = np.clip(col - NA_WIN_C // 2, 0, GRID_W - NA_WIN_C)
    in_win = (col[None, :] >= c_start[:, None]) & (col[None, :] < c_start[:, None] + NA_WIN_C)
    dc_idx = np.clip(col[None, :] - col[:, None] + NA_WIN_C - 1, 0, 2 * NA_WIN_C - 2)
    qa = np.arange(NA_QR)[:, None]
    kc = np.arange(NA_KR)[None, :]
    row_ok, dr_idx = [], []
    for pat in range(3):
        off = (NA_WIN_R // 2) * pat
        first = (0 * qa, qa, 0 * qa + NA_WIN_R // 2)[pat]
        row_ok.append((kc >= first) & (kc < first + NA_WIN_R))
        dr_idx.append(np.clip(kc - qa - off + NA_WIN_R - 1, 0, 2 * NA_WIN_R - 2))
    sel_c = np.eye(2 * NA_WIN_C - 1, dtype=np.float32)[dc_idx]
    sel_r = np.eye(2 * NA_WIN_R - 1, dtype=np.float32)[np.stack(dr_idx)]
    hp = lax.Precision.HIGHEST
    t = jnp.einsum('hrs,qks->hrqk', rpb, sel_c, precision=hp)
    bias = jnp.einsum('pacr,hrqk->phaqck', sel_r, t, precision=hp)
    ok = jnp.asarray(np.stack(row_ok))[:, None, :, None, :, None] & jnp.asarray(in_win)[None, None, None, :, None, :]
    bias = jnp.where(ok, bias, NEG_INF)
    return bias.reshape(3, NA_HEADS, NA_QB, NA_KR * GRID_W).astype(F32)


def _outproj_kernel(orw_ref, bonus_ref, g_ref, mla_ref, na_ref, x_ref, mod_ref, gnw_ref, gnb_ref, ones_ref,
                    wout_ref, ln1g_ref, ln1b_ref, rwh_ref, rwl_ref, rb_ref,
                    x1_ref, h_ref, lg_ref, *, dn_alpha):
    o = orw_ref[0] + orw_ref[1]
    inv_n = 1.0 / RW_HEAD_DIM
    mu = _group_sum(o, ones_ref) * inv_n
    dlt = o - mu
    var = _group_sum(dlt * dlt, ones_ref) * inv_n
    on = dlt * lax.rsqrt(var + RW_GN_EPS) * gnw_ref[...] + gnb_ref[...]
    rw_y = ((on + bonus_ref[0]) * g_ref[0]).astype(BF16)
    y = (_dot(rw_y, wout_ref[0:RW_DIM, :])
         + _dot(mla_ref[0], wout_ref[RW_DIM:RW_DIM + MLA_DIM, :])
         + _dot(na_ref[0], wout_ref[RW_DIM + MLA_DIM:, :]))
    m = mod_ref[0, 0]
    x1 = _ln(dn_alpha * x_ref[0] + m[2:3] * y) * ln1g_ref[...] + ln1b_ref[...]
    x1_ref[0] = x1
    h = _ln(x1) * (1.0 + m[4:5]) + m[3:4]
    h_ref[0] = _pack_bf16_pair(h)
    lg_ref[0] = _dot3(h, rwh_ref[...], rwl_ref[...]) + rb_ref[...]


def _out_proj(o_rw, bonus, g, mla_o, na_o, xa, mod6, prm, t0, nct, dn_alpha):
    B, S, D = xa.shape
    nt = S // TM - t0
    So = nt * TM
    kind = lambda t: (t + t0 >= nct).astype(jnp.int32)
    full = lambda shape: pl.BlockSpec(shape, lambda b, t: (0,) * len(shape))
    off = lambda C: pl.BlockSpec((1, TM, C), lambda b, t: (b, t + t0, 0))
    own = lambda C: pl.BlockSpec((1, TM, C), lambda b, t: (b, t, 0))
    return pl.pallas_call(
        functools.partial(_outproj_kernel, dn_alpha=dn_alpha),
        grid=(B, nt),
        in_specs=[
            pl.BlockSpec((2, TM, RW_DIM), lambda b, t: (0, t + t0, b)),
            off(RW_DIM), off(RW_DIM), own(MLA_DIM), own(NA_DIM), off(D),
            pl.BlockSpec((1, 1, 6, D), lambda b, t: (b, kind(t), 0, 0)),
            full((1, RW_DIM)), full((1, RW_DIM)), full((RW_DIM, RW_DIM)),
            full((D, D)), full((1, D)), full((1, D)),
            full((D, 128)), full((D, 128)), full((1, 128)),
        ],
        out_specs=[own(D), own(D // 2), own(128)],
        out_shape=[
            jax.ShapeDtypeStruct((B, So, D), F32),
            jax.ShapeDtypeStruct((B, So, D // 2), jnp.int32),
            jax.ShapeDtypeStruct((B, So, 128), F32),
        ],
        compiler_params=_cparams(("arbitrary", "arbitrary")),
        name="out_proj",
    )(o_rw, bonus, g, mla_o, na_o, xa, mod6, prm["gn_w"], prm["gn_b"], prm["ones"],
      prm["w_out"], prm["ln1_g"], prm["ln1_b"], prm["router_hi"], prm["router_lo"], prm["router_b"])


GU_CHUNK = 256


def _moe_kernel(be_ref, nu_ref, x_ref, wgu_ref, bgu_ref, wdn_ref, bdn_ref, perm_ref, y_ref,
                wgu_s, wdn_s, act_s):
    i = pl.program_id(0)
    F = wdn_ref.shape[2]
    n_chunks = 2 * F // GU_CHUNK
    half = GU_CHUNK // 2
    valid = i < nu_ref[0]
    e = be_ref[i]
    new_expert = jnp.logical_or(i == 0, e != be_ref[jnp.maximum(i - 1, 0)])

    @pl.when(jnp.logical_and(valid, new_expert))
    def _():
        for c in range(n_chunks):
            cs = slice(c * GU_CHUNK, (c + 1) * GU_CHUNK)
            wgu_s[:, cs] = _dot(wgu_ref[0, 0, :, cs].astype(BF16), perm_ref[...]).astype(BF16)
        wdn_s[...] = wdn_ref[0, 0].astype(BF16)

    @pl.when(valid)
    def _():
        x = _unpack_bf16_pair(x_ref[...]).astype(BF16)
        for c in range(n_chunks):
            cs = slice(c * GU_CHUNK, (c + 1) * GU_CHUNK)
            gu = _dot(x, wgu_s[:, cs]) + bgu_ref[0, :, cs]
            glu = jnp.minimum(gu[:, :half], SWIGLU_LIMIT)
            lin = jnp.clip(gu[:, half:], -SWIGLU_LIMIT, SWIGLU_LIMIT)
            act_s[:, c * half:(c + 1) * half] = (glu * _sigmoid(SWIGLU_ALPHA * glu) * (lin + 1.0)).astype(BF16)
        y = _dot(act_s[...], wdn_s[...]) + bdn_ref[0]
        y_ref[...] = _pack_bf16_pair(y)

    @pl.when(jnp.logical_not(valid))
    def _():
        y_ref[...] = jnp.zeros_like(y_ref)


def _moe_kernel_into(be_ref, nu_ref, x_ref, wgu_ref, bgu_ref, wdn_ref, bdn_ref, perm_ref, yprev_ref, y_ref,
                     wgu_s, wdn_s, act_s):
    del yprev_ref
    _moe_kernel(be_ref, nu_ref, x_ref, wgu_ref, bgu_ref, wdn_ref, bdn_ref, perm_ref, y_ref, wgu_s, wdn_s, act_s)


def _moe_ffn(x_part, block_e, n_used, w_gu, b_gu, w_dn, b_dn, layer, blk0, n_pad, y_prev):
    D = w_gu.shape[2]
    Dp = D // 2
    _, E, _, F2 = w_gu.shape
    F = F2 // 2
    n_blocks = x_part.shape[0] // MOE_BLK
    half = GU_CHUNK // 2
    src = np.concatenate([2 * np.arange(half), 2 * np.arange(half) + 1])
    perm = jnp.asarray(np.eye(GU_CHUNK, dtype=np.float32)[:, src], BF16)
    bgu_p = jnp.swapaxes(b_gu.reshape(E, F2 // GU_CHUNK, half, 2), 2, 3).reshape(E, 1, F2)
    grid_spec = pltpu.PrefetchScalarGridSpec(
        num_scalar_prefetch=2,
        grid=(n_blocks,),
        in_specs=[
            pl.BlockSpec((MOE_BLK, Dp), lambda i, be, nu: (i, 0)),
            pl.BlockSpec((1, 1, D, F2), lambda i, be, nu: (layer, be[i], 0, 0)),
            pl.BlockSpec((1, 1, F2), lambda i, be, nu: (be[i], 0, 0)),
            pl.BlockSpec((1, 1, F, D), lambda i, be, nu: (layer, be[i], 0, 0)),
            pl.BlockSpec((1, 1, D), lambda i, be, nu: (be[i], 0, 0)),
            pl.BlockSpec((GU_CHUNK, GU_CHUNK), lambda i, be, nu: (0, 0)),
        ] + ([] if y_prev is None else [pl.BlockSpec(memory_space=pl.ANY)]),
        out_specs=pl.BlockSpec((MOE_BLK, Dp), lambda i, be, nu: (i + blk0, 0)),
        scratch_shapes=[pltpu.VMEM((D, F2), BF16), pltpu.VMEM((F, D), BF16), pltpu.VMEM((MOE_BLK, F), BF16)],
    )
    args = (block_e, n_used, x_part, w_gu, bgu_p, w_dn, b_dn.reshape(E, 1, D), perm)
    return pl.pallas_call(
        _moe_kernel if y_prev is None else _moe_kernel_into,
        grid_spec=grid_spec,
        out_shape=jax.ShapeDtypeStruct((n_pad, Dp), jnp.int32),
        input_output_aliases={} if y_prev is None else {len(args): 0},
        compiler_params=_cparams(("arbitrary",)),
        name="moe_ffn",
    )(*args, *(() if y_prev is None else (y_prev,)))


SC_CORES = 2
SC_SUBCORES = 16
SC_WORKERS = SC_CORES * SC_SUBCORES
SC_ROWS = 64
SC_NBUF = 2


def _sc_gather(table, idx):
    V, W = table.shape
    n = idx.shape[0]
    assert n % (SC_WORKERS * SC_ROWS * SC_NBUF) == 0
    steps = n // (SC_WORKERS * SC_ROWS)
    idx3 = idx.reshape(SC_WORKERS, steps, SC_ROWS)
    mesh = plsc.VectorSubcoreMesh(core_axis_name="c", subcore_axis_name="s")

    @functools.partial(
        pl.kernel, mesh=mesh,
        out_type=jax.ShapeDtypeStruct((n, W), table.dtype),
        scratch_types=[pltpu.VMEM((steps, SC_ROWS), jnp.int32),
                       pltpu.VMEM((SC_NBUF, SC_ROWS, W), table.dtype),
                       pltpu.SemaphoreType.DMA((SC_NBUF,)),
                       pltpu.SemaphoreType.DMA((SC_NBUF,))],
        name="sc_row_gather",
    )
    def gather_kernel(table_hbm, idx_hbm, out_hbm, idx_v, rows_v, sem_g, sem_o):
        wid = lax.axis_index("s") * SC_CORES + lax.axis_index("c")
        step0 = wid * steps
        pltpu.sync_copy(idx_hbm.at[wid], idx_v)

        def gather(j, b):
            return pltpu.make_async_copy(table_hbm.at[idx_v.at[j + b]], rows_v.at[b], sem_g.at[b])

        def put(j, b):
            row0 = pl.multiple_of((step0 + j + b) * SC_ROWS, SC_ROWS)
            return pltpu.make_async_copy(rows_v.at[b], out_hbm.at[pl.ds(row0, SC_ROWS)], sem_o.at[b])

        @pl.loop(0, steps, step=SC_NBUF)
        def _(j):
            for b in range(SC_NBUF):
                gather(j, b).start()
            for b in range(SC_NBUF):
                gather(j, b).wait()
                put(j, b).start()
            for b in range(SC_NBUF):
                put(j, b).wait()

    return gather_kernel(table, idx3)


SC_SROWS = 128
SC_SWORDS = 128


def _sc_scatter(vals, idx, n_out):
    n, W = vals.shape
    assert W == SC_SWORDS and n % (SC_WORKERS * SC_SROWS) == 0
    steps = n // (SC_WORKERS * SC_SROWS)
    idx3 = idx.reshape(SC_WORKERS, steps, SC_SROWS)
    vals4 = vals.reshape(SC_WORKERS, steps, SC_SROWS, W)
    mesh = plsc.VectorSubcoreMesh(core_axis_name="c", subcore_axis_name="s")

    @functools.partial(
        pl.kernel, mesh=mesh,
        out_type=jax.ShapeDtypeStruct((n_out, W), vals.dtype),
        scratch_types=[pltpu.VMEM((steps, SC_SROWS), jnp.int32),
                       pltpu.VMEM((SC_SROWS, W), vals.dtype)],
        name="sc_row_scatter",
    )
    def scatter_kernel(vals_hbm, idx_hbm, out_hbm, idx_v, rows_v):
        wid = lax.axis_index("s") * SC_CORES + lax.axis_index("c")
        pltpu.sync_copy(idx_hbm.at[wid], idx_v)

        @pl.loop(0, steps)
        def _(j):
            pltpu.sync_copy(vals_hbm.at[wid, j], rows_v)
            pltpu.sync_copy(rows_v, out_hbm.at[idx_v.at[j]])

    return scatter_kernel(vals4, idx3)


def _router_kernel(lg_ref, tri_ref, meta_ref, cnt_ref, base_s):
    i = pl.program_id(0)

    @pl.when(i == 0)
    def _():
        base_s[...] = jnp.zeros_like(base_s)

    lg = lg_ref[...]
    lane = lax.broadcasted_iota(jnp.int32, lg.shape, 1)
    lg = jnp.where(lane < N_EXPERTS, lg, -jnp.inf)
    vals, ids, hots = [], [], []
    for _ in range(TOP_K):
        m = jnp.max(lg, axis=-1, keepdims=True)
        idx = jnp.min(jnp.where(lg == m, lane, lg.shape[1]), axis=-1, keepdims=True)
        hot = lane == idx
        vals.append(m)
        ids.append(idx)
        hots.append(hot)
        lg = jnp.where(hot, -jnp.inf, lg)
    es = [jnp.exp(v - vals[0]) for v in vals]
    den = es[0] + es[1] + es[2] + es[3]
    onehot = hots[0].astype(F32) + hots[1].astype(F32) + hots[2].astype(F32) + hots[3].astype(F32)
    before = base_s[...] + _dot(tri_ref[...], onehot.astype(BF16))
    meta = jnp.zeros(lg.shape, F32)
    for k in range(TOP_K):
        rank = jnp.sum(jnp.where(hots[k], before, 0.0), axis=-1, keepdims=True)
        meta = jnp.where(lane == k, ids[k].astype(F32), meta)
        meta = jnp.where(lane == TOP_K + k, rank, meta)
        meta = jnp.where(lane == 2 * TOP_K + k, es[k] / den, meta)
    meta_ref[...] = meta
    base_s[...] = base_s[...] + jnp.sum(onehot, axis=0, keepdims=True)
    cnt_ref[...] = base_s[...]


def _router(logits):
    n_tok, W = logits.shape
    tri = jnp.asarray(np.tril(np.ones((TM, TM), np.float32), -1), BF16)
    return pl.pallas_call(
        _router_kernel,
        grid=(n_tok // TM,),
        in_specs=[pl.BlockSpec((TM, W), lambda i: (i, 0)), pl.BlockSpec((TM, TM), lambda i: (0, 0))],
        out_specs=[pl.BlockSpec((TM, W), lambda i: (i, 0)), pl.BlockSpec((1, W), lambda i: (0, 0))],
        out_shape=[jax.ShapeDtypeStruct((n_tok, W), F32), jax.ShapeDtypeStruct((1, W), F32)],
        scratch_shapes=[pltpu.VMEM((1, W), F32)],
        compiler_params=_cparams(("arbitrary",)),
        name="router",
    )(logits, tri)


def _route(logits, n_tok):
    meta, cnt = _router(logits)
    top_i = meta[:, 0:TOP_K].astype(jnp.int32)
    rank = meta[:, TOP_K:2 * TOP_K].astype(jnp.int32)
    gates = meta[:, 2 * TOP_K:3 * TOP_K]
    n_assign = n_tok * TOP_K
    counts = cnt[0, :N_EXPERTS].astype(jnp.int32)
    padded = (counts + MOE_BLK - 1) // MOE_BLK * MOE_BLK
    ends_p = jnp.cumsum(padded)
    g_start = ends_p - padded
    sel = top_i[:, :, None] == jnp.arange(N_EXPERTS, dtype=jnp.int32)[None, None, :]
    dest = jnp.sum(jnp.where(sel, g_start[None, None, :], 0), axis=-1) + rank
    n_blocks = -(-(n_assign + N_EXPERTS * (MOE_BLK - 1)) // MOE_BLK)
    n_pad = n_blocks * MOE_BLK
    blk_start = jnp.arange(n_blocks, dtype=jnp.int32) * MOE_BLK
    block_e = jnp.minimum(jnp.sum((ends_p[None, :] <= blk_start[:, None]).astype(jnp.int32), axis=1),
                          N_EXPERTS - 1).astype(jnp.int32)
    n_used = (ends_p[-1] // MOE_BLK).astype(jnp.int32).reshape(1)
    tok_of = jnp.broadcast_to(jnp.arange(n_tok, dtype=jnp.int32)[:, None, None], (n_tok, TOP_K, SC_SWORDS))
    slot_tok = _sc_scatter(tok_of.reshape(n_assign, SC_SWORDS), dest.reshape(-1), n_pad)[:, 0]
    tok_pad = jnp.clip(slot_tok, 0, n_tok - 1)
    return gates, dest, tok_pad, block_e, n_used


def _final_kernel(x1_ref, yk_ref, gate_ref, mod_ref, g_ref, b_ref, o_ref, *, dn_alpha):
    D = x1_ref.shape[2]
    m = mod_ref[0, 0]
    gates = gate_ref[0]
    f = _unpack_bf16_pair(yk_ref[0, 0]) * gates[:, 0:1]
    for k in range(1, TOP_K):
        f = f + _unpack_bf16_pair(yk_ref[k, 0]) * gates[:, k:k + 1]
    o_ref[0] = _ln(dn_alpha * x1_ref[0] + m[5:6] * f) * g_ref[...] + b_ref[...]


def _final(x1, yk, gates, mod6, ln_g, ln_b, t0, nct, dn_alpha):
    B, So, D = x1.shape
    kind = lambda t: (t + t0 >= nct).astype(jnp.int32)
    blk = pl.BlockSpec((1, TM, D), lambda b, t: (b, t, 0))
    vec = pl.BlockSpec((1, D), lambda b, t: (0, 0))
    return pl.pallas_call(
        functools.partial(_final_kernel, dn_alpha=dn_alpha),
        grid=(B, So // TM),
        in_specs=[blk,
                  pl.BlockSpec((TOP_K, 1, TM, D // 2), lambda b, t: (0, b, t, 0)),
                  pl.BlockSpec((1, TM, TOP_K), lambda b, t: (b, t, 0)),
                  pl.BlockSpec((1, 1, 6, D), lambda b, t: (b, kind(t), 0, 0)), vec, vec],
        out_specs=blk,
        out_shape=jax.ShapeDtypeStruct((B, So, D), F32),
        compiler_params=_cparams(("arbitrary", "arbitrary")),
        name="ffn_residual",
    )(x1, yk, gates, mod6, ln_g, ln_b)


def _rope_tables(n_ctx, T):
    t = jnp.arange(T)
    row = (t // GRID_W).astype(F32)
    col = (t % GRID_W).astype(F32)
    n_freq = MLA_ROPE // 4
    inv = ROPE_BASE ** (-jnp.arange(n_freq, dtype=F32) / n_freq)
    ang = jnp.concatenate([row[:, None] * inv, col[:, None] * inv], axis=-1)
    cos, sin = jnp.cos(ang), jnp.sin(ang)
    z = jnp.zeros((T, 128 - MLA_ROPE), F32)
    cos128 = jnp.concatenate([cos, cos, z], axis=-1)
    sin128 = jnp.concatenate([-sin, sin, z], axis=-1)
    cos_c = jnp.concatenate([jnp.ones((n_ctx, MLA_ROPE), F32), jnp.zeros((n_ctx, 128 - MLA_ROPE), F32)], -1)
    sin_c = jnp.zeros((n_ctx, 128), F32)
    return jnp.concatenate([cos_c, cos128], 0), jnp.concatenate([sin_c, sin128], 0)


def _rope_slabs(w):
    ev, od = w[:, 0::2], w[:, 1::2]
    z = jnp.zeros((w.shape[0], 128 - MLA_ROPE), w.dtype)
    return jnp.concatenate([ev, od, z, od, ev, z], axis=-1)


def _blockdiag2(m):
    z = jnp.zeros_like(m[0])
    return jnp.concatenate([jnp.concatenate([m[0], z], 1), jnp.concatenate([z, m[1]], 1)], 0)


def kernel(x, c, ctx, c_ctx, ada_w, ada_b, w_in, rw_mu, rw_w0, rw_w2, rw_a0, rw_a2, rw_g2, rw_kk, rw_ka, rw_rk, rw_gn_w, rw_gn_b, mla_q_norm, mla_kv_norm, mla_w_uq, mla_w_ukv, na_rpb, w_out, ln1_g, ln1_b, router_w, router_b, w_gu, b_gu, w_dn, b_dn, ln2_g, ln2_b):
    B, T, D = x.shape
    n_ctx = ctx.shape[1]
    depth = ada_w.shape[0]
    S = n_ctx + T
    assert n_ctx % TM == 0 and T % TM == 0 and T % GRID_W == 0 and T // GRID_W >= NA_WIN_R
    nct = n_ctx // TM
    BH = B * RW_HEADS
    dn_alpha = (2 * depth) ** 0.25
    F = w_dn.shape[2]

    R = (B + 1 + 7) // 8 * 8
    cond = jnp.zeros((R, D), F32).at[:B].set(c).at[B].set(c_ctx)
    mod = _ada_mod(cond, ada_w, ada_b)
    mod_l = mod[:, :B].reshape(depth, B, 1, 6, D)
    mod_c = jnp.broadcast_to(mod[:, B].reshape(depth, 1, 1, 6, D), (depth, B, 1, 6, D))
    mod6 = jnp.concatenate([mod_c, mod_l], axis=2)

    cos128, sin128 = _rope_tables(n_ctx, T)
    ones_blk = jnp.kron(jnp.eye(RW_HEADS, dtype=F32), jnp.ones((RW_HEAD_DIM, RW_HEAD_DIM), F32)).astype(BF16)

    na_bias = [_na_bias_table(na_rpb[l]) for l in range(depth)]

    xa = jnp.concatenate([ctx, x], axis=1)
    for l in range(depth):
        need_ctx = l < depth - 1
        t0 = 0 if need_ctx else nct

        wi = w_in[l]
        c_m = RW_COLS
        w_in_ext = jnp.concatenate(
            [wi[:, :c_m + MLA_Q_LORA + MLA_KV_LORA], _rope_slabs(wi[:, c_m + MLA_Q_LORA + MLA_KV_LORA:c_m + MLA_COLS]),
             wi[:, c_m + MLA_COLS:]], axis=-1).astype(BF16)
        wuq = mla_w_uq[l].reshape(MLA_Q_LORA, MLA_HEADS, MLA_NOPE + MLA_ROPE)
        wuq_ext = jnp.concatenate(
            [jnp.concatenate([wuq[:, h, :MLA_NOPE], _rope_slabs(wuq[:, h, MLA_NOPE:])], -1) for h in range(MLA_HEADS)],
            axis=-1).astype(BF16)
        rw_prm = dict(
            mu=rw_mu[l],
            w0=rw_w0[l].reshape(1, 2 * RW_DIM), w2=_blockdiag2(rw_w2[l]).astype(BF16),
            a0=rw_a0[l].reshape(1, 2 * RW_DIM), a2=_blockdiag2(rw_a2[l]).astype(BF16),
            g2=rw_g2[l].astype(BF16), kk=rw_kk[l].reshape(1, RW_DIM), rk=rw_rk[l].reshape(1, RW_DIM),
            ones=ones_blk)
        mla_prm = dict(q_norm=mla_q_norm[l].reshape(1, -1), kv_norm=mla_kv_norm[l].reshape(1, -1),
                       w_uq=wuq_ext, w_ukv=mla_w_ukv[l].astype(BF16))
        rt = jnp.zeros((D, 128), F32).at[:, :N_EXPERTS].set(router_w[l])
        rt_hi = rt.astype(BF16)
        out_prm = dict(gn_w=rw_gn_w[l].reshape(1, -1), gn_b=rw_gn_b[l].reshape(1, -1), ones=ones_blk,
                       w_out=w_out[l].astype(BF16), ln1_g=ln1_g[l].reshape(1, -1), ln1_b=ln1_b[l].reshape(1, -1),
                       router_hi=rt_hi, router_lo=(rt - rt_hi.astype(F32)).astype(BF16),
                       router_b=jnp.zeros((1, 128), F32).at[0, :N_EXPERTS].set(router_b[l]))

        p_rw, p_mla, p_na = _in_proj(xa, mod6[l], w_in_ext, nct)
        feat, g_gate, bonus = _rw_features(p_rw, rw_prm, nct)
        ft = jnp.swapaxes(feat.reshape(S, BH, 8 * RW_HEAD_DIM), 1, 2).reshape(S, 8, RW_HEAD_DIM, BH)
        ka_t = jnp.tile(rw_ka[l].reshape(RW_HEADS, RW_HEAD_DIM).T[:, None, :], (1, B, 1)).reshape(RW_HEAD_DIM, BH)
        q, k, v = _mla_prep(p_mla, cos128, sin128, mla_prm)
        mla_o = _mla_attn(q, k, v, t0, nct, n_ctx)
        na_o = _na_attn(p_na, na_bias[l], n_ctx, need_ctx)

        o_scan = _rw_scan(ft, ka_t, n_ctx, mla_o)
        o_rw = jnp.swapaxes(o_scan, 2, 3).reshape(2, S, B * RW_DIM)

        x1, h, logits = _out_proj(o_rw, bonus, g_gate, mla_o, na_o, xa, mod6[l], out_prm, t0, nct, dn_alpha)

        So = x1.shape[1]
        n_tok = B * So
        gates, dest, tok_pad, block_e, n_used = _route(logits.reshape(n_tok, 128), n_tok)
        n_pad = tok_pad.shape[0]
        n_blocks = n_pad // MOE_BLK
        parts = MOE_PARTS if n_blocks % MOE_PARTS == 0 else 1
        pb = n_blocks // parts
        h2 = h.reshape(n_tok, D // 2)
        y_sorted = None
        for c in range(parts):
            x_part = _sc_gather(h2, tok_pad[c * pb * MOE_BLK:(c + 1) * pb * MOE_BLK])
            y_sorted = _moe_ffn(x_part, block_e[c * pb:(c + 1) * pb], jnp.clip(n_used - c * pb, 0, pb),
                                w_gu, b_gu[l], w_dn, b_dn[l], l, c * pb, n_pad, y_sorted)
        yk = _sc_gather(y_sorted, dest.T.reshape(-1)).reshape(TOP_K, B, So, D // 2)

        xa = _final(x1, yk, gates.reshape(B, So, TOP_K), mod6[l], ln2_g[l].reshape(1, -1), ln2_b[l].reshape(1, -1),
                    t0, nct, dn_alpha)
    return xa
```

```python
import functools

import numpy as np
import jax
import jax.numpy as jnp
from jax import lax
from jax.experimental import pallas as pl
from jax.experimental.pallas import tpu as pltpu
from jax.experimental.pallas import tpu_sc as plsc

F32 = jnp.float32
BF16 = jnp.bfloat16

GRID_W = 64
RW_HEAD_DIM = 64
RW_HEADS = 4
RW_DIM = RW_HEADS * RW_HEAD_DIM
RW_LORA = 64
RW_G_LORA = 128
RW_GN_EPS = 64e-5
RW_COLS = 3 * RW_DIM + 4 * RW_LORA + RW_G_LORA
MLA_HEADS = 4
MLA_NOPE = 128
MLA_ROPE = 64
MLA_V = 128
MLA_Q_LORA = 256
MLA_KV_LORA = 128
MLA_COLS = MLA_Q_LORA + MLA_KV_LORA + MLA_ROPE
MLA_COLS_EXT = MLA_Q_LORA + MLA_KV_LORA + 256
MLA_DIM = MLA_HEADS * MLA_V
NA_HEADS = 4
NA_HEAD_DIM = 64
NA_DIM = NA_HEADS * NA_HEAD_DIM
NA_WIN_R = 8
NA_WIN_C = 16
NA_COLS = 3 * NA_DIM
ROPE_BASE = 10000.0
N_EXPERTS = 32
TOP_K = 4
SWIGLU_ALPHA = 1.702
SWIGLU_LIMIT = 7.0
NEG_INF = -1e30

TM = 256
SCAN_TB = 32
MOE_BLK = 512
MOE_PARTS = 4
VMEM_LIMIT = 56 * 1024 * 1024


def _cparams(sem):
    return pltpu.CompilerParams(dimension_semantics=sem, vmem_limit_bytes=VMEM_LIMIT)


def _ln(x, eps=1e-5):
    mu = jnp.mean(x, axis=-1, keepdims=True)
    d = x - mu
    var = jnp.mean(d * d, axis=-1, keepdims=True)
    return d * lax.rsqrt(var + eps)


def _dot(a, b):
    return jnp.dot(a, b, preferred_element_type=F32)


def _split(a):
    hi = a.astype(BF16)
    lo = (a - hi.astype(F32)).astype(BF16)
    return hi, lo


def _dot_hl(a, b_bf16):
    hi, lo = _split(a)
    return _dot(hi, b_bf16) + _dot(lo, b_bf16)


def _dot3(a, b_hi, b_lo):
    hi, lo = _split(a)
    return _dot(hi, b_hi) + _dot(lo, b_hi) + _dot(hi, b_lo)


def _sigmoid(x):
    return 1.0 / (1.0 + jnp.exp(-x))


def _pack_bf16_pair(x):
    w = x.shape[1] // 2
    u = pltpu.bitcast(x.astype(BF16).astype(F32), jnp.uint32)
    lo = lax.shift_right_logical(u[:, :w], jnp.uint32(16))
    hi = lax.bitwise_and(u[:, w:], jnp.uint32(0xFFFF0000))
    return pltpu.bitcast(lax.bitwise_or(lo, hi), jnp.int32)


def _unpack_bf16_pair(p):
    u = pltpu.bitcast(p, jnp.uint32)
    lo = pltpu.bitcast(lax.shift_left(u, jnp.uint32(16)), F32)
    hi = pltpu.bitcast(lax.bitwise_and(u, jnp.uint32(0xFFFF0000)), F32)
    return jnp.concatenate([lo, hi], axis=-1)


def _ada_kernel(cond_ref, w_ref, b_ref, o_ref):
    c = cond_ref[...]
    s = c * _sigmoid(c)
    w = w_ref[0]
    w_hi, w_lo = _split(w)
    o_ref[0] = _dot3(s, w_hi, w_lo) + b_ref[0]


def _ada_mod(cond, ada_w, ada_b):
    L, D, N = ada_w.shape
    R = cond.shape[0]
    tn = 512
    return pl.pallas_call(
        _ada_kernel,
        grid=(L, N // tn),
        in_specs=[
            pl.BlockSpec((R, D), lambda l, j: (0, 0)),
            pl.BlockSpec((1, D, tn), lambda l, j: (l, 0, j)),
            pl.BlockSpec((1, 1, tn), lambda l, j: (l, 0, j)),
        ],
        out_specs=pl.BlockSpec((1, R, tn), lambda l, j: (l, 0, j)),
        out_shape=jax.ShapeDtypeStruct((L, R, N), F32),
        compiler_params=_cparams(("arbitrary", "arbitrary")),
        name="ada_mod",
    )(cond, ada_w, ada_b.reshape(L, 1, N))


def _win_kernel(x_ref, mod_ref, w_ref, prw_ref, pmla_ref, pna_ref):
    x = x_ref[0]
    m = mod_ref[0, 0]
    xm = _ln(x) * (1.0 + m[1:2]) + m[0:1]
    p = _dot(xm.astype(BF16), w_ref[...])
    prw_ref[0] = p[:, :RW_COLS]
    pmla_ref[0] = p[:, RW_COLS:RW_COLS + MLA_COLS_EXT]
    pna_ref[0] = p[:, RW_COLS + MLA_COLS_EXT:].astype(BF16)


def _in_proj(xa, mod6, w_in_ext, nct):
    B, S, D = xa.shape
    NC = w_in_ext.shape[1]
    kind = lambda t: (t >= nct).astype(jnp.int32)
    return pl.pallas_call(
        _win_kernel,
        grid=(B, S // TM),
        in_specs=[
            pl.BlockSpec((1, TM, D), lambda b, t: (b, t, 0)),
            pl.BlockSpec((1, 1, 6, D), lambda b, t: (b, kind(t), 0, 0)),
            pl.BlockSpec((D, NC), lambda b, t: (0, 0)),
        ],
        out_specs=[
            pl.BlockSpec((1, TM, RW_COLS), lambda b, t: (b, t, 0)),
            pl.BlockSpec((1, TM, MLA_COLS_EXT), lambda b, t: (b, t, 0)),
            pl.BlockSpec((1, TM, NA_COLS), lambda b, t: (b, t, 0)),
        ],
        out_shape=[
            jax.ShapeDtypeStruct((B, S, RW_COLS), F32),
            jax.ShapeDtypeStruct((B, S, MLA_COLS_EXT), F32),
            jax.ShapeDtypeStruct((B, S, NA_COLS), BF16),
        ],
        compiler_params=_cparams(("arbitrary", "arbitrary")),
        name="in_proj",
    )(xa, mod6, w_in_ext)


def _group_sum(x, ones_ref):
    return _dot_hl(x, ones_ref[...])


def _rwfeat_kernel(p_ref, pp_ref, pn_ref, mu_ref, w0_ref, w2_ref, a0_ref, a2_ref, g2_ref,
                   kk_ref, rk_ref, ones_ref, f_ref, g_ref, bonus_ref, *, nct, nt):
    t = pl.program_id(1)
    p = p_ref[0]
    first = jnp.logical_or(t == 0, t == nct)
    last = jnp.logical_or(t == nct - 1, t == nt - 1)
    prev_row = jnp.where(first, 0.0, pp_ref[0, 7:8, :])
    next_row = jnp.where(last, 0.0, pn_ref[0, 0:1, :])
    rows = lax.broadcasted_iota(jnp.int32, p.shape, 0)
    prev = jnp.where(rows == 0, prev_row, pltpu.roll(p, 1, axis=0))
    nxt = jnp.where(rows == TM - 1, next_row, pltpu.roll(p, TM - 1, axis=0))
    mu = mu_ref[...]
    xs = p + mu[0:1] * (prev - p) + mu[1:2] * (nxt - p)

    D3 = 3 * RW_DIM
    r = xs[:, 0:RW_DIM]
    k = xs[:, RW_DIM:2 * RW_DIM]
    v = xs[:, 2 * RW_DIM:D3]
    w_lo = xs[:, D3:D3 + 2 * RW_LORA]
    a_lo = xs[:, D3 + 2 * RW_LORA:D3 + 4 * RW_LORA]
    g_pre = xs[:, D3 + 4 * RW_LORA:]

    lw = _dot(jnp.tanh(w_lo).astype(BF16), w2_ref[...]) + w0_ref[...]
    logw = jnp.minimum(lw, 0.0) - jnp.log(1.0 + jnp.exp(-jnp.abs(lw))) - 0.5
    decay = jnp.exp(-jnp.exp(logw))
    a = _sigmoid(_dot(a_lo.astype(BF16), a2_ref[...]) + a0_ref[...])

    kkr = k * kk_ref[...]
    kk = kkr * lax.rsqrt(_group_sum(kkr * kkr, ones_ref) + 1e-12)
    g_ref[0] = _dot(_sigmoid(g_pre).astype(BF16), g2_ref[...])
    bonus_ref[0] = _group_sum(r * k * rk_ref[...], ones_ref) * v

    comps = (r, k, v, kk, decay[:, :RW_DIM], a[:, :RW_DIM], decay[:, RW_DIM:], a[:, RW_DIM:])
    N = RW_HEAD_DIM
    for h in range(RW_HEADS):
        for ci, comp in enumerate(comps):
            col = (h * len(comps) + ci) * N
            f_ref[:, col:col + N] = comp[:, h * N:(h + 1) * N]


def _rw_features(p_rw, prm, nct):
    B, S, C = p_rw.shape
    nt = S // TM
    hb = TM // 8
    last_hb = S // 8 - 1
    full = lambda shape: pl.BlockSpec(shape, lambda b, t: (0,) * len(shape))
    return pl.pallas_call(
        functools.partial(_rwfeat_kernel, nct=nct, nt=nt),
        grid=(B, nt),
        in_specs=[
            pl.BlockSpec((1, TM, C), lambda b, t: (b, t, 0)),
            pl.BlockSpec((1, 8, C), lambda b, t: (b, jnp.maximum(t * hb - 1, 0), 0)),
            pl.BlockSpec((1, 8, C), lambda b, t: (b, jnp.minimum((t + 1) * hb, last_hb), 0)),
            full((2, C)),
            full((1, 2 * RW_DIM)), full((2 * RW_LORA, 2 * RW_DIM)),
            full((1, 2 * RW_DIM)), full((2 * RW_LORA, 2 * RW_DIM)),
            full((RW_G_LORA, RW_DIM)),
            full((1, RW_DIM)), full((1, RW_DIM)),
            full((RW_DIM, RW_DIM)),
        ],
        out_specs=[
            pl.BlockSpec((TM, 8 * RW_DIM), lambda b, t: (t, b)),
            pl.BlockSpec((1, TM, RW_DIM), lambda b, t: (b, t, 0)),
            pl.BlockSpec((1, TM, RW_DIM), lambda b, t: (b, t, 0)),
        ],
        out_shape=[
            jax.ShapeDtypeStruct((S, B * 8 * RW_DIM), F32),
            jax.ShapeDtypeStruct((B, S, RW_DIM), F32),
            jax.ShapeDtypeStruct((B, S, RW_DIM), F32),
        ],
        compiler_params=_cparams(("arbitrary", "arbitrary")),
        name="rw_features",
    )(p_rw, p_rw, p_rw, prm["mu"], prm["w0"], prm["w2"], prm["a0"], prm["a2"], prm["g2"],
      prm["kk"], prm["rk"], prm["ones"])


def _scan_kernel(fs_ref, fd_ref, ka_ref, after_ref, o_ref, s_ref, tmp_ref, *, tb):
    del after_ref
    d = pl.program_id(0)
    g = pl.program_id(1)
    N = RW_HEAD_DIM

    @pl.when(g == 0)
    def _():
        s_ref[...] = jnp.zeros_like(s_ref)

    ka = ka_ref[...]

    def step(i, carry):
        tt = jnp.where(d == 0, i, tb - 1 - i)
        r = fs_ref[tt, 0]
        k = fs_ref[tt, 1]
        v = fs_ref[tt, 2]
        kk = fs_ref[tt, 3]
        w = fd_ref[tt, 0]
        a = fd_ref[tt, 1]
        b = a * kk
        kd = k * (1.0 + (a - 1.0) * ka)
        wr = w * r
        br = jnp.sum(b * r, axis=0, keepdims=True)
        kr = jnp.sum(kd * r, axis=0, keepdims=True)
        tmp_ref[0] = wr
        tmp_ref[1] = b
        tmp_ref[2] = kd
        sa = [jnp.zeros_like(v), jnp.zeros_like(v)]
        op = [jnp.zeros_like(v), jnp.zeros_like(v)]
        for j in range(N):
            sk = s_ref[j]
            sa[j % 2] = sa[j % 2] + sk * fs_ref[tt, 3, pl.ds(j, 1), :]
            op[j % 2] = op[j % 2] + sk * tmp_ref[0, pl.ds(j, 1), :]
        sa = sa[0] + sa[1]
        op = op[0] + op[1]
        for j in range(N):
            s_ref[j] = (s_ref[j] * fd_ref[tt, 0, pl.ds(j, 1), :]
                        - sa * tmp_ref[1, pl.ds(j, 1), :]
                        + v * tmp_ref[2, pl.ds(j, 1), :])
        o_ref[0, tt] = op - sa * br + v * kr
        return carry

    lax.fori_loop(0, tb, step, 0)


def _rw_scan(ft, ka_t, n_ctx, after):
    S, _, N, BH = ft.shape
    tb = SCAN_TB
    nb = S // tb
    ncb = n_ctx // tb

    def tblk(d, g):
        bwd = jnp.where(g < ncb, ncb - 1 - g, nb - 1 - g + ncb)
        return jnp.where(d == 0, g, bwd)

    return pl.pallas_call(
        functools.partial(_scan_kernel, tb=tb),
        grid=(2, nb),
        in_specs=[
            pl.BlockSpec((tb, 4, N, BH), lambda d, g: (tblk(d, g), 0, 0, 0)),
            pl.BlockSpec((tb, 2, N, BH), lambda d, g: (tblk(d, g), 2 + d, 0, 0)),
            pl.BlockSpec((N, BH), lambda d, g: (0, 0)),
            pl.BlockSpec(memory_space=pl.ANY),
        ],
        out_specs=pl.BlockSpec((1, tb, N, BH), lambda d, g: (d, tblk(d, g), 0, 0)),
        out_shape=jax.ShapeDtypeStruct((2, S, N, BH), F32),
        scratch_shapes=[pltpu.VMEM((N, N, BH), F32), pltpu.VMEM((3, N, BH), F32)],
        compiler_params=_cparams(("arbitrary", "arbitrary")),
        name="rw_scan",
    )(ft, ft, ka_t, after)


def _mlaprep_kernel(p_ref, cos_ref, sin_ref, qn_ref, kvn_ref, wuq_ref, wukv_ref, q_ref, k_ref, v_ref):
    p = p_ref[0]
    cos = cos_ref[...]
    sin = sin_ref[...]
    scale = (MLA_NOPE + MLA_ROPE) ** -0.5

    def rms(x, g):
        return x * lax.rsqrt(jnp.mean(x * x, axis=-1, keepdims=True) + 1e-6) * g

    q = _dot(rms(p[:, :MLA_Q_LORA], qn_ref[...]).astype(BF16), wuq_ref[...])
    kv = _dot(rms(p[:, MLA_Q_LORA:MLA_Q_LORA + MLA_KV_LORA], kvn_ref[...]).astype(BF16), wukv_ref[...])
    c0 = MLA_Q_LORA + MLA_KV_LORA
    kr = (p[:, c0:c0 + 128] * cos + p[:, c0 + 128:c0 + 256] * sin).astype(BF16)
    for h in range(MLA_HEADS):
        qb = h * 384
        q_ref[0, :, h * 256:h * 256 + 128] = (q[:, qb:qb + 128] * scale).astype(BF16)
        q_ref[0, :, h * 256 + 128:h * 256 + 256] = (
            (q[:, qb + 128:qb + 256] * cos + q[:, qb + 256:qb + 384] * sin) * scale).astype(BF16)
        k_ref[0, :, h * 256:h * 256 + 128] = kv[:, h * 256:h * 256 + 128].astype(BF16)
        k_ref[0, :, h * 256 + 128:h * 256 + 256] = kr
        v_ref[0, :, h * 128:(h + 1) * 128] = kv[:, h * 256 + 128:h * 256 + 256].astype(BF16)


def _mla_prep(p_mla, cos128, sin128, prm):
    B, S, C = p_mla.shape
    full = lambda shape: pl.BlockSpec(shape, lambda b, t: (0,) * len(shape))
    H = MLA_HEADS
    return pl.pallas_call(
        _mlaprep_kernel,
        grid=(B, S // TM),
        in_specs=[
            pl.BlockSpec((1, TM, C), lambda b, t: (b, t, 0)),
            pl.BlockSpec((TM, 128), lambda b, t: (t, 0)),
            pl.BlockSpec((TM, 128), lambda b, t: (t, 0)),
            full((1, MLA_Q_LORA)), full((1, MLA_KV_LORA)),
            full((MLA_Q_LORA, H * 384)), full((MLA_KV_LORA, H * 256)),
        ],
        out_specs=[
            pl.BlockSpec((1, TM, H * 256), lambda b, t: (b, t, 0)),
            pl.BlockSpec((1, TM, H * 256), lambda b, t: (b, t, 0)),
            pl.BlockSpec((1, TM, H * 128), lambda b, t: (b, t, 0)),
        ],
        out_shape=[
            jax.ShapeDtypeStruct((B, S, H * 256), BF16),
            jax.ShapeDtypeStruct((B, S, H * 256), BF16),
            jax.ShapeDtypeStruct((B, S, H * 128), BF16),
        ],
        compiler_params=_cparams(("arbitrary", "arbitrary")),
        name="mla_prep",
    )(p_mla, cos128, sin128, prm["q_norm"], prm["kv_norm"], prm["w_uq"], prm["w_ukv"])


def _mla_attn_kernel(q_ref, k_ref, v_ref, o_ref, *, t0, nct, n_ctx):
    t = pl.program_id(1) + t0

    def attend(n_keys):
        for h in range(MLA_HEADS):
            q = q_ref[0, :, h * 256:(h + 1) * 256]
            k = k_ref[0, 0:n_keys, h * 256:(h + 1) * 256]
            s = lax.dot_general(q, k, (((1,), (1,)), ((), ())), preferred_element_type=F32)
            m = jnp.max(s, axis=-1, keepdims=True)
            e = jnp.exp(s - m)
            l = jnp.sum(e, axis=-1, keepdims=True)
            o = _dot(e.astype(BF16), v_ref[0, 0:n_keys, h * 128:(h + 1) * 128])
            o_ref[0, :, h * 128:(h + 1) * 128] = (o / l).astype(BF16)

    S = k_ref.shape[1]
    if t0 < nct:
        @pl.when(t < nct)
        def _():
            attend(n_ctx)

        @pl.when(t >= nct)
        def _():
            attend(S)
    else:
        attend(S)


def _mla_attn(q, k, v, t0, nct, n_ctx):
    B, S, _ = q.shape
    nq = S // TM - t0
    H = MLA_HEADS
    return pl.pallas_call(
        functools.partial(_mla_attn_kernel, t0=t0, nct=nct, n_ctx=n_ctx),
        grid=(B, nq),
        in_specs=[
            pl.BlockSpec((1, TM, H * 256), lambda b, t: (b, t + t0, 0)),
            pl.BlockSpec((1, S, H * 256), lambda b, t: (b, 0, 0)),
            pl.BlockSpec((1, S, H * 128), lambda b, t: (b, 0, 0)),
        ],
        out_specs=pl.BlockSpec((1, TM, H * 128), lambda b, t: (b, t, 0)),
        out_shape=jax.ShapeDtypeStruct((B, nq * TM, H * 128), BF16),
        compiler_params=_cparams(("arbitrary", "arbitrary")),
        name="mla_attn",
    )(q, k, v)


NA_QR = 4
NA_KR = NA_WIN_R + NA_QR
NA_QB = NA_QR * GRID_W


def _na_kernel(p_ref, bias_ref, o_ref, *, n_ctx, rows, with_ctx):
    s_id = pl.program_id(1)
    W = GRID_W
    n_loc = NA_KR * W
    scale = NA_HEAD_DIM ** -0.5
    nq_ctx = n_ctx // NA_QB
    nblk = rows // NA_QR

    def heads_out(q, parts):
        outs = []
        for h in range(NA_HEADS):
            hs = slice(h * NA_HEAD_DIM, (h + 1) * NA_HEAD_DIM)
            qh = q[:, hs] * scale
            ss = []
            for kx, vx, bias in parts:
                s = lax.dot_general(qh, kx[:, hs], (((1,), (1,)), ((), ())), preferred_element_type=F32)
                if bias is not None:
                    s = s + bias(h)
                ss.append(s)
            m = ss[0].max(axis=-1, keepdims=True)
            for s in ss[1:]:
                m = jnp.maximum(m, s.max(axis=-1, keepdims=True))
            acc = 0.0
            l = 0.0
            for s, (kx, vx, bias) in zip(ss, parts):
                e = jnp.exp(s - m)
                l = l + jnp.sum(e, axis=-1, keepdims=True)
                acc = acc + _dot(e.astype(BF16), vx[:, hs])
            outs.append(acc / l)
        return jnp.concatenate(outs, axis=-1).astype(BF16)

    k_c = p_ref[0, 0:n_ctx, NA_DIM:2 * NA_DIM]
    v_c = p_ref[0, 0:n_ctx, 2 * NA_DIM:3 * NA_DIM]

    def lat_block(j):
        i0 = j * NA_QR
        k_start = jnp.clip(i0 - NA_WIN_R // 2, 0, rows - NA_KR)
        pat = jnp.where(j == 0, 0, jnp.where(j == nblk - 1, 2, 1))
        q0 = pl.multiple_of(n_ctx + i0 * W, NA_QB)
        k0 = pl.multiple_of(n_ctx + k_start * W, W)
        q = p_ref[0, pl.ds(q0, NA_QB), 0:NA_DIM]
        k_l = p_ref[0, pl.ds(k0, n_loc), NA_DIM:2 * NA_DIM]
        v_l = p_ref[0, pl.ds(k0, n_loc), 2 * NA_DIM:3 * NA_DIM]
        o_ref[0] = heads_out(q, [(k_l, v_l, lambda h: bias_ref[pat, h]), (k_c, v_c, None)])

    if with_ctx:
        @pl.when(s_id < nq_ctx)
        def _():
            q0 = pl.multiple_of(s_id * NA_QB, NA_QB)
            q = p_ref[0, pl.ds(q0, NA_QB), 0:NA_DIM]
            o_ref[0] = heads_out(q, [(k_c, v_c, None)])

        @pl.when(s_id >= nq_ctx)
        def _():
            lat_block(s_id - nq_ctx)
    else:
        lat_block(s_id)


def _na_attn(p_na, bias_tab, n_ctx, with_ctx):
    B, S, C = p_na.shape
    T = S - n_ctx
    rows = T // GRID_W
    assert rows % NA_QR == 0 and rows >= NA_KR and n_ctx % NA_QB == 0
    nsteps = rows // NA_QR + (n_ctx // NA_QB if with_ctx else 0)
    return pl.pallas_call(
        functools.partial(_na_kernel, n_ctx=n_ctx, rows=rows, with_ctx=with_ctx),
        grid=(B, nsteps),
        in_specs=[
            pl.BlockSpec((1, S, C), lambda b, s: (b, 0, 0)),
            pl.BlockSpec(bias_tab.shape, lambda b, s: (0, 0, 0, 0)),
        ],
        out_specs=pl.BlockSpec((1, NA_QB, NA_DIM), lambda b, s: (b, s, 0)),
        out_shape=jax.ShapeDtypeStruct((B, nsteps * NA_QB, NA_DIM), BF16),
        compiler_params=_cparams(("arbitrary", "arbitrary")),
        name="na_attn",
    )(p_na, bias_tab)


def _na_bias_table(rpb):
    col = np.arange(GRID_W)
    c_start = np.clip(col - NA_WIN_C // 2, 0, GRID_W - NA_WIN_C)
    in_win = (col[None, :] >= c_start[:, None]) & (col[None, :] < c_start[:, None] + NA_WIN_C)
    dc_idx = np.clip(col[None, :] - col[:, None] + NA_WIN_C - 1, 0, 2 * NA_WIN_C - 2)
    qa = np.arange(NA_QR)[:, None]
    kc = np.arange(NA_KR)[None, :]
    row_ok, dr_idx = [], []
    for pat in range(3):
        off = (NA_WIN_R // 2) * pat
        first = (0 * qa, qa, 0 * qa + NA_WIN_R // 2)[pat]
        row_ok.append((kc >= first) & (kc < first + NA_WIN_R))
        dr_idx.append(np.clip(kc - qa - off + NA_WIN_R - 1, 0, 2 * NA_WIN_R - 2))
    sel_c = np.eye(2 * NA_WIN_C - 1, dtype=np.float32)[dc_idx]
    sel_r = np.eye(2 * NA_WIN_R - 1, dtype=np.float32)[np.stack(dr_idx)]
    hp = lax.Precision.HIGHEST
    t = jnp.einsum('hrs,qks->hrqk', rpb, sel_c, precision=hp)
    bias = jnp.einsum('pacr,hrqk->phaqck', sel_r, t, precision=hp)
    ok = jnp.asarray(np.stack(row_ok))[:, None, :, None, :, None] & jnp.asarray(in_win)[None, None, None, :, None, :]
    bias = jnp.where(ok, bias, NEG_INF)
    return bias.reshape(3, NA_HEADS, NA_QB, NA_KR * GRID_W).astype(F32)


def _outproj_kernel(orw_ref, bonus_ref, g_ref, mla_ref, na_ref, x_ref, mod_ref, gnw_ref, gnb_ref, ones_ref,
                    wout_ref, ln1g_ref, ln1b_ref, rwh_ref, rwl_ref, rb_ref,
                    x1_ref, h_ref, lg_ref, *, dn_alpha):
    o = orw_ref[0] + orw_ref[1]
    inv_n = 1.0 / RW_HEAD_DIM
    mu = _group_sum(o, ones_ref) * inv_n
    dlt = o - mu
    var = _group_sum(dlt * dlt, ones_ref) * inv_n
    on = dlt * lax.rsqrt(var + RW_GN_EPS) * gnw_ref[...] + gnb_ref[...]
    rw_y = ((on + bonus_ref[0]) * g_ref[0]).astype(BF16)
    y = (_dot(rw_y, wout_ref[0:RW_DIM, :])
         + _dot(mla_ref[0], wout_ref[RW_DIM:RW_DIM + MLA_DIM, :])
         + _dot(na_ref[0], wout_ref[RW_DIM + MLA_DIM:, :]))
    m = mod_ref[0, 0]
    x1 = _ln(dn_alpha * x_ref[0] + m[2:3] * y) * ln1g_ref[...] + ln1b_ref[...]
    x1_ref[0] = x1
    h = _ln(x1) * (1.0 + m[4:5]) + m[3:4]
    h_ref[0] = _pack_bf16_pair(h)
    lg_ref[0] = _dot3(h, rwh_ref[...], rwl_ref[...]) + rb_ref[...]


def _out_proj(o_rw, bonus, g, mla_o, na_o, xa, mod6, prm, t0, nct, dn_alpha):
    B, S, D = xa.shape
    nt = S // TM - t0
    So = nt * TM
    kind = lambda t: (t + t0 >= nct).astype(jnp.int32)
    full = lambda shape: pl.BlockSpec(shape, lambda b, t: (0,) * len(shape))
    off = lambda C: pl.BlockSpec((1, TM, C), lambda b, t: (b, t + t0, 0))
    own = lambda C: pl.BlockSpec((1, TM, C), lambda b, t: (b, t, 0))
    return pl.pallas_call(
        functools.partial(_outproj_kernel, dn_alpha=dn_alpha),
        grid=(B, nt),
        in_specs=[
            pl.BlockSpec((2, TM, RW_DIM), lambda b, t: (0, t + t0, b)),
            off(RW_DIM), off(RW_DIM), own(MLA_DIM), own(NA_DIM), off(D),
            pl.BlockSpec((1, 1, 6, D), lambda b, t: (b, kind(t), 0, 0)),
            full((1, RW_DIM)), full((1, RW_DIM)), full((RW_DIM, RW_DIM)),
            full((D, D)), full((1, D)), full((1, D)),
            full((D, 128)), full((D, 128)), full((1, 128)),
        ],
        out_specs=[own(D), own(D // 2), own(128)],
        out_shape=[
            jax.ShapeDtypeStruct((B, So, D), F32),
            jax.ShapeDtypeStruct((B, So, D // 2), jnp.int32),
            jax.ShapeDtypeStruct((B, So, 128), F32),
        ],
        compiler_params=_cparams(("arbitrary", "arbitrary")),
        name="out_proj",
    )(o_rw, bonus, g, mla_o, na_o, xa, mod6, prm["gn_w"], prm["gn_b"], prm["ones"],
      prm["w_out"], prm["ln1_g"], prm["ln1_b"], prm["router_hi"], prm["router_lo"], prm["router_b"])


GU_CHUNK = 256


def _moe_kernel(be_ref, nu_ref, x_ref, wgu_ref, bgu_ref, wdn_ref, bdn_ref, perm_ref, y_ref,
                wgu_s, wdn_s, act_s):
    i = pl.program_id(0)
    F = wdn_ref.shape[2]
    n_chunks = 2 * F // GU_CHUNK
    half = GU_CHUNK // 2
    valid = i < nu_ref[0]
    e = be_ref[i]
    new_expert = jnp.logical_or(i == 0, e != be_ref[jnp.maximum(i - 1, 0)])

    @pl.when(jnp.logical_and(valid, new_expert))
    def _():
        for c in range(n_chunks):
            cs = slice(c * GU_CHUNK, (c + 1) * GU_CHUNK)
            wgu_s[:, cs] = _dot(wgu_ref[0, 0, :, cs].astype(BF16), perm_ref[...]).astype(BF16)
        wdn_s[...] = wdn_ref[0, 0].astype(BF16)

    @pl.when(valid)
    def _():
        gu = _dot(_unpack_bf16_pair(x_ref[...]).astype(BF16), wgu_s[...]) + bgu_ref[0]
        for c in range(n_chunks):
            glu = jnp.minimum(gu[:, c * GU_CHUNK:c * GU_CHUNK + half], SWIGLU_LIMIT)
            lin = jnp.clip(gu[:, c * GU_CHUNK + half:(c + 1) * GU_CHUNK], -SWIGLU_LIMIT, SWIGLU_LIMIT)
            act_s[:, c * half:(c + 1) * half] = (glu * _sigmoid(SWIGLU_ALPHA * glu) * (lin + 1.0)).astype(BF16)
        y = _dot(act_s[...], wdn_s[...]) + bdn_ref[0]
        y_ref[...] = _pack_bf16_pair(y)

    @pl.when(jnp.logical_not(valid))
    def _():
        y_ref[...] = jnp.zeros_like(y_ref)


def _moe_kernel_into(be_ref, nu_ref, x_ref, wgu_ref, bgu_ref, wdn_ref, bdn_ref, perm_ref, yprev_ref, y_ref,
                     wgu_s, wdn_s, act_s):
    del yprev_ref
    _moe_kernel(be_ref, nu_ref, x_ref, wgu_ref, bgu_ref, wdn_ref, bdn_ref, perm_ref, y_ref, wgu_s, wdn_s, act_s)


def _moe_ffn(x_part, block_e, n_used, w_gu, b_gu, w_dn, b_dn, layer, blk0, n_pad, y_prev):
    D = w_gu.shape[2]
    Dp = D // 2
    _, E, _, F2 = w_gu.shape
    F = F2 // 2
    n_blocks = x_part.shape[0] // MOE_BLK
    half = GU_CHUNK // 2
    src = np.concatenate([2 * np.arange(half), 2 * np.arange(half) + 1])
    perm = jnp.asarray(np.eye(GU_CHUNK, dtype=np.float32)[:, src], BF16)
    bgu_p = jnp.swapaxes(b_gu.reshape(E, F2 // GU_CHUNK, half, 2), 2, 3).reshape(E, 1, F2)
    grid_spec = pltpu.PrefetchScalarGridSpec(
        num_scalar_prefetch=2,
        grid=(n_blocks,),
        in_specs=[
            pl.BlockSpec((MOE_BLK, Dp), lambda i, be, nu: (i, 0)),
            pl.BlockSpec((1, 1, D, F2), lambda i, be, nu: (layer, be[i], 0, 0)),
            pl.BlockSpec((1, 1, F2), lambda i, be, nu: (be[i], 0, 0)),
            pl.BlockSpec((1, 1, F, D), lambda i, be, nu: (layer, be[i], 0, 0)),
            pl.BlockSpec((1, 1, D), lambda i, be, nu: (be[i], 0, 0)),
            pl.BlockSpec((GU_CHUNK, GU_CHUNK), lambda i, be, nu: (0, 0)),
        ] + ([] if y_prev is None else [pl.BlockSpec(memory_space=pl.ANY)]),
        out_specs=pl.BlockSpec((MOE_BLK, Dp), lambda i, be, nu: (i + blk0, 0)),
        scratch_shapes=[pltpu.VMEM((D, F2), BF16), pltpu.VMEM((F, D), BF16), pltpu.VMEM((MOE_BLK, F), BF16)],
    )
    args = (block_e, n_used, x_part, w_gu, bgu_p, w_dn, b_dn.reshape(E, 1, D), perm)
    return pl.pallas_call(
        _moe_kernel if y_prev is None else _moe_kernel_into,
        grid_spec=grid_spec,
        out_shape=jax.ShapeDtypeStruct((n_pad, Dp), jnp.int32),
        input_output_aliases={} if y_prev is None else {len(args): 0},
        compiler_params=_cparams(("arbitrary",)),
        name="moe_ffn",
    )(*args, *(() if y_prev is None else (y_prev,)))


SC_CORES = 2
SC_SUBCORES = 16
SC_WORKERS = SC_CORES * SC_SUBCORES
SC_ROWS = 64
SC_NBUF = 2


def _sc_gather(table, idx):
    V, W = table.shape
    n = idx.shape[0]
    assert n % (SC_WORKERS * SC_ROWS * SC_NBUF) == 0
    steps = n // (SC_WORKERS * SC_ROWS)
    idx3 = idx.reshape(SC_WORKERS, steps, SC_ROWS)
    mesh = plsc.VectorSubcoreMesh(core_axis_name="c", subcore_axis_name="s")

    @functools.partial(
        pl.kernel, mesh=mesh,
        out_type=jax.ShapeDtypeStruct((n, W), table.dtype),
        scratch_types=[pltpu.VMEM((steps, SC_ROWS), jnp.int32),
                       pltpu.VMEM((SC_NBUF, SC_ROWS, W), table.dtype),
                       pltpu.SemaphoreType.DMA((SC_NBUF,)),
                       pltpu.SemaphoreType.DMA((SC_NBUF,))],
        name="sc_row_gather",
    )
    def gather_kernel(table_hbm, idx_hbm, out_hbm, idx_v, rows_v, sem_g, sem_o):
        wid = lax.axis_index("s") * SC_CORES + lax.axis_index("c")
        step0 = wid * steps
        pltpu.sync_copy(idx_hbm.at[wid], idx_v)

        def gather(j, b):
            return pltpu.make_async_copy(table_hbm.at[idx_v.at[j + b]], rows_v.at[b], sem_g.at[b])

        def put(j, b):
            row0 = pl.multiple_of((step0 + j + b) * SC_ROWS, SC_ROWS)
            return pltpu.make_async_copy(rows_v.at[b], out_hbm.at[pl.ds(row0, SC_ROWS)], sem_o.at[b])

        @pl.loop(0, steps, step=SC_NBUF)
        def _(j):
            for b in range(SC_NBUF):
                gather(j, b).start()
            for b in range(SC_NBUF):
                gather(j, b).wait()
                put(j, b).start()
            for b in range(SC_NBUF):
                put(j, b).wait()

    return gather_kernel(table, idx3)


SC_SROWS = 128
SC_SWORDS = 128


def _sc_scatter(vals, idx, n_out):
    n, W = vals.shape
    assert W == SC_SWORDS and n % (SC_WORKERS * SC_SROWS) == 0
    steps = n // (SC_WORKERS * SC_SROWS)
    idx3 = idx.reshape(SC_WORKERS, steps, SC_SROWS)
    vals4 = vals.reshape(SC_WORKERS, steps, SC_SROWS, W)
    mesh = plsc.VectorSubcoreMesh(core_axis_name="c", subcore_axis_name="s")

    @functools.partial(
        pl.kernel, mesh=mesh,
        out_type=jax.ShapeDtypeStruct((n_out, W), vals.dtype),
        scratch_types=[pltpu.VMEM((steps, SC_SROWS), jnp.int32),
                       pltpu.VMEM((SC_SROWS, W), vals.dtype)],
        name="sc_row_scatter",
    )
    def scatter_kernel(vals_hbm, idx_hbm, out_hbm, idx_v, rows_v):
        wid = lax.axis_index("s") * SC_CORES + lax.axis_index("c")
        pltpu.sync_copy(idx_hbm.at[wid], idx_v)

        @pl.loop(0, steps)
        def _(j):
            pltpu.sync_copy(vals_hbm.at[wid, j], rows_v)
            pltpu.sync_copy(rows_v, out_hbm.at[idx_v.at[j]])

    return scatter_kernel(vals4, idx3)


def _router_kernel(lg_ref, tri_ref, meta_ref, cnt_ref, base_s):
    i = pl.program_id(0)

    @pl.when(i == 0)
    def _():
        base_s[...] = jnp.zeros_like(base_s)

    lg = lg_ref[...]
    lane = lax.broadcasted_iota(jnp.int32, lg.shape, 1)
    lg = jnp.where(lane < N_EXPERTS, lg, -jnp.inf)
    vals, ids, hots = [], [], []
    for _ in range(TOP_K):
        m = jnp.max(lg, axis=-1, keepdims=True)
        idx = jnp.min(jnp.where(lg == m, lane, lg.shape[1]), axis=-1, keepdims=True)
        hot = lane == idx
        vals.append(m)
        ids.append(idx)
        hots.append(hot)
        lg = jnp.where(hot, -jnp.inf, lg)
    es = [jnp.exp(v - vals[0]) for v in vals]
    den = es[0] + es[1] + es[2] + es[3]
    onehot = hots[0].astype(F32) + hots[1].astype(F32) + hots[2].astype(F32) + hots[3].astype(F32)
    before = base_s[...] + _dot(tri_ref[...], onehot.astype(BF16))
    meta = jnp.zeros(lg.shape, F32)
    for k in range(TOP_K):
        rank = jnp.sum(jnp.where(hots[k], before, 0.0), axis=-1, keepdims=True)
        meta = jnp.where(lane == k, ids[k].astype(F32), meta)
        meta = jnp.where(lane == TOP_K + k, rank, meta)
        meta = jnp.where(lane == 2 * TOP_K + k, es[k] / den, meta)
    meta_ref[...] = meta
    base_s[...] = base_s[...] + jnp.sum(onehot, axis=0, keepdims=True)
    cnt_ref[...] = base_s[...]


def _router(logits):
    n_tok, W = logits.shape
    tri = jnp.asarray(np.tril(np.ones((TM, TM), np.float32), -1), BF16)
    return pl.pallas_call(
        _router_kernel,
        grid=(n_tok // TM,),
        in_specs=[pl.BlockSpec((TM, W), lambda i: (i, 0)), pl.BlockSpec((TM, TM), lambda i: (0, 0))],
        out_specs=[pl.BlockSpec((TM, W), lambda i: (i, 0)), pl.BlockSpec((1, W), lambda i: (0, 0))],
        out_shape=[jax.ShapeDtypeStruct((n_tok, W), F32), jax.ShapeDtypeStruct((1, W), F32)],
        scratch_shapes=[pltpu.VMEM((1, W), F32)],
        compiler_params=_cparams(("arbitrary",)),
        name="router",
    )(logits, tri)


def _route(logits, n_tok):
    meta, cnt = _router(logits)
    top_i = meta[:, 0:TOP_K].astype(jnp.int32)
    rank = meta[:, TOP_K:2 * TOP_K].astype(jnp.int32)
    gates = meta[:, 2 * TOP_K:3 * TOP_K]
    n_assign = n_tok * TOP_K
    counts = cnt[0, :N_EXPERTS].astype(jnp.int32)
    padded = (counts + MOE_BLK - 1) // MOE_BLK * MOE_BLK
    ends_p = jnp.cumsum(padded)
    g_start = ends_p - padded
    sel = top_i[:, :, None] == jnp.arange(N_EXPERTS, dtype=jnp.int32)[None, None, :]
    dest = jnp.sum(jnp.where(sel, g_start[None, None, :], 0), axis=-1) + rank
    n_blocks = -(-(n_assign + N_EXPERTS * (MOE_BLK - 1)) // MOE_BLK)
    n_pad = n_blocks * MOE_BLK
    blk_start = jnp.arange(n_blocks, dtype=jnp.int32) * MOE_BLK
    block_e = jnp.minimum(jnp.sum((ends_p[None, :] <= blk_start[:, None]).astype(jnp.int32), axis=1),
                          N_EXPERTS - 1).astype(jnp.int32)
    n_used = (ends_p[-1] // MOE_BLK).astype(jnp.int32).reshape(1)
    tok_of = jnp.broadcast_to(jnp.arange(n_tok, dtype=jnp.int32)[:, None, None], (n_tok, TOP_K, SC_SWORDS))
    slot_tok = _sc_scatter(tok_of.reshape(n_assign, SC_SWORDS), dest.reshape(-1), n_pad)[:, 0]
    tok_pad = jnp.clip(slot_tok, 0, n_tok - 1)
    return gates, dest, tok_pad, block_e, n_used


def _final_kernel(x1_ref, yk_ref, gate_ref, mod_ref, g_ref, b_ref, o_ref, *, dn_alpha):
    D = x1_ref.shape[2]
    m = mod_ref[0, 0]
    gates = gate_ref[0]
    f = _unpack_bf16_pair(yk_ref[0, 0]) * gates[:, 0:1]
    for k in range(1, TOP_K):
        f = f + _unpack_bf16_pair(yk_ref[k, 0]) * gates[:, k:k + 1]
    o_ref[0] = _ln(dn_alpha * x1_ref[0] + m[5:6] * f) * g_ref[...] + b_ref[...]


def _final(x1, yk, gates, mod6, ln_g, ln_b, t0, nct, dn_alpha):
    B, So, D = x1.shape
    kind = lambda t: (t + t0 >= nct).astype(jnp.int32)
    blk = pl.BlockSpec((1, TM, D), lambda b, t: (b, t, 0))
    vec = pl.BlockSpec((1, D), lambda b, t: (0, 0))
    return pl.pallas_call(
        functools.partial(_final_kernel, dn_alpha=dn_alpha),
        grid=(B, So // TM),
        in_specs=[blk,
                  pl.BlockSpec((TOP_K, 1, TM, D // 2), lambda b, t: (0, b, t, 0)),
                  pl.BlockSpec((1, TM, TOP_K), lambda b, t: (b, t, 0)),
                  pl.BlockSpec((1, 1, 6, D), lambda b, t: (b, kind(t), 0, 0)), vec, vec],
        out_specs=blk,
        out_shape=jax.ShapeDtypeStruct((B, So, D), F32),
        compiler_params=_cparams(("arbitrary", "arbitrary")),
        name="ffn_residual",
    )(x1, yk, gates, mod6, ln_g, ln_b)


def _rope_tables(n_ctx, T):
    t = jnp.arange(T)
    row = (t // GRID_W).astype(F32)
    col = (t % GRID_W).astype(F32)
    n_freq = MLA_ROPE // 4
    inv = ROPE_BASE ** (-jnp.arange(n_freq, dtype=F32) / n_freq)
    ang = jnp.concatenate([row[:, None] * inv, col[:, None] * inv], axis=-1)
    cos, sin = jnp.cos(ang), jnp.sin(ang)
    z = jnp.zeros((T, 128 - MLA_ROPE), F32)
    cos128 = jnp.concatenate([cos, cos, z], axis=-1)
    sin128 = jnp.concatenate([-sin, sin, z], axis=-1)
    cos_c = jnp.concatenate([jnp.ones((n_ctx, MLA_ROPE), F32), jnp.zeros((n_ctx, 128 - MLA_ROPE), F32)], -1)
    sin_c = jnp.zeros((n_ctx, 128), F32)
    return jnp.concatenate([cos_c, cos128], 0), jnp.concatenate([sin_c, sin128], 0)


def _rope_slabs(w):
    ev, od = w[:, 0::2], w[:, 1::2]
    z = jnp.zeros((w.shape[0], 128 - MLA_ROPE), w.dtype)
    return jnp.concatenate([ev, od, z, od, ev, z], axis=-1)


def _blockdiag2(m):
    z = jnp.zeros_like(m[0])
    return jnp.concatenate([jnp.concatenate([m[0], z], 1), jnp.concatenate([z, m[1]], 1)], 0)


def kernel(x, c, ctx, c_ctx, ada_w, ada_b, w_in, rw_mu, rw_w0, rw_w2, rw_a0, rw_a2, rw_g2, rw_kk, rw_ka, rw_rk, rw_gn_w, rw_gn_b, mla_q_norm, mla_kv_norm, mla_w_uq, mla_w_ukv, na_rpb, w_out, ln1_g, ln1_b, router_w, router_b, w_gu, b_gu, w_dn, b_dn, ln2_g, ln2_b):
    B, T, D = x.shape
    n_ctx = ctx.shape[1]
    depth = ada_w.shape[0]
    S = n_ctx + T
    assert n_ctx % TM == 0 and T % TM == 0 and T % GRID_W == 0 and T // GRID_W >= NA_WIN_R
    nct = n_ctx // TM
    BH = B * RW_HEADS
    dn_alpha = (2 * depth) ** 0.25
    F = w_dn.shape[2]

    R = (B + 1 + 7) // 8 * 8
    cond = jnp.zeros((R, D), F32).at[:B].set(c).at[B].set(c_ctx)
    mod = _ada_mod(cond, ada_w, ada_b)
    mod_l = mod[:, :B].reshape(depth, B, 1, 6, D)
    mod_c = jnp.broadcast_to(mod[:, B].reshape(depth, 1, 1, 6, D), (depth, B, 1, 6, D))
    mod6 = jnp.concatenate([mod_c, mod_l], axis=2)

    cos128, sin128 = _rope_tables(n_ctx, T)
    ones_blk = jnp.kron(jnp.eye(RW_HEADS, dtype=F32), jnp.ones((RW_HEAD_DIM, RW_HEAD_DIM), F32)).astype(BF16)

    na_bias = [_na_bias_table(na_rpb[l]) for l in range(depth)]

    xa = jnp.concatenate([ctx, x], axis=1)
    for l in range(depth):
        need_ctx = l < depth - 1
        t0 = 0 if need_ctx else nct

        wi = w_in[l]
        c_m = RW_COLS
        w_in_ext = jnp.concatenate(
            [wi[:, :c_m + MLA_Q_LORA + MLA_KV_LORA], _rope_slabs(wi[:, c_m + MLA_Q_LORA + MLA_KV_LORA:c_m + MLA_COLS]),
             wi[:, c_m + MLA_COLS:]], axis=-1).astype(BF16)
        wuq = mla_w_uq[l].reshape(MLA_Q_LORA, MLA_HEADS, MLA_NOPE + MLA_ROPE)
        wuq_ext = jnp.concatenate(
            [jnp.concatenate([wuq[:, h, :MLA_NOPE], _rope_slabs(wuq[:, h, MLA_NOPE:])], -1) for h in range(MLA_HEADS)],
            axis=-1).astype(BF16)
        rw_prm = dict(
            mu=rw_mu[l],
            w0=rw_w0[l].reshape(1, 2 * RW_DIM), w2=_blockdiag2(rw_w2[l]).astype(BF16),
            a0=rw_a0[l].reshape(1, 2 * RW_DIM), a2=_blockdiag2(rw_a2[l]).astype(BF16),
            g2=rw_g2[l].astype(BF16), kk=rw_kk[l].reshape(1, RW_DIM), rk=rw_rk[l].reshape(1, RW_DIM),
            ones=ones_blk)
        mla_prm = dict(q_norm=mla_q_norm[l].reshape(1, -1), kv_norm=mla_kv_norm[l].reshape(1, -1),
                       w_uq=wuq_ext, w_ukv=mla_w_ukv[l].astype(BF16))
        rt = jnp.zeros((D, 128), F32).at[:, :N_EXPERTS].set(router_w[l])
        rt_hi = rt.astype(BF16)
        out_prm = dict(gn_w=rw_gn_w[l].reshape(1, -1), gn_b=rw_gn_b[l].reshape(1, -1), ones=ones_blk,
                       w_out=w_out[l].astype(BF16), ln1_g=ln1_g[l].reshape(1, -1), ln1_b=ln1_b[l].reshape(1, -1),
                       router_hi=rt_hi, router_lo=(rt - rt_hi.astype(F32)).astype(BF16),
                       router_b=jnp.zeros((1, 128), F32).at[0, :N_EXPERTS].set(router_b[l]))

        p_rw, p_mla, p_na = _in_proj(xa, mod6[l], w_in_ext, nct)
        feat, g_gate, bonus = _rw_features(p_rw, rw_prm, nct)
        ft = jnp.swapaxes(feat.reshape(S, BH, 8 * RW_HEAD_DIM), 1, 2).reshape(S, 8, RW_HEAD_DIM, BH)
        ka_t = jnp.tile(rw_ka[l].reshape(RW_HEADS, RW_HEAD_DIM).T[:, None, :], (1, B, 1)).reshape(RW_HEAD_DIM, BH)
        q, k, v = _mla_prep(p_mla, cos128, sin128, mla_prm)
        mla_o = _mla_attn(q, k, v, t0, nct, n_ctx)
        na_o = _na_attn(p_na, na_bias[l], n_ctx, need_ctx)

        o_scan = _rw_scan(ft, ka_t, n_ctx, mla_o)
        o_rw = jnp.swapaxes(o_scan, 2, 3).reshape(2, S, B * RW_DIM)

        x1, h, logits = _out_proj(o_rw, bonus, g_gate, mla_o, na_o, xa, mod6[l], out_prm, t0, nct, dn_alpha)

        So = x1.shape[1]
        n_tok = B * So
        gates, dest, tok_pad, block_e, n_used = _route(logits.reshape(n_tok, 128), n_tok)
        n_pad = tok_pad.shape[0]
        n_blocks = n_pad // MOE_BLK
        parts = MOE_PARTS if n_blocks % MOE_PARTS == 0 else 1
        pb = n_blocks // parts
        h2 = h.reshape(n_tok, D // 2)
        y_sorted = None
        for c in range(parts):
            x_part = _sc_gather(h2, tok_pad[c * pb * MOE_BLK:(c + 1) * pb * MOE_BLK])
            y_sorted = _moe_ffn(x_part, block_e[c * pb:(c + 1) * pb], jnp.clip(n_used - c * pb, 0, pb),
                                w_gu, b_gu[l], w_dn, b_dn[l], l, c * pb, n_pad, y_sorted)
        yk = _sc_gather(y_sorted, dest.T.reshape(-1)).reshape(TOP_K, B, So, D // 2)

        xa = _final(x1, yk, gates.reshape(B, So, TOP_K), mod6[l], ln2_g[l].reshape(1, -1), ln2_b[l].reshape(1, -1),
                    t0, nct, dn_alpha)
    return xa
```

```python
import functools

import numpy as np
import jax
import jax.numpy as jnp
from jax import lax
from jax.experimental import pallas as pl
from jax.experimental.pallas import tpu as pltpu
from jax.experimental.pallas import tpu_sc as plsc

F32 = jnp.float32
BF16 = jnp.bfloat16

GRID_W = 64
RW_HEAD_DIM = 64
RW_HEADS = 4
RW_DIM = RW_HEADS * RW_HEAD_DIM
RW_LORA = 64
RW_G_LORA = 128
RW_GN_EPS = 64e-5
RW_COLS = 3 * RW_DIM + 4 * RW_LORA + RW_G_LORA
MLA_HEADS = 4
MLA_NOPE = 128
MLA_ROPE = 64
MLA_V = 128
MLA_Q_LORA = 256
MLA_KV_LORA = 128
MLA_COLS = MLA_Q_LORA + MLA_KV_LORA + MLA_ROPE
MLA_COLS_EXT = MLA_Q_LORA + MLA_KV_LORA + 256
MLA_DIM = MLA_HEADS * MLA_V
NA_HEADS = 4
NA_HEAD_DIM = 64
NA_DIM = NA_HEADS * NA_HEAD_DIM
NA_WIN_R = 8
NA_WIN_C = 16
NA_COLS = 3 * NA_DIM
ROPE_BASE = 10000.0
N_EXPERTS = 32
TOP_K = 4
SWIGLU_ALPHA = 1.702
SWIGLU_LIMIT = 7.0
NEG_INF = -1e30

TM = 256
SCAN_TB = 16
MOE_BLK = 512
MOE_PARTS = 4
VMEM_LIMIT = 56 * 1024 * 1024


def _cparams(sem):
    return pltpu.CompilerParams(dimension_semantics=sem, vmem_limit_bytes=VMEM_LIMIT)


def _ln(x, eps=1e-5):
    mu = jnp.mean(x, axis=-1, keepdims=True)
    d = x - mu
    var = jnp.mean(d * d, axis=-1, keepdims=True)
    return d * lax.rsqrt(var + eps)


def _dot(a, b):
    return jnp.dot(a, b, preferred_element_type=F32)


def _split(a):
    hi = a.astype(BF16)
    lo = (a - hi.astype(F32)).astype(BF16)
    return hi, lo


def _dot_hl(a, b_bf16):
    hi, lo = _split(a)
    return _dot(hi, b_bf16) + _dot(lo, b_bf16)


def _dot3(a, b_hi, b_lo):
    hi, lo = _split(a)
    return _dot(hi, b_hi) + _dot(lo, b_hi) + _dot(hi, b_lo)


def _sigmoid(x):
    return 1.0 / (1.0 + jnp.exp(-x))


def _pack_bf16_pair(x):
    w = x.shape[1] // 2
    u = pltpu.bitcast(x.astype(BF16).astype(F32), jnp.uint32)
    lo = lax.shift_right_logical(u[:, :w], jnp.uint32(16))
    hi = lax.bitwise_and(u[:, w:], jnp.uint32(0xFFFF0000))
    return pltpu.bitcast(lax.bitwise_or(lo, hi), jnp.int32)


def _unpack_bf16_pair(p):
    u = pltpu.bitcast(p, jnp.uint32)
    lo = pltpu.bitcast(lax.shift_left(u, jnp.uint32(16)), F32)
    hi = pltpu.bitcast(lax.bitwise_and(u, jnp.uint32(0xFFFF0000)), F32)
    return jnp.concatenate([lo, hi], axis=-1)


def _ada_kernel(cond_ref, w_ref, b_ref, o_ref):
    c = cond_ref[...]
    s = c * _sigmoid(c)
    w = w_ref[0]
    w_hi, w_lo = _split(w)
    o_ref[0] = _dot3(s, w_hi, w_lo) + b_ref[0]


def _ada_mod(cond, ada_w, ada_b):
    L, D, N = ada_w.shape
    R = cond.shape[0]
    tn = 512
    return pl.pallas_call(
        _ada_kernel,
        grid=(L, N // tn),
        in_specs=[
            pl.BlockSpec((R, D), lambda l, j: (0, 0)),
            pl.BlockSpec((1, D, tn), lambda l, j: (l, 0, j)),
            pl.BlockSpec((1, 1, tn), lambda l, j: (l, 0, j)),
        ],
        out_specs=pl.BlockSpec((1, R, tn), lambda l, j: (l, 0, j)),
        out_shape=jax.ShapeDtypeStruct((L, R, N), F32),
        compiler_params=_cparams(("arbitrary", "arbitrary")),
        name="ada_mod",
    )(cond, ada_w, ada_b.reshape(L, 1, N))


def _win_kernel(x_ref, mod_ref, w_ref, prw_ref, pmla_ref, pna_ref):
    x = x_ref[0]
    m = mod_ref[0, 0]
    xm = _ln(x) * (1.0 + m[1:2]) + m[0:1]
    p = _dot(xm.astype(BF16), w_ref[...])
    prw_ref[0] = p[:, :RW_COLS]
    pmla_ref[0] = p[:, RW_COLS:RW_COLS + MLA_COLS_EXT]
    pna_ref[0] = p[:, RW_COLS + MLA_COLS_EXT:].astype(BF16)


def _in_proj(xa, mod6, w_in_ext, nct):
    B, S, D = xa.shape
    NC = w_in_ext.shape[1]
    kind = lambda t: (t >= nct).astype(jnp.int32)
    return pl.pallas_call(
        _win_kernel,
        grid=(B, S // TM),
        in_specs=[
            pl.BlockSpec((1, TM, D), lambda b, t: (b, t, 0)),
            pl.BlockSpec((1, 1, 6, D), lambda b, t: (b, kind(t), 0, 0)),
            pl.BlockSpec((D, NC), lambda b, t: (0, 0)),
        ],
        out_specs=[
            pl.BlockSpec((1, TM, RW_COLS), lambda b, t: (b, t, 0)),
            pl.BlockSpec((1, TM, MLA_COLS_EXT), lambda b, t: (b, t, 0)),
            pl.BlockSpec((1, TM, NA_COLS), lambda b, t: (b, t, 0)),
        ],
        out_shape=[
            jax.ShapeDtypeStruct((B, S, RW_COLS), F32),
            jax.ShapeDtypeStruct((B, S, MLA_COLS_EXT), F32),
            jax.ShapeDtypeStruct((B, S, NA_COLS), BF16),
        ],
        compiler_params=_cparams(("arbitrary", "arbitrary")),
        name="in_proj",
    )(xa, mod6, w_in_ext)


def _group_sum(x, ones_ref):
    return _dot_hl(x, ones_ref[...])


def _rwfeat_kernel(p_ref, pp_ref, pn_ref, mu_ref, w0_ref, w2_ref, a0_ref, a2_ref, g2_ref,
                   kk_ref, rk_ref, ones_ref, f_ref, g_ref, bonus_ref, *, nct, nt):
    t = pl.program_id(1)
    p = p_ref[0]
    first = jnp.logical_or(t == 0, t == nct)
    last = jnp.logical_or(t == nct - 1, t == nt - 1)
    prev_row = jnp.where(first, 0.0, pp_ref[0, 7:8, :])
    next_row = jnp.where(last, 0.0, pn_ref[0, 0:1, :])
    rows = lax.broadcasted_iota(jnp.int32, p.shape, 0)
    prev = jnp.where(rows == 0, prev_row, pltpu.roll(p, 1, axis=0))
    nxt = jnp.where(rows == TM - 1, next_row, pltpu.roll(p, TM - 1, axis=0))
    mu = mu_ref[...]
    xs = p + mu[0:1] * (prev - p) + mu[1:2] * (nxt - p)

    D3 = 3 * RW_DIM
    r = xs[:, 0:RW_DIM]
    k = xs[:, RW_DIM:2 * RW_DIM]
    v = xs[:, 2 * RW_DIM:D3]
    w_lo = xs[:, D3:D3 + 2 * RW_LORA]
    a_lo = xs[:, D3 + 2 * RW_LORA:D3 + 4 * RW_LORA]
    g_pre = xs[:, D3 + 4 * RW_LORA:]

    lw = _dot(jnp.tanh(w_lo).astype(BF16), w2_ref[...]) + w0_ref[...]
    logw = jnp.minimum(lw, 0.0) - jnp.log(1.0 + jnp.exp(-jnp.abs(lw))) - 0.5
    decay = jnp.exp(-jnp.exp(logw))
    a = _sigmoid(_dot(a_lo.astype(BF16), a2_ref[...]) + a0_ref[...])

    kkr = k * kk_ref[...]
    kk = kkr * lax.rsqrt(_group_sum(kkr * kkr, ones_ref) + 1e-12)
    g_ref[0] = _dot(_sigmoid(g_pre).astype(BF16), g2_ref[...])
    bonus_ref[0] = _group_sum(r * k * rk_ref[...], ones_ref) * v

    comps = (r, k, v, kk, decay[:, :RW_DIM], a[:, :RW_DIM], decay[:, RW_DIM:], a[:, RW_DIM:])
    N = RW_HEAD_DIM
    for h in range(RW_HEADS):
        for ci, comp in enumerate(comps):
            col = (h * len(comps) + ci) * N
            f_ref[:, col:col + N] = comp[:, h * N:(h + 1) * N]


def _rw_features(p_rw, prm, nct):
    B, S, C = p_rw.shape
    nt = S // TM
    hb = TM // 8
    last_hb = S // 8 - 1
    full = lambda shape: pl.BlockSpec(shape, lambda b, t: (0,) * len(shape))
    return pl.pallas_call(
        functools.partial(_rwfeat_kernel, nct=nct, nt=nt),
        grid=(B, nt),
        in_specs=[
            pl.BlockSpec((1, TM, C), lambda b, t: (b, t, 0)),
            pl.BlockSpec((1, 8, C), lambda b, t: (b, jnp.maximum(t * hb - 1, 0), 0)),
            pl.BlockSpec((1, 8, C), lambda b, t: (b, jnp.minimum((t + 1) * hb, last_hb), 0)),
            full((2, C)),
            full((1, 2 * RW_DIM)), full((2 * RW_LORA, 2 * RW_DIM)),
            full((1, 2 * RW_DIM)), full((2 * RW_LORA, 2 * RW_DIM)),
            full((RW_G_LORA, RW_DIM)),
            full((1, RW_DIM)), full((1, RW_DIM)),
            full((RW_DIM, RW_DIM)),
        ],
        out_specs=[
            pl.BlockSpec((TM, 8 * RW_DIM), lambda b, t: (t, b)),
            pl.BlockSpec((1, TM, RW_DIM), lambda b, t: (b, t, 0)),
            pl.BlockSpec((1, TM, RW_DIM), lambda b, t: (b, t, 0)),
        ],
        out_shape=[
            jax.ShapeDtypeStruct((S, B * 8 * RW_DIM), F32),
            jax.ShapeDtypeStruct((B, S, RW_DIM), F32),
            jax.ShapeDtypeStruct((B, S, RW_DIM), F32),
        ],
        compiler_params=_cparams(("arbitrary", "arbitrary")),
        name="rw_features",
    )(p_rw, p_rw, p_rw, prm["mu"], prm["w0"], prm["w2"], prm["a0"], prm["a2"], prm["g2"],
      prm["kk"], prm["rk"], prm["ones"])


def _scan_kernel(fs_ref, fd_ref, ka_ref, after_ref, o_ref, s_ref, tmp_ref, *, tb):
    del after_ref
    d = pl.program_id(0)
    g = pl.program_id(1)
    N = RW_HEAD_DIM

    @pl.when(g == 0)
    def _():
        s_ref[...] = jnp.zeros_like(s_ref)

    ka = ka_ref[...]

    def step(i, carry):
        tt = jnp.where(d == 0, i, tb - 1 - i)
        r = fs_ref[tt, 0]
        k = fs_ref[tt, 1]
        v = fs_ref[tt, 2]
        kk = fs_ref[tt, 3]
        w = fd_ref[tt, 0]
        a = fd_ref[tt, 1]
        b = a * kk
        kd = k * (1.0 + (a - 1.0) * ka)
        wr = w * r
        br = jnp.sum(b * r, axis=0, keepdims=True)
        kr = jnp.sum(kd * r, axis=0, keepdims=True)
        tmp_ref[0] = wr
        tmp_ref[1] = b
        tmp_ref[2] = kd
        sa = [jnp.zeros_like(v), jnp.zeros_like(v)]
        op = [jnp.zeros_like(v), jnp.zeros_like(v)]
        for j in range(N):
            sk = s_ref[j]
            sa[j % 2] = sa[j % 2] + sk * fs_ref[tt, 3, pl.ds(j, 1), :]
            op[j % 2] = op[j % 2] + sk * tmp_ref[0, pl.ds(j, 1), :]
        sa = sa[0] + sa[1]
        op = op[0] + op[1]
        for j in range(N):
            s_ref[j] = (s_ref[j] * fd_ref[tt, 0, pl.ds(j, 1), :]
                        - sa * tmp_ref[1, pl.ds(j, 1), :]
                        + v * tmp_ref[2, pl.ds(j, 1), :])
        o_ref[0, tt] = op - sa * br + v * kr
        return carry

    lax.fori_loop(0, tb, step, 0)


def _rw_scan(ft, ka_t, n_ctx, after):
    S, _, N, BH = ft.shape
    tb = SCAN_TB
    nb = S // tb
    ncb = n_ctx // tb

    def tblk(d, g):
        bwd = jnp.where(g < ncb, ncb - 1 - g, nb - 1 - g + ncb)
        return jnp.where(d == 0, g, bwd)

    return pl.pallas_call(
        functools.partial(_scan_kernel, tb=tb),
        grid=(2, nb),
        in_specs=[
            pl.BlockSpec((tb, 4, N, BH), lambda d, g: (tblk(d, g), 0, 0, 0)),
            pl.BlockSpec((tb, 2, N, BH), lambda d, g: (tblk(d, g), 2 + d, 0, 0)),
            pl.BlockSpec((N, BH), lambda d, g: (0, 0)),
            pl.BlockSpec(memory_space=pl.ANY),
        ],
        out_specs=pl.BlockSpec((1, tb, N, BH), lambda d, g: (d, tblk(d, g), 0, 0)),
        out_shape=jax.ShapeDtypeStruct((2, S, N, BH), F32),
        scratch_shapes=[pltpu.VMEM((N, N, BH), F32), pltpu.VMEM((3, N, BH), F32)],
        compiler_params=_cparams(("arbitrary", "arbitrary")),
        name="rw_scan",
    )(ft, ft, ka_t, after)


def _mlaprep_kernel(p_ref, cos_ref, sin_ref, qn_ref, kvn_ref, wuq_ref, wukv_ref, q_ref, k_ref, v_ref):
    p = p_ref[0]
    cos = cos_ref[...]
    sin = sin_ref[...]
    scale = (MLA_NOPE + MLA_ROPE) ** -0.5

    def rms(x, g):
        return x * lax.rsqrt(jnp.mean(x * x, axis=-1, keepdims=True) + 1e-6) * g

    q = _dot(rms(p[:, :MLA_Q_LORA], qn_ref[...]).astype(BF16), wuq_ref[...])
    kv = _dot(rms(p[:, MLA_Q_LORA:MLA_Q_LORA + MLA_KV_LORA], kvn_ref[...]).astype(BF16), wukv_ref[...])
    c0 = MLA_Q_LORA + MLA_KV_LORA
    kr = (p[:, c0:c0 + 128] * cos + p[:, c0 + 128:c0 + 256] * sin).astype(BF16)
    for h in range(MLA_HEADS):
        qb = h * 384
        q_ref[0, :, h * 256:h * 256 + 128] = (q[:, qb:qb + 128] * scale).astype(BF16)
        q_ref[0, :, h * 256 + 128:h * 256 + 256] = (
            (q[:, qb + 128:qb + 256] * cos + q[:, qb + 256:qb + 384] * sin) * scale).astype(BF16)
        k_ref[0, :, h * 256:h * 256 + 128] = kv[:, h * 256:h * 256 + 128].astype(BF16)
        k_ref[0, :, h * 256 + 128:h * 256 + 256] = kr
        v_ref[0, :, h * 128:(h + 1) * 128] = kv[:, h * 256 + 128:h * 256 + 256].astype(BF16)


def _mla_prep(p_mla, cos128, sin128, prm):
    B, S, C = p_mla.shape
    full = lambda shape: pl.BlockSpec(shape, lambda b, t: (0,) * len(shape))
    H = MLA_HEADS
    return pl.pallas_call(
        _mlaprep_kernel,
        grid=(B, S // TM),
        in_specs=[
            pl.BlockSpec((1, TM, C), lambda b, t: (b, t, 0)),
            pl.BlockSpec((TM, 128), lambda b, t: (t, 0)),
            pl.BlockSpec((TM, 128), lambda b, t: (t, 0)),
            full((1, MLA_Q_LORA)), full((1, MLA_KV_LORA)),
            full((MLA_Q_LORA, H * 384)), full((MLA_KV_LORA, H * 256)),
        ],
        out_specs=[
            pl.BlockSpec((1, TM, H * 256), lambda b, t: (b, t, 0)),
            pl.BlockSpec((1, TM, H * 256), lambda b, t: (b, t, 0)),
            pl.BlockSpec((1, TM, H * 128), lambda b, t: (b, t, 0)),
        ],
        out_shape=[
            jax.ShapeDtypeStruct((B, S, H * 256), BF16),
            jax.ShapeDtypeStruct((B, S, H * 256), BF16),
            jax.ShapeDtypeStruct((B, S, H * 128), BF16),
        ],
        compiler_params=_cparams(("arbitrary", "arbitrary")),
        name="mla_prep",
    )(p_mla, cos128, sin128, prm["q_norm"], prm["kv_norm"], prm["w_uq"], prm["w_ukv"])


def _mla_attn_kernel(q_ref, k_ref, v_ref, o_ref, *, t0, nct, n_ctx):
    t = pl.program_id(1) + t0

    def attend(n_keys):
        for h in range(MLA_HEADS):
            q = q_ref[0, :, h * 256:(h + 1) * 256]
            k = k_ref[0, 0:n_keys, h * 256:(h + 1) * 256]
            s = lax.dot_general(q, k, (((1,), (1,)), ((), ())), preferred_element_type=F32)
            m = jnp.max(s, axis=-1, keepdims=True)
            e = jnp.exp(s - m)
            l = jnp.sum(e, axis=-1, keepdims=True)
            o = _dot(e.astype(BF16), v_ref[0, 0:n_keys, h * 128:(h + 1) * 128])
            o_ref[0, :, h * 128:(h + 1) * 128] = (o / l).astype(BF16)

    S = k_ref.shape[1]
    if t0 < nct:
        @pl.when(t < nct)
        def _():
            attend(n_ctx)

        @pl.when(t >= nct)
        def _():
            attend(S)
    else:
        attend(S)


def _mla_attn(q, k, v, t0, nct, n_ctx):
    B, S, _ = q.shape
    nq = S // TM - t0
    H = MLA_HEADS
    return pl.pallas_call(
        functools.partial(_mla_attn_kernel, t0=t0, nct=nct, n_ctx=n_ctx),
        grid=(B, nq),
        in_specs=[
            pl.BlockSpec((1, TM, H * 256), lambda b, t: (b, t + t0, 0)),
            pl.BlockSpec((1, S, H * 256), lambda b, t: (b, 0, 0)),
            pl.BlockSpec((1, S, H * 128), lambda b, t: (b, 0, 0)),
        ],
        out_specs=pl.BlockSpec((1, TM, H * 128), lambda b, t: (b, t, 0)),
        out_shape=jax.ShapeDtypeStruct((B, nq * TM, H * 128), BF16),
        compiler_params=_cparams(("arbitrary", "arbitrary")),
        name="mla_attn",
    )(q, k, v)


NA_QR = 4
NA_KR = NA_WIN_R + NA_QR
NA_QB = NA_QR * GRID_W


def _na_kernel(p_ref, bias_ref, o_ref, *, n_ctx, rows, with_ctx):
    s_id = pl.program_id(1)
    W = GRID_W
    n_loc = NA_KR * W
    scale = NA_HEAD_DIM ** -0.5
    nq_ctx = n_ctx // NA_QB
    nblk = rows // NA_QR

    def heads_out(q, parts):
        outs = []
        for h in range(NA_HEADS):
            hs = slice(h * NA_HEAD_DIM, (h + 1) * NA_HEAD_DIM)
            qh = q[:, hs] * scale
            ss = []
            for kx, vx, bias in parts:
                s = lax.dot_general(qh, kx[:, hs], (((1,), (1,)), ((), ())), preferred_element_type=F32)
                if bias is not None:
                    s = s + bias(h)
                ss.append(s)
            m = ss[0].max(axis=-1, keepdims=True)
            for s in ss[1:]:
                m = jnp.maximum(m, s.max(axis=-1, keepdims=True))
            acc = 0.0
            l = 0.0
            for s, (kx, vx, bias) in zip(ss, parts):
                e = jnp.exp(s - m)
                l = l + jnp.sum(e, axis=-1, keepdims=True)
                acc = acc + _dot(e.astype(BF16), vx[:, hs])
            outs.append(acc / l)
        return jnp.concatenate(outs, axis=-1).astype(BF16)

    k_c = p_ref[0, 0:n_ctx, NA_DIM:2 * NA_DIM]
    v_c = p_ref[0, 0:n_ctx, 2 * NA_DIM:3 * NA_DIM]

    def lat_block(j):
        i0 = j * NA_QR
        k_start = jnp.clip(i0 - NA_WIN_R // 2, 0, rows - NA_KR)
        pat = jnp.where(j == 0, 0, jnp.where(j == nblk - 1, 2, 1))
        q0 = pl.multiple_of(n_ctx + i0 * W, NA_QB)
        k0 = pl.multiple_of(n_ctx + k_start * W, W)
        q = p_ref[0, pl.ds(q0, NA_QB), 0:NA_DIM]
        k_l = p_ref[0, pl.ds(k0, n_loc), NA_DIM:2 * NA_DIM]
        v_l = p_ref[0, pl.ds(k0, n_loc), 2 * NA_DIM:3 * NA_DIM]
        o_ref[0] = heads_out(q, [(k_l, v_l, lambda h: bias_ref[pat, h]), (k_c, v_c, None)])

    if with_ctx:
        @pl.when(s_id < nq_ctx)
        def _():
            q0 = pl.multiple_of(s_id * NA_QB, NA_QB)
            q = p_ref[0, pl.ds(q0, NA_QB), 0:NA_DIM]
            o_ref[0] = heads_out(q, [(k_c, v_c, None)])

        @pl.when(s_id >= nq_ctx)
        def _():
            lat_block(s_id - nq_ctx)
    else:
        lat_block(s_id)


def _na_attn(p_na, bias_tab, n_ctx, with_ctx):
    B, S, C = p_na.shape
    T = S - n_ctx
    rows = T // GRID_W
    assert rows % NA_QR == 0 and rows >= NA_KR and n_ctx % NA_QB == 0
    nsteps = rows // NA_QR + (n_ctx // NA_QB if with_ctx else 0)
    return pl.pallas_call(
        functools.partial(_na_kernel, n_ctx=n_ctx, rows=rows, with_ctx=with_ctx),
        grid=(B, nsteps),
        in_specs=[
            pl.BlockSpec((1, S, C), lambda b, s: (b, 0, 0)),
            pl.BlockSpec(bias_tab.shape, lambda b, s: (0, 0, 0, 0)),
        ],
        out_specs=pl.BlockSpec((1, NA_QB, NA_DIM), lambda b, s: (b, s, 0)),
        out_shape=jax.ShapeDtypeStruct((B, nsteps * NA_QB, NA_DIM), BF16),
        compiler_params=_cparams(("arbitrary", "arbitrary")),
        name="na_attn",
    )(p_na, bias_tab)


def _na_bias_table(rpb):
    col = np.arange(GRID_W)
    c_start = np.clip(col - NA_WIN_C // 2, 0, GRID_W - NA_WIN_C)
    in_win = (col[None, :] >= c_start[:, None]) & (col[None, :] < c_start[:, None] + NA_WIN_C)
    dc_idx = np.clip(col[None, :] - col[:, None] + NA_WIN_C - 1, 0, 2 * NA_WIN_C - 2)
    qa = np.arange(NA_QR)[:, None]
    kc = np.arange(NA_KR)[None, :]
    row_ok, dr_idx = [], []
    for pat in range(3):
        off = (NA_WIN_R // 2) * pat
        first = (0 * qa, qa, 0 * qa + NA_WIN_R // 2)[pat]
        row_ok.append((kc >= first) & (kc < first + NA_WIN_R))
        dr_idx.append(np.clip(kc - qa - off + NA_WIN_R - 1, 0, 2 * NA_WIN_R - 2))
    sel_c = np.eye(2 * NA_WIN_C - 1, dtype=np.float32)[dc_idx]
    sel_r = np.eye(2 * NA_WIN_R - 1, dtype=np.float32)[np.stack(dr_idx)]
    hp = lax.Precision.HIGHEST
    t = jnp.einsum('hrs,qks->hrqk', rpb, sel_c, precision=hp)
    bias = jnp.einsum('pacr,hrqk->phaqck', sel_r, t, precision=hp)
    ok = jnp.asarray(np.stack(row_ok))[:, None, :, None, :, None] & jnp.asarray(in_win)[None, None, None, :, None, :]
    bias = jnp.where(ok, bias, NEG_INF)
    return bias.reshape(3, NA_HEADS, NA_QB, NA_KR * GRID_W).astype(F32)


def _outproj_kernel(orw_ref, bonus_ref, g_ref, mla_ref, na_ref, x_ref, mod_ref, gnw_ref, gnb_ref, ones_ref,
                    wout_ref, ln1g_ref, ln1b_ref, rwh_ref, rwl_ref, rb_ref, tri_ref,
                    x1_ref, h_ref, meta_ref, cnt_ref, base_s, *, dn_alpha):
    o = orw_ref[0] + orw_ref[1]
    inv_n = 1.0 / RW_HEAD_DIM
    mu = _group_sum(o, ones_ref) * inv_n
    dlt = o - mu
    var = _group_sum(dlt * dlt, ones_ref) * inv_n
    on = dlt * lax.rsqrt(var + RW_GN_EPS) * gnw_ref[...] + gnb_ref[...]
    rw_y = ((on + bonus_ref[0]) * g_ref[0]).astype(BF16)
    y = (_dot(rw_y, wout_ref[0:RW_DIM, :])
         + _dot(mla_ref[0], wout_ref[RW_DIM:RW_DIM + MLA_DIM, :])
         + _dot(na_ref[0], wout_ref[RW_DIM + MLA_DIM:, :]))
    m = mod_ref[0, 0]
    x1 = _ln(dn_alpha * x_ref[0] + m[2:3] * y) * ln1g_ref[...] + ln1b_ref[...]
    x1_ref[0] = x1
    h = _ln(x1) * (1.0 + m[4:5]) + m[3:4]
    h_ref[0] = _pack_bf16_pair(h)
    first = jnp.logical_and(pl.program_id(0) == 0, pl.program_id(1) == 0)
    meta_ref[0] = _route_tile(_dot3(h, rwh_ref[...], rwl_ref[...]) + rb_ref[...], tri_ref, base_s, first)
    cnt_ref[...] = base_s[...]


def _out_proj(o_rw, bonus, g, mla_o, na_o, xa, mod6, prm, t0, nct, dn_alpha):
    B, S, D = xa.shape
    nt = S // TM - t0
    So = nt * TM
    kind = lambda t: (t + t0 >= nct).astype(jnp.int32)
    full = lambda shape: pl.BlockSpec(shape, lambda b, t: (0,) * len(shape))
    off = lambda C: pl.BlockSpec((1, TM, C), lambda b, t: (b, t + t0, 0))
    own = lambda C: pl.BlockSpec((1, TM, C), lambda b, t: (b, t, 0))
    return pl.pallas_call(
        functools.partial(_outproj_kernel, dn_alpha=dn_alpha),
        grid=(B, nt),
        in_specs=[
            pl.BlockSpec((2, TM, RW_DIM), lambda b, t: (0, t + t0, b)),
            off(RW_DIM), off(RW_DIM), own(MLA_DIM), own(NA_DIM), off(D),
            pl.BlockSpec((1, 1, 6, D), lambda b, t: (b, kind(t), 0, 0)),
            full((1, RW_DIM)), full((1, RW_DIM)), full((RW_DIM, RW_DIM)),
            full((D, D)), full((1, D)), full((1, D)),
            full((D, 128)), full((D, 128)), full((1, 128)), full((TM, TM)),
        ],
        out_specs=[own(D), own(D // 2), own(128), full((1, 128))],
        out_shape=[
            jax.ShapeDtypeStruct((B, So, D), F32),
            jax.ShapeDtypeStruct((B, So, D // 2), jnp.int32),
            jax.ShapeDtypeStruct((B, So, 128), F32),
            jax.ShapeDtypeStruct((1, 128), F32),
        ],
        scratch_shapes=[pltpu.VMEM((1, 128), F32)],
        compiler_params=_cparams(("arbitrary", "arbitrary")),
        name="out_proj",
    )(o_rw, bonus, g, mla_o, na_o, xa, mod6, prm["gn_w"], prm["gn_b"], prm["ones"],
      prm["w_out"], prm["ln1_g"], prm["ln1_b"], prm["router_hi"], prm["router_lo"], prm["router_b"],
      jnp.asarray(np.tril(np.ones((TM, TM), np.float32), -1), BF16))


GU_CHUNK = 256


def _moe_kernel(be_ref, nu_ref, x_ref, wgu_ref, bgu_ref, wdn_ref, bdn_ref, perm_ref, y_ref,
                wgu_s, wdn_s, act_s):
    i = pl.program_id(0)
    F = wdn_ref.shape[2]
    n_chunks = 2 * F // GU_CHUNK
    half = GU_CHUNK // 2
    valid = i < nu_ref[0]
    e = be_ref[i]
    new_expert = jnp.logical_or(i == 0, e != be_ref[jnp.maximum(i - 1, 0)])

    @pl.when(jnp.logical_and(valid, new_expert))
    def _():
        for c in range(n_chunks):
            cs = slice(c * GU_CHUNK, (c + 1) * GU_CHUNK)
            wgu_s[:, cs] = _dot(wgu_ref[0, 0, :, cs].astype(BF16), perm_ref[...]).astype(BF16)
        wdn_s[...] = wdn_ref[0, 0].astype(BF16)

    @pl.when(valid)
    def _():
        gu = _dot(_unpack_bf16_pair(x_ref[...]).astype(BF16), wgu_s[...]) + bgu_ref[0]
        for c in range(n_chunks):
            glu = jnp.minimum(gu[:, c * GU_CHUNK:c * GU_CHUNK + half], SWIGLU_LIMIT)
            lin = jnp.clip(gu[:, c * GU_CHUNK + half:(c + 1) * GU_CHUNK], -SWIGLU_LIMIT, SWIGLU_LIMIT)
            act_s[:, c * half:(c + 1) * half] = (glu * _sigmoid(SWIGLU_ALPHA * glu) * (lin + 1.0)).astype(BF16)
        y = _dot(act_s[...], wdn_s[...]) + bdn_ref[0]
        y_ref[...] = _pack_bf16_pair(y)

    @pl.when(jnp.logical_not(valid))
    def _():
        y_ref[...] = jnp.zeros_like(y_ref)


def _moe_kernel_into(be_ref, nu_ref, x_ref, wgu_ref, bgu_ref, wdn_ref, bdn_ref, perm_ref, yprev_ref, y_ref,
                     wgu_s, wdn_s, act_s):
    del yprev_ref
    _moe_kernel(be_ref, nu_ref, x_ref, wgu_ref, bgu_ref, wdn_ref, bdn_ref, perm_ref, y_ref, wgu_s, wdn_s, act_s)


def _moe_ffn(x_part, block_e, n_used, w_gu, b_gu, w_dn, b_dn, layer, blk0, n_pad, y_prev):
    D = w_gu.shape[2]
    Dp = D // 2
    _, E, _, F2 = w_gu.shape
    F = F2 // 2
    n_blocks = x_part.shape[0] // MOE_BLK
    half = GU_CHUNK // 2
    src = np.concatenate([2 * np.arange(half), 2 * np.arange(half) + 1])
    perm = jnp.asarray(np.eye(GU_CHUNK, dtype=np.float32)[:, src], BF16)
    bgu_p = jnp.swapaxes(b_gu.reshape(E, F2 // GU_CHUNK, half, 2), 2, 3).reshape(E, 1, F2)
    grid_spec = pltpu.PrefetchScalarGridSpec(
        num_scalar_prefetch=2,
        grid=(n_blocks,),
        in_specs=[
            pl.BlockSpec((MOE_BLK, Dp), lambda i, be, nu: (i, 0)),
            pl.BlockSpec((1, 1, D, F2), lambda i, be, nu: (layer, be[i], 0, 0)),
            pl.BlockSpec((1, 1, F2), lambda i, be, nu: (be[i], 0, 0)),
            pl.BlockSpec((1, 1, F, D), lambda i, be, nu: (layer, be[i], 0, 0)),
            pl.BlockSpec((1, 1, D), lambda i, be, nu: (be[i], 0, 0)),
            pl.BlockSpec((GU_CHUNK, GU_CHUNK), lambda i, be, nu: (0, 0)),
        ] + ([] if y_prev is None else [pl.BlockSpec(memory_space=pl.ANY)]),
        out_specs=pl.BlockSpec((MOE_BLK, Dp), lambda i, be, nu: (i + blk0, 0)),
        scratch_shapes=[pltpu.VMEM((D, F2), BF16), pltpu.VMEM((F, D), BF16), pltpu.VMEM((MOE_BLK, F), BF16)],
    )
    args = (block_e, n_used, x_part, w_gu, bgu_p, w_dn, b_dn.reshape(E, 1, D), perm)
    return pl.pallas_call(
        _moe_kernel if y_prev is None else _moe_kernel_into,
        grid_spec=grid_spec,
        out_shape=jax.ShapeDtypeStruct((n_pad, Dp), jnp.int32),
        input_output_aliases={} if y_prev is None else {len(args): 0},
        compiler_params=_cparams(("arbitrary",)),
        name="moe_ffn",
    )(*args, *(() if y_prev is None else (y_prev,)))


SC_CORES = 2
SC_SUBCORES = 16
SC_WORKERS = SC_CORES * SC_SUBCORES
SC_ROWS = 64
SC_NBUF = 2


def _sc_gather(table, idx):
    V, W = table.shape
    n = idx.shape[0]
    assert n % (SC_WORKERS * SC_ROWS * SC_NBUF) == 0
    steps = n // (SC_WORKERS * SC_ROWS)
    idx3 = idx.reshape(SC_WORKERS, steps, SC_ROWS)
    mesh = plsc.VectorSubcoreMesh(core_axis_name="c", subcore_axis_name="s")

    @functools.partial(
        pl.kernel, mesh=mesh,
        out_type=jax.ShapeDtypeStruct((n, W), table.dtype),
        scratch_types=[pltpu.VMEM((steps, SC_ROWS), jnp.int32),
                       pltpu.VMEM((SC_NBUF, SC_ROWS, W), table.dtype),
                       pltpu.SemaphoreType.DMA((SC_NBUF,)),
                       pltpu.SemaphoreType.DMA((SC_NBUF,))],
        name="sc_row_gather",
    )
    def gather_kernel(table_hbm, idx_hbm, out_hbm, idx_v, rows_v, sem_g, sem_o):
        wid = lax.axis_index("s") * SC_CORES + lax.axis_index("c")
        step0 = wid * steps
        pltpu.sync_copy(idx_hbm.at[wid], idx_v)

        def gather(j, b):
            return pltpu.make_async_copy(table_hbm.at[idx_v.at[j + b]], rows_v.at[b], sem_g.at[b])

        def put(j, b):
            row0 = pl.multiple_of((step0 + j + b) * SC_ROWS, SC_ROWS)
            return pltpu.make_async_copy(rows_v.at[b], out_hbm.at[pl.ds(row0, SC_ROWS)], sem_o.at[b])

        @pl.loop(0, steps, step=SC_NBUF)
        def _(j):
            for b in range(SC_NBUF):
                gather(j, b).start()
            for b in range(SC_NBUF):
                gather(j, b).wait()
                put(j, b).start()
            for b in range(SC_NBUF):
                put(j, b).wait()

    return gather_kernel(table, idx3)


SC_SROWS = 128
SC_SWORDS = 128


def _sc_scatter(vals, idx, n_out):
    n, W = vals.shape
    assert W == SC_SWORDS and n % (SC_WORKERS * SC_SROWS) == 0
    steps = n // (SC_WORKERS * SC_SROWS)
    idx3 = idx.reshape(SC_WORKERS, steps, SC_SROWS)
    vals4 = vals.reshape(SC_WORKERS, steps, SC_SROWS, W)
    mesh = plsc.VectorSubcoreMesh(core_axis_name="c", subcore_axis_name="s")

    @functools.partial(
        pl.kernel, mesh=mesh,
        out_type=jax.ShapeDtypeStruct((n_out, W), vals.dtype),
        scratch_types=[pltpu.VMEM((steps, SC_SROWS), jnp.int32),
                       pltpu.VMEM((SC_SROWS, W), vals.dtype)],
        name="sc_row_scatter",
    )
    def scatter_kernel(vals_hbm, idx_hbm, out_hbm, idx_v, rows_v):
        wid = lax.axis_index("s") * SC_CORES + lax.axis_index("c")
        pltpu.sync_copy(idx_hbm.at[wid], idx_v)

        @pl.loop(0, steps)
        def _(j):
            pltpu.sync_copy(vals_hbm.at[wid, j], rows_v)
            pltpu.sync_copy(rows_v, out_hbm.at[idx_v.at[j]])

    return scatter_kernel(vals4, idx3)


def _route_tile(lg, tri_ref, base_s, first):
    @pl.when(first)
    def _():
        base_s[...] = jnp.zeros_like(base_s)

    lane = lax.broadcasted_iota(jnp.int32, lg.shape, 1)
    lg = jnp.where(lane < N_EXPERTS, lg, -jnp.inf)
    vals, ids, hots = [], [], []
    for _ in range(TOP_K):
        m = jnp.max(lg, axis=-1, keepdims=True)
        idx = jnp.min(jnp.where(lg == m, lane, lg.shape[1]), axis=-1, keepdims=True)
        hot = lane == idx
        vals.append(m)
        ids.append(idx)
        hots.append(hot)
        lg = jnp.where(hot, -jnp.inf, lg)
    es = [jnp.exp(v - vals[0]) for v in vals]
    den = es[0] + es[1] + es[2] + es[3]
    onehot = hots[0].astype(F32) + hots[1].astype(F32) + hots[2].astype(F32) + hots[3].astype(F32)
    before = base_s[...] + _dot(tri_ref[...], onehot.astype(BF16))
    meta = jnp.zeros(lg.shape, F32)
    for k in range(TOP_K):
        rank = jnp.sum(jnp.where(hots[k], before, 0.0), axis=-1, keepdims=True)
        meta = jnp.where(lane == k, ids[k].astype(F32), meta)
        meta = jnp.where(lane == TOP_K + k, rank, meta)
        meta = jnp.where(lane == 2 * TOP_K + k, es[k] / den, meta)
    base_s[...] = base_s[...] + jnp.sum(onehot, axis=0, keepdims=True)
    return meta


def _route(meta, cnt, n_tok):
    top_i = meta[:, 0:TOP_K].astype(jnp.int32)
    rank = meta[:, TOP_K:2 * TOP_K].astype(jnp.int32)
    gates = meta[:, 2 * TOP_K:3 * TOP_K]
    n_assign = n_tok * TOP_K
    counts = cnt[0, :N_EXPERTS].astype(jnp.int32)
    padded = (counts + MOE_BLK - 1) // MOE_BLK * MOE_BLK
    ends_p = jnp.cumsum(padded)
    g_start = ends_p - padded
    sel = top_i[:, :, None] == jnp.arange(N_EXPERTS, dtype=jnp.int32)[None, None, :]
    dest = jnp.sum(jnp.where(sel, g_start[None, None, :], 0), axis=-1) + rank
    n_blocks = -(-(n_assign + N_EXPERTS * (MOE_BLK - 1)) // MOE_BLK)
    n_pad = n_blocks * MOE_BLK
    blk_start = jnp.arange(n_blocks, dtype=jnp.int32) * MOE_BLK
    block_e = jnp.minimum(jnp.sum((ends_p[None, :] <= blk_start[:, None]).astype(jnp.int32), axis=1),
                          N_EXPERTS - 1).astype(jnp.int32)
    n_used = (ends_p[-1] // MOE_BLK).astype(jnp.int32).reshape(1)
    tok_of = jnp.broadcast_to(jnp.arange(n_tok, dtype=jnp.int32)[:, None, None], (n_tok, TOP_K, SC_SWORDS))
    slot_tok = _sc_scatter(tok_of.reshape(n_assign, SC_SWORDS), dest.reshape(-1), n_pad)[:, 0]
    tok_pad = jnp.clip(slot_tok, 0, n_tok - 1)
    return gates, dest, tok_pad, block_e, n_used


def _final_kernel(x1_ref, yk_ref, gate_ref, mod_ref, g_ref, b_ref, o_ref, *, dn_alpha):
    D = x1_ref.shape[2]
    m = mod_ref[0, 0]
    gates = gate_ref[0]
    f = _unpack_bf16_pair(yk_ref[0, 0]) * gates[:, 0:1]
    for k in range(1, TOP_K):
        f = f + _unpack_bf16_pair(yk_ref[k, 0]) * gates[:, k:k + 1]
    o_ref[0] = _ln(dn_alpha * x1_ref[0] + m[5:6] * f) * g_ref[...] + b_ref[...]


def _final(x1, yk, gates, mod6, ln_g, ln_b, t0, nct, dn_alpha):
    B, So, D = x1.shape
    kind = lambda t: (t + t0 >= nct).astype(jnp.int32)
    blk = pl.BlockSpec((1, TM, D), lambda b, t: (b, t, 0))
    vec = pl.BlockSpec((1, D), lambda b, t: (0, 0))
    return pl.pallas_call(
        functools.partial(_final_kernel, dn_alpha=dn_alpha),
        grid=(B, So // TM),
        in_specs=[blk,
                  pl.BlockSpec((TOP_K, 1, TM, D // 2), lambda b, t: (0, b, t, 0)),
                  pl.BlockSpec((1, TM, TOP_K), lambda b, t: (b, t, 0)),
                  pl.BlockSpec((1, 1, 6, D), lambda b, t: (b, kind(t), 0, 0)), vec, vec],
        out_specs=blk,
        out_shape=jax.ShapeDtypeStruct((B, So, D), F32),
        compiler_params=_cparams(("arbitrary", "arbitrary")),
        name="ffn_residual",
    )(x1, yk, gates, mod6, ln_g, ln_b)


def _rope_tables(n_ctx, T):
    t = jnp.arange(T)
    row = (t // GRID_W).astype(F32)
    col = (t % GRID_W).astype(F32)
    n_freq = MLA_ROPE // 4
    inv = ROPE_BASE ** (-jnp.arange(n_freq, dtype=F32) / n_freq)
    ang = jnp.concatenate([row[:, None] * inv, col[:, None] * inv], axis=-1)
    cos, sin = jnp.cos(ang), jnp.sin(ang)
    z = jnp.zeros((T, 128 - MLA_ROPE), F32)
    cos128 = jnp.concatenate([cos, cos, z], axis=-1)
    sin128 = jnp.concatenate([-sin, sin, z], axis=-1)
    cos_c = jnp.concatenate([jnp.ones((n_ctx, MLA_ROPE), F32), jnp.zeros((n_ctx, 128 - MLA_ROPE), F32)], -1)
    sin_c = jnp.zeros((n_ctx, 128), F32)
    return jnp.concatenate([cos_c, cos128], 0), jnp.concatenate([sin_c, sin128], 0)


def _rope_slabs(w):
    ev, od = w[:, 0::2], w[:, 1::2]
    z = jnp.zeros((w.shape[0], 128 - MLA_ROPE), w.dtype)
    return jnp.concatenate([ev, od, z, od, ev, z], axis=-1)


def _blockdiag2(m):
    z = jnp.zeros_like(m[0])
    return jnp.concatenate([jnp.concatenate([m[0], z], 1), jnp.concatenate([z, m[1]], 1)], 0)


def kernel(x, c, ctx, c_ctx, ada_w, ada_b, w_in, rw_mu, rw_w0, rw_w2, rw_a0, rw_a2, rw_g2, rw_kk, rw_ka, rw_rk, rw_gn_w, rw_gn_b, mla_q_norm, mla_kv_norm, mla_w_uq, mla_w_ukv, na_rpb, w_out, ln1_g, ln1_b, router_w, router_b, w_gu, b_gu, w_dn, b_dn, ln2_g, ln2_b):
    B, T, D = x.shape
    n_ctx = ctx.shape[1]
    depth = ada_w.shape[0]
    S = n_ctx + T
    assert n_ctx % TM == 0 and T % TM == 0 and T % GRID_W == 0 and T // GRID_W >= NA_WIN_R
    nct = n_ctx // TM
    BH = B * RW_HEADS
    dn_alpha = (2 * depth) ** 0.25
    F = w_dn.shape[2]

    R = (B + 1 + 7) // 8 * 8
    cond = jnp.zeros((R, D), F32).at[:B].set(c).at[B].set(c_ctx)
    mod = _ada_mod(cond, ada_w, ada_b)
    mod_l = mod[:, :B].reshape(depth, B, 1, 6, D)
    mod_c = jnp.broadcast_to(mod[:, B].reshape(depth, 1, 1, 6, D), (depth, B, 1, 6, D))
    mod6 = jnp.concatenate([mod_c, mod_l], axis=2)

    cos128, sin128 = _rope_tables(n_ctx, T)
    ones_blk = jnp.kron(jnp.eye(RW_HEADS, dtype=F32), jnp.ones((RW_HEAD_DIM, RW_HEAD_DIM), F32)).astype(BF16)

    na_bias = [_na_bias_table(na_rpb[l]) for l in range(depth)]

    xa = jnp.concatenate([ctx, x], axis=1)
    for l in range(depth):
        need_ctx = l < depth - 1
        t0 = 0 if need_ctx else nct

        wi = w_in[l]
        c_m = RW_COLS
        w_in_ext = jnp.concatenate(
            [wi[:, :c_m + MLA_Q_LORA + MLA_KV_LORA], _rope_slabs(wi[:, c_m + MLA_Q_LORA + MLA_KV_LORA:c_m + MLA_COLS]),
             wi[:, c_m + MLA_COLS:]], axis=-1).astype(BF16)
        wuq = mla_w_uq[l].reshape(MLA_Q_LORA, MLA_HEADS, MLA_NOPE + MLA_ROPE)
        wuq_ext = jnp.concatenate(
            [jnp.concatenate([wuq[:, h, :MLA_NOPE], _rope_slabs(wuq[:, h, MLA_NOPE:])], -1) for h in range(MLA_HEADS)],
            axis=-1).astype(BF16)
        rw_prm = dict(
            mu=rw_mu[l],
            w0=rw_w0[l].reshape(1, 2 * RW_DIM), w2=_blockdiag2(rw_w2[l]).astype(BF16),
            a0=rw_a0[l].reshape(1, 2 * RW_DIM), a2=_blockdiag2(rw_a2[l]).astype(BF16),
            g2=rw_g2[l].astype(BF16), kk=rw_kk[l].reshape(1, RW_DIM), rk=rw_rk[l].reshape(1, RW_DIM),
            ones=ones_blk)
        mla_prm = dict(q_norm=mla_q_norm[l].reshape(1, -1), kv_norm=mla_kv_norm[l].reshape(1, -1),
                       w_uq=wuq_ext, w_ukv=mla_w_ukv[l].astype(BF16))
        rt = jnp.zeros((D, 128), F32).at[:, :N_EXPERTS].set(router_w[l])
        rt_hi = rt.astype(BF16)
        out_prm = dict(gn_w=rw_gn_w[l].reshape(1, -1), gn_b=rw_gn_b[l].reshape(1, -1), ones=ones_blk,
                       w_out=w_out[l].astype(BF16), ln1_g=ln1_g[l].reshape(1, -1), ln1_b=ln1_b[l].reshape(1, -1),
                       router_hi=rt_hi, router_lo=(rt - rt_hi.astype(F32)).astype(BF16),
                       router_b=jnp.zeros((1, 128), F32).at[0, :N_EXPERTS].set(router_b[l]))

        p_rw, p_mla, p_na = _in_proj(xa, mod6[l], w_in_ext, nct)
        feat, g_gate, bonus = _rw_features(p_rw, rw_prm, nct)
        ft = jnp.swapaxes(feat.reshape(S, BH, 8 * RW_HEAD_DIM), 1, 2).reshape(S, 8, RW_HEAD_DIM, BH)
        ka_t = jnp.tile(rw_ka[l].reshape(RW_HEADS, RW_HEAD_DIM).T[:, None, :], (1, B, 1)).reshape(RW_HEAD_DIM, BH)
        q, k, v = _mla_prep(p_mla, cos128, sin128, mla_prm)
        mla_o = _mla_attn(q, k, v, t0, nct, n_ctx)
        na_o = _na_attn(p_na, na_bias[l], n_ctx, need_ctx)

        o_scan = _rw_scan(ft, ka_t, n_ctx, mla_o)
        o_rw = jnp.swapaxes(o_scan, 2, 3).reshape(2, S, B * RW_DIM)

        x1, h, meta, cnt = _out_proj(o_rw, bonus, g_gate, mla_o, na_o, xa, mod6[l], out_prm, t0, nct, dn_alpha)

        So = x1.shape[1]
        n_tok = B * So
        gates, dest, tok_pad, block_e, n_used = _route(meta.reshape(n_tok, 128), cnt, n_tok)
        n_pad = tok_pad.shape[0]
        n_blocks = n_pad // MOE_BLK
        parts = MOE_PARTS if n_blocks % MOE_PARTS == 0 else 1
        pb = n_blocks // parts
        h2 = h.reshape(n_tok, D // 2)
        y_sorted = None
        for c in range(parts):
            x_part = _sc_gather(h2, tok_pad[c * pb * MOE_BLK:(c + 1) * pb * MOE_BLK])
            y_sorted = _moe_ffn(x_part, block_e[c * pb:(c + 1) * pb], jnp.clip(n_used - c * pb, 0, pb),
                                w_gu, b_gu[l], w_dn, b_dn[l], l, c * pb, n_pad, y_sorted)
        yk = _sc_gather(y_sorted, dest.T.reshape(-1)).reshape(TOP_K, B, So, D // 2)

        xa = _final(x1, yk, gates.reshape(B, So, TOP_K), mod6[l], ln2_g[l].reshape(1, -1), ln2_b[l].reshape(1, -1),
                    t0, nct, dn_alpha)
    return xa
```
